```python
import math
import jax
import jax.numpy as jnp
from jax import lax
import numpy as np

D_MODEL = 4096
BATCH = 2
SEQ = 8192
DEPTH = 2

MEM_TOKENS = 256
MIX_WIDTH = D_MODEL // 4
N_BRANCH = 3
MLSTM_HEADS = 4
MLSTM_DV = MIX_WIDTH // MLSTM_HEADS
MLSTM_DK = MLSTM_DV // 2
GLA_HEADS = 4
GLA_DV = MIX_WIDTH // GLA_HEADS
GLA_DK = GLA_DV // 2
GLA_RANK = 16
GLA_TAU = 16.0
HYENA_CH = MIX_WIDTH
HYENA_ORDER = 2
HYENA_SHORT = 3
HYENA_BANDS = 16
HYENA_EMB = 1 + 2 * HYENA_BANDS
HYENA_HID = 64
HYENA_MIN_DECAY = math.log(1e-2) / 1.5
HYENA_MAX_DECAY = math.log(1e-2) / 0.3
HYENA_FILTER_SCALE = 0.005
CHUNK = 64
XATTN_HEADS = 4
XATTN_HD = D_MODEL // XATTN_HEADS
D_FF = 4 * D_MODEL
EPS = 1e-6

SEG_SIZES = (
    MLSTM_HEADS * MLSTM_DK,
    MLSTM_HEADS * MLSTM_DK,
    MIX_WIDTH,
    MIX_WIDTH,
    4 * MLSTM_HEADS,
    GLA_HEADS * GLA_DK,
    GLA_HEADS * GLA_DK,
    MIX_WIDTH,
    MIX_WIDTH,
    2 * GLA_RANK,
    3 * HYENA_CH,
    N_BRANCH * D_MODEL,
)
SEG_OFFSETS = tuple(sum(SEG_SIZES[:i + 1]) for i in range(len(SEG_SIZES) - 1))
IN_WIDTH = sum(SEG_SIZES)
MLSTM_GATE_START = SEG_OFFSETS[3]

kernel_name = 'hybrid_mlstm_gla_hyena_encoder'


def rmsnorm(x, g):
    xf = x.astype(jnp.float32)
    r = lax.rsqrt(jnp.mean(xf * xf, axis=-1, keepdims=True) + EPS)
    return (xf * r).astype(x.dtype) * g


def head_rmsnorm(h, g):
    B, L, H, d = h.shape
    return rmsnorm(h, g.reshape(H, d)).reshape(B, L, H * d)


def flip(a):
    return jnp.flip(a, axis=1)


def to_chunks(a):
    B, L, H, d = a.shape
    return a.reshape(B, L // CHUNK, CHUNK, H, d).transpose(1, 0, 3, 2, 4)


def from_chunks(a):
    NC, B, H, C, d = a.shape
    return a.transpose(1, 0, 3, 2, 4).reshape(B, NC * C, H, d)


def mlstm_scan(q, k, v, i_pre, f_pre):
    B, L, H, dk = q.shape
    dv = v.shape[-1]
    qc, kc, vc = to_chunks(q), to_chunks(k), to_chunks(v)
    ic = to_chunks(i_pre[..., None])[..., 0]
    lfc = to_chunks(jax.nn.log_sigmoid(f_pre)[..., None])[..., 0]
    tri = jnp.tril(jnp.ones((CHUNK, CHUNK), dtype=bool))

    def step(carry, xs):
        C, n, m = carry
        qj, ks, vs, ig, lf = xs
        b = jnp.cumsum(lf, axis=-1)
        g = b[..., -1]
        logD = jnp.where(tri, b[..., :, None] - b[..., None, :] + ig[..., None, :], -jnp.inf)
        m_inter = b + m[..., None]
        m_j = jnp.maximum(m_inter, jnp.max(logD, axis=-1))
        s = jnp.einsum('bhjd,bhsd->bhjs', qj, ks) * jnp.exp(logD - m_j[..., None])
        w_inter = jnp.exp(m_inter - m_j)
        num = jnp.einsum('bhjs,bhsv->bhjv', s, vs) + w_inter[..., None] * jnp.einsum('bhjd,bhdv->bhjv', qj, C)
        den = jnp.sum(s, axis=-1) + w_inter * jnp.einsum('bhjd,bhd->bhj', qj, n)
        h = num / jnp.maximum(jnp.abs(den), jnp.exp(-m_j))[..., None]
        a = g[..., None] - b + ig
        m_new = jnp.maximum(g + m, jnp.max(a, axis=-1))
        wk = jnp.exp(a - m_new[..., None])
        decay = jnp.exp(g + m - m_new)
        C = decay[..., None, None] * C + jnp.einsum('bhs,bhsd,bhsv->bhdv', wk, ks, vs)
        n = decay[..., None] * n + jnp.einsum('bhs,bhsd->bhd', wk, ks)
        return (C, n, m_new), h

    init = (jnp.zeros((B, H, dk, dv), jnp.float32),
            jnp.zeros((B, H, dk), jnp.float32),
            jnp.zeros((B, H), jnp.float32))
    _, h = lax.scan(step, init, (qc, kc, vc, ic, lfc))
    return from_chunks(h)


def gla_scan(q, k, v, log_a):
    B, L, H, dk = q.shape
    dv = v.shape[-1]
    qc, kc, vc, ac = to_chunks(q), to_chunks(k), to_chunks(v), to_chunks(log_a)
    tri = jnp.tril(jnp.ones((CHUNK, CHUNK), dtype=bool))[:, :, None]

    def step(S, xs):
        qj, ks, vs, la = xs
        b = jnp.cumsum(la, axis=-2)
        rel = jnp.where(tri, b[..., :, None, :] - b[..., None, :, :], -jnp.inf)
        att = jnp.einsum('bhjd,bhsd,bhjsd->bhjs', qj, ks, jnp.exp(rel))
        o = jnp.einsum('bhjs,bhsv->bhjv', att, vs) + jnp.einsum('bhjd,bhdv->bhjv', qj * jnp.exp(b), S)
        g = b[..., -1, :]
        S = jnp.exp(g)[..., None] * S + jnp.einsum('bhsd,bhsv->bhdv', ks * jnp.exp(g[..., None, :] - b), vs)
        return S, o

    _, o = lax.scan(step, jnp.zeros((B, H, dk, dv), jnp.float32), (qc, kc, vc, ac))
    return from_chunks(o)


def mlstm_branch(q_in, k_in, v_in, o_in, gate_in, head_norm):
    B, L, _ = q_in.shape
    H = MLSTM_HEADS
    f32 = jnp.float32
    q = q_in.astype(f32).reshape(B, L, H, MLSTM_DK) * MLSTM_DK ** -0.5
    k = k_in.astype(f32).reshape(B, L, H, MLSTM_DK)
    v = v_in.astype(f32).reshape(B, L, H, MLSTM_DV)
    gts = gate_in.astype(f32).reshape(B, L, 4, H)
    h_fwd = mlstm_scan(q, k, v, gts[:, :, 0], gts[:, :, 1])
    h_bwd = flip(mlstm_scan(flip(q), flip(k), flip(v), flip(gts[:, :, 2]), flip(gts[:, :, 3])))
    h = head_rmsnorm(h_fwd + h_bwd, head_norm).astype(o_in.dtype)
    return h * jax.nn.sigmoid(o_in)


def gla_branch(q_in, k_in, v_in, g_in, lr_in, decay_up, decay_bias, head_norm):
    B, L, _ = q_in.shape
    H = GLA_HEADS
    f32 = jnp.float32
    q = q_in.astype(f32).reshape(B, L, H, GLA_DK) * GLA_DK ** -0.5
    k = k_in.astype(f32).reshape(B, L, H, GLA_DK)
    v = v_in.astype(f32).reshape(B, L, H, GLA_DV)
    lr = lr_in.reshape(B, L, 2, GLA_RANK)
    z = (jnp.einsum('blzr,zrk->blzk', lr, decay_up) + decay_bias).astype(f32)
    log_a = jax.nn.log_sigmoid(z) / GLA_TAU
    la_fwd = log_a[:, :, 0].reshape(B, L, H, GLA_DK)
    la_bwd = log_a[:, :, 1].reshape(B, L, H, GLA_DK)
    o = gla_scan(q, k, v, la_fwd) + flip(gla_scan(flip(q), flip(k), flip(v), flip(la_bwd)))
    o = head_rmsnorm(o, head_norm).astype(g_in.dtype)
    return o * jax.nn.silu(g_in)


def short_conv(u, w, b):
    L = u.shape[1]
    pad = HYENA_SHORT // 2
    up = jnp.pad(u, ((0, 0), (pad, pad), (0, 0)))
    return sum(up[:, j:j + L] * w[j] for j in range(HYENA_SHORT)) + b


def hyena_filters(L, w1, b1, w2, b2, w3, freq):
    f32 = jnp.float32
    n = jnp.arange(L, dtype=f32)
    t = n / (L - 1)
    bands = jnp.linspace(1e-4, HYENA_BANDS - 1, HYENA_BANDS, dtype=f32)
    ang = (2.0 * math.pi * n / L)[:, None] * bands[None, :]
    z = jnp.concatenate([t[:, None], jnp.cos(ang), -jnp.sin(ang)], axis=-1)
    h = jnp.sin(freq * (z @ w1 + b1))
    h = jnp.sin(freq * (h @ w2 + b2))
    h = (h @ w3).astype(f32).reshape(L, HYENA_ORDER, 2, HYENA_CH)
    deltas = jnp.abs(jnp.linspace(HYENA_MIN_DECAY, HYENA_MAX_DECAY, HYENA_CH, dtype=f32))
    window = jnp.exp(-t[:, None] * deltas[None, :])
    return h * window[:, None, None, :]


def fft_conv_bidir(u, h_fwd, h_bwd, skip):
    L = u.shape[1]
    k = jnp.concatenate([h_fwd, jnp.zeros_like(h_fwd[:1]), h_bwd[:0:-1]], axis=0)
    uf = jnp.fft.rfft(u, n=2 * L, axis=1)
    kf = jnp.fft.rfft(k, n=2 * L, axis=0)
    y = jnp.fft.irfft(uf * kf[None], n=2 * L, axis=1)[:, :L]
    return y + skip * u


def hyena_branch(hy_in, conv_w, conv_b, w1, b1, w2, b2, w3, freq, skip):
    L = hy_in.shape[1]
    u = short_conv(hy_in, conv_w, conv_b).astype(jnp.float32)
    v, x1, x2 = jnp.split(u, 3, axis=-1)
    filt = hyena_filters(L, w1, b1, w2, b2, w3, freq)
    skip = skip.astype(jnp.float32)
    z = v
    for o, gate in enumerate((x1, x2)):
        z = gate * fft_conv_bidir(z, filt[:, o, 0], filt[:, o, 1], skip[o])
    return z.astype(hy_in.dtype)


def mixer_block(xn, w_in, b_in, mlstm_head_norm, gla_decay_up, gla_decay_bias, gla_head_norm,
                hyena_conv_w, hyena_conv_b, hyena_ffn_w1, hyena_ffn_b1, hyena_ffn_w2, hyena_ffn_b2,
                hyena_ffn_w3, hyena_freq, hyena_skip, w_branch, w_out):
    B, L, _ = xn.shape
    proj = xn @ w_in + b_in
    (mq, mk, mv, mo, mg, gq, gk, gv, gg, glr, hy, gates) = jnp.split(proj, SEG_OFFSETS, axis=-1)
    y_a = mlstm_branch(mq, mk, mv, mo, mg, mlstm_head_norm)
    y_b = gla_branch(gq, gk, gv, gg, glr, gla_decay_up, gla_decay_bias, gla_head_norm)
    y_c = hyena_branch(hy, hyena_conv_w, hyena_conv_b, hyena_ffn_w1, hyena_ffn_b1, hyena_ffn_w2,
                       hyena_ffn_b2, hyena_ffn_w3, hyena_freq, hyena_skip)
    branches = jnp.stack([y_a, y_b, y_c], axis=2)
    up = jnp.einsum('blnc,ncd->blnd', branches, w_branch)
    gate = jax.nn.sigmoid(gates.reshape(B, L, N_BRANCH, D_MODEL))
    merged = jnp.sum(up * gate, axis=2)
    return merged @ w_out


def cross_attention(xn, memn, wq, wkv, wo):
    B, L, _ = xn.shape
    M = memn.shape[1]
    q = (xn @ wq).reshape(B, L, XATTN_HEADS, XATTN_HD)
    kv = memn @ wkv
    k, v = jnp.split(kv, 2, axis=-1)
    k = k.reshape(B, M, XATTN_HEADS, XATTN_HD)
    v = v.reshape(B, M, XATTN_HEADS, XATTN_HD)
    s = jnp.einsum('blhd,bmhd->bhlm', q, k).astype(jnp.float32) * XATTN_HD ** -0.5
    p = jax.nn.softmax(s, axis=-1).astype(v.dtype)
    o = jnp.einsum('bhlm,bmhd->blhd', p, v).reshape(B, L, D_MODEL)
    return o @ wo


def squared_relu_mlp(xn, w1, w2):
    return jnp.square(jax.nn.relu(xn @ w1)) @ w2


def setup_inputs(seed: int = 0) -> dict:
    key = jax.random.key(seed)
    ks = jax.random.split(key, 32)
    f32 = jnp.float32
    D, H = D_MODEL, MLSTM_HEADS

    def nrm(k, shape, scale):
        return jax.random.normal(k, shape, f32) * scale

    x = nrm(ks[0], (BATCH, SEQ, D), 1.0)
    mem = nrm(ks[1], (BATCH, MEM_TOKENS, D), 1.0)
    norm_gains = 1.0 + nrm(ks[2], (DEPTH, 4, D), 0.02)
    final_norm = 1.0 + nrm(ks[3], (D,), 0.02)
    w_in = nrm(ks[4], (DEPTH, D, IN_WIDTH), D ** -0.5)
    fbias = jnp.linspace(3.0, 6.0, H, dtype=f32)
    g0 = MLSTM_GATE_START
    b_in = nrm(ks[5], (DEPTH, IN_WIDTH), 0.02)
    b_in = b_in.at[:, g0 + H:g0 + 2 * H].add(fbias).at[:, g0 + 3 * H:g0 + 4 * H].add(fbias)
    mlstm_head_norm = 1.0 + nrm(ks[6], (DEPTH, MIX_WIDTH), 0.02)
    gla_decay_up = nrm(ks[7], (DEPTH, 2, GLA_RANK, GLA_HEADS * GLA_DK), GLA_RANK ** -0.5)
    gla_decay_bias = nrm(ks[8], (DEPTH, 2, GLA_HEADS * GLA_DK), 0.1)
    gla_head_norm = 1.0 + nrm(ks[9], (DEPTH, MIX_WIDTH), 0.02)
    hyena_conv_w = nrm(ks[10], (DEPTH, HYENA_SHORT, 3 * HYENA_CH), HYENA_SHORT ** -0.5)
    hyena_conv_b = nrm(ks[11], (DEPTH, 3 * HYENA_CH), 0.02)
    hyena_ffn_w1 = nrm(ks[12], (DEPTH, HYENA_EMB, HYENA_HID), HYENA_EMB ** -0.5)
    hyena_ffn_b1 = nrm(ks[13], (DEPTH, HYENA_HID), 0.1)
    hyena_ffn_w2 = nrm(ks[14], (DEPTH, HYENA_HID, HYENA_HID), HYENA_HID ** -0.5)
    hyena_ffn_b2 = nrm(ks[15], (DEPTH, HYENA_HID), 0.1)
    hyena_ffn_w3 = nrm(ks[16], (DEPTH, HYENA_HID, HYENA_ORDER * 2 * HYENA_CH), HYENA_FILTER_SCALE)
    hyena_freq = 1.0 + nrm(ks[17], (DEPTH, HYENA_HID), 0.02)
    hyena_skip = nrm(ks[18], (DEPTH, HYENA_ORDER, HYENA_CH), 1.0)
    w_branch = nrm(ks[19], (DEPTH, N_BRANCH, MIX_WIDTH, D), MIX_WIDTH ** -0.5)
    w_out = nrm(ks[20], (DEPTH, D, D), D ** -0.5)
    xattn_wq = nrm(ks[21], (DEPTH, D, D), D ** -0.5)
    xattn_wkv = nrm(ks[22], (DEPTH, D, 2 * D), D ** -0.5)
    xattn_wo = nrm(ks[23], (DEPTH, D, D), D ** -0.5)
    mlp_w1 = nrm(ks[24], (DEPTH, D, D_FF), D ** -0.5)
    mlp_w2 = nrm(ks[25], (DEPTH, D_FF, D), D_FF ** -0.5)
    return {'x': x, 'mem': mem, 'norm_gains': norm_gains, 'final_norm': final_norm,
            'w_in': w_in, 'b_in': b_in, 'mlstm_head_norm': mlstm_head_norm,
            'gla_decay_up': gla_decay_up, 'gla_decay_bias': gla_decay_bias, 'gla_head_norm': gla_head_norm,
            'hyena_conv_w': hyena_conv_w, 'hyena_conv_b': hyena_conv_b,
            'hyena_ffn_w1': hyena_ffn_w1, 'hyena_ffn_b1': hyena_ffn_b1,
            'hyena_ffn_w2': hyena_ffn_w2, 'hyena_ffn_b2': hyena_ffn_b2,
            'hyena_ffn_w3': hyena_ffn_w3, 'hyena_freq': hyena_freq, 'hyena_skip': hyena_skip,
            'w_branch': w_branch, 'w_out': w_out,
            'xattn_wq': xattn_wq, 'xattn_wkv': xattn_wkv, 'xattn_wo': xattn_wo,
            'mlp_w1': mlp_w1, 'mlp_w2': mlp_w2}


def reference(x, mem, norm_gains, final_norm, w_in, b_in, mlstm_head_norm,
              gla_decay_up, gla_decay_bias, gla_head_norm,
              hyena_conv_w, hyena_conv_b, hyena_ffn_w1, hyena_ffn_b1,
              hyena_ffn_w2, hyena_ffn_b2, hyena_ffn_w3, hyena_freq, hyena_skip,
              w_branch, w_out, xattn_wq, xattn_wkv, xattn_wo, mlp_w1, mlp_w2):
    h = x
    for l in range(DEPTH):
        g = norm_gains[l]
        h = h + mixer_block(rmsnorm(h, g[0]), w_in[l], b_in[l], mlstm_head_norm[l],
                            gla_decay_up[l], gla_decay_bias[l], gla_head_norm[l],
                            hyena_conv_w[l], hyena_conv_b[l], hyena_ffn_w1[l], hyena_ffn_b1[l],
                            hyena_ffn_w2[l], hyena_ffn_b2[l], hyena_ffn_w3[l], hyena_freq[l],
                            hyena_skip[l], w_branch[l], w_out[l])
        h = h + cross_attention(rmsnorm(h, g[1]), rmsnorm(mem, g[2]),
                                xattn_wq[l], xattn_wkv[l], xattn_wo[l])
        h = h + squared_relu_mlp(rmsnorm(h, g[3]), mlp_w1[l], mlp_w2[l])
    return rmsnorm(h, final_norm)
```

```python
import functools
import math

import numpy as np
import jax
import jax.numpy as jnp
from jax import lax
from jax.experimental import pallas as pl
from jax.experimental.pallas import tpu as pltpu

F32 = jnp.float32
BF16 = jnp.bfloat16

DEPTH = 2
N_BRANCH = 3
MLSTM_HEADS = 4
GLA_HEADS = 4
GLA_RANK = 16
GLA_TAU = 16.0
HYENA_ORDER = 2
HYENA_BANDS = 16
HYENA_MIN_DECAY = math.log(1e-2) / 1.5
HYENA_MAX_DECAY = math.log(1e-2) / 0.3
XATTN_HEADS = 4
EPS = 1e-6

LANES = 128
VMEM_LIMIT_BYTES = 56 * 1024 * 1024

SCAN_CHUNK = 128
FFT_N2 = 128
NEG_BIG = -1e30


def _params(sem):
    return pltpu.CompilerParams(dimension_semantics=sem, vmem_limit_bytes=VMEM_LIMIT_BYTES)


def _log_sigmoid(x):
    return -(jnp.maximum(-x, 0.0) + jnp.log(1.0 + jnp.exp(-jnp.abs(x))))


def _split_bf16(a, parts):
    out = []
    r = a
    for _ in range(parts):
        p = r.astype(BF16)
        out.append(p)
        r = r - p.astype(F32)
    return out


def _dot(a, b):
    return jnp.dot(a, b, preferred_element_type=F32)


def _dot_nt(a, b):
    return lax.dot_general(a, b, (((1,), (1,)), ((), ())), preferred_element_type=F32)


def _dot_tn(a, b):
    return lax.dot_general(a, b, (((0,), (0,)), ((), ())), preferred_element_type=F32)


def _dot3(a, b):
    a_hi, a_lo = _split_bf16(a, 2)
    b_hi, b_lo = _split_bf16(b, 2)
    return _dot(a_hi, b_hi) + _dot(a_hi, b_lo) + _dot(a_lo, b_hi)


def _rmsnorm_kernel(x_ref, g_ref, o_ref):
    x = x_ref[...].astype(F32)
    r = lax.rsqrt(jnp.mean(x * x, axis=-1, keepdims=True) + EPS)
    o_ref[...] = ((x * r) * g_ref[...]).astype(o_ref.dtype)


def _rmsnorm(x, g, out_dtype, tm=256):
    m, d = x.shape
    tm = min(tm, m)
    return pl.pallas_call(
        _rmsnorm_kernel,
        out_shape=jax.ShapeDtypeStruct((m, d), out_dtype),
        grid=(m // tm,),
        in_specs=[pl.BlockSpec((tm, d), lambda i: (i, 0)),
                  pl.BlockSpec((1, d), lambda i: (0, 0))],
        out_specs=pl.BlockSpec((tm, d), lambda i: (i, 0)),
        compiler_params=_params(("parallel",)),
        name="rmsnorm",
    )(x, g.reshape(1, d).astype(F32))


def _mm_kernel(*refs, nk, act, has_bias, has_resid):
    x_ref, w_ref = refs[0], refs[1]
    idx = 2
    b_ref = r_ref = None
    if has_bias:
        b_ref = refs[idx]
        idx += 1
    if has_resid:
        r_ref = refs[idx]
        idx += 1
    o_ref = refs[idx]

    def epilogue(acc):
        if has_bias:
            acc = acc + b_ref[...]
        if act == "relu2":
            acc = jnp.square(jnp.maximum(acc, 0.0))
        if has_resid:
            acc = acc + r_ref[...]
        o_ref[...] = acc.astype(o_ref.dtype)

    if nk == 1:
        epilogue(_dot(x_ref[...], w_ref[...]))
    else:
        acc_ref = refs[idx + 1]
        k = pl.program_id(2)

        @pl.when(k == 0)
        def _():
            acc_ref[...] = jnp.zeros_like(acc_ref)

        acc_ref[...] += _dot(x_ref[...], w_ref[...])

        @pl.when(k == nk - 1)
        def _():
            epilogue(acc_ref[...])


def _mm(x, w, *, bias=None, resid=None, act=None, out_dtype=BF16, tm=1024, tn=512, tk=None, name="mm"):
    m, kdim = x.shape
    n = w.shape[1]
    tm, tn = min(tm, m), min(tn, n)
    tk = kdim if tk is None else min(tk, kdim)
    nk = kdim // tk
    assert m % tm == 0 and n % tn == 0 and kdim % tk == 0
    in_specs = [pl.BlockSpec((tm, tk), lambda i, j, k: (i, k)),
                pl.BlockSpec((tk, tn), lambda i, j, k: (k, j))]
    args = [x, w]
    if bias is not None:
        in_specs.append(pl.BlockSpec((1, tn), lambda i, j, k: (0, j)))
        args.append(bias.reshape(1, n).astype(F32))
    if resid is not None:
        in_specs.append(pl.BlockSpec((tm, tn), lambda i, j, k: (i, j)))
        args.append(resid)
    scratch = [pltpu.VMEM((tm, tn), F32)] if nk > 1 else []
    return pl.pallas_call(
        functools.partial(_mm_kernel, nk=nk, act=act, has_bias=bias is not None,
                          has_resid=resid is not None),
        out_shape=jax.ShapeDtypeStruct((m, n), out_dtype),
        grid=(m // tm, n // tn, nk),
        in_specs=in_specs,
        out_specs=pl.BlockSpec((tm, tn), lambda i, j, k: (i, j)),
        scratch_shapes=scratch,
        compiler_params=_params(("parallel", "parallel", "arbitrary")),
        name=name,
    )(*args)


def _mlstm_kernel(q_ref, k_ref, v_ref, g_ref, o_ref, c_ref, m_ref, *, nc, heads, dv, scale):
    lc = SCAN_CHUNK
    s = pl.program_id(0)
    c = pl.program_id(1)
    rev = (s // heads) % 2
    cc = c + rev * (nc - 1 - 2 * c)

    @pl.when(c == 0)
    def _():
        c_ref[...] = jnp.zeros_like(c_ref)
        m_ref[...] = jnp.zeros_like(m_ref)

    row = lax.broadcasted_iota(jnp.int32, (lc, lc), 0)
    col = lax.broadcasted_iota(jnp.int32, (lc, lc), 1)
    tri = ((col - row) * (1 - 2 * rev)) <= 0
    eye = row == col

    gl = g_ref[cc]
    ig = gl[0:1, :]
    lf = _log_sigmoid(gl[1:2, :])
    b_col = jnp.sum(jnp.where(tri, jnp.broadcast_to(lf, (lc, lc)), 0.0), axis=1, keepdims=True)
    b_row = jnp.sum(jnp.where(eye, b_col, 0.0), axis=0, keepdims=True)
    g = jnp.sum(lf, axis=1, keepdims=True)
    m_prev = m_ref[...]

    log_d = jnp.where(tri, b_col - b_row + ig, NEG_BIG)
    m_inter = b_col + m_prev
    m_j = jnp.maximum(m_inter, jnp.max(log_d, axis=1, keepdims=True))
    p = jnp.exp(log_d - m_j)

    q = q_ref[...]
    k = k_ref[...]
    v = v_ref[...]
    ones_blk = (lax.broadcasted_iota(jnp.int32, (lc, LANES), 1) == 0).astype(BF16)
    v_aug = jnp.concatenate([v, ones_blk], axis=1)
    smat = (_dot_nt(q, k) * p).astype(BF16)
    w_inter = jnp.exp(m_inter - m_j)
    num_aug = (_dot(smat, v_aug) + w_inter * _dot(q, c_ref[...].astype(BF16))) * scale
    num = num_aug[:, :dv]
    den = num_aug[:, dv:dv + 1]
    o_ref[...] = (num / jnp.maximum(jnp.abs(den), jnp.exp(-m_j))).astype(o_ref.dtype)

    a_row = g - b_row + ig
    m_new = jnp.maximum(g + m_prev, jnp.max(a_row, axis=1, keepdims=True))
    wk_row = jnp.exp(a_row - m_new)
    wk_col = jnp.sum(jnp.where(eye, jnp.broadcast_to(wk_row, (lc, lc)), 0.0), axis=1, keepdims=True)
    decay = jnp.exp(g + m_prev - m_new)
    kw = (k.astype(F32) * wk_col).astype(BF16)
    c_ref[...] = decay * c_ref[...] + _dot_tn(kw, v_aug)
    m_ref[...] = m_new


def _mlstm(proj, gates, *, batch, seq, q_off, k_off, v_off, heads, dk, dv):
    lc = SCAN_CHUNK
    nc = seq // lc
    assert dk == lc and seq % lc == 0 and dv % LANES == 0

    def rowblk(s, c):
        rev = (s // heads) % 2
        return (s // (2 * heads)) * nc + c + rev * (nc - 1 - 2 * c)

    return pl.pallas_call(
        functools.partial(_mlstm_kernel, nc=nc, heads=heads, dv=dv, scale=dk ** -0.5),
        out_shape=jax.ShapeDtypeStruct((2, batch * seq, heads * dv), F32),
        grid=(batch * 2 * heads, nc),
        in_specs=[
            pl.BlockSpec((lc, dk), lambda s, c: (rowblk(s, c), q_off // dk + s % heads)),
            pl.BlockSpec((lc, dk), lambda s, c: (rowblk(s, c), k_off // dk + s % heads)),
            pl.BlockSpec((lc, dv), lambda s, c: (rowblk(s, c), v_off // dv + s % heads)),
            pl.BlockSpec((None, nc, 2, lc), lambda s, c: (s, 0, 0, 0)),
        ],
        out_specs=pl.BlockSpec((None, lc, dv), lambda s, c: ((s // heads) % 2, rowblk(s, c), s % heads)),
        scratch_shapes=[pltpu.VMEM((dk, dv + LANES), F32), pltpu.VMEM((1, 1), F32)],
        compiler_params=_params(("parallel", "arbitrary")),
        name="mlstm_scan",
    )(proj, proj, proj, gates)


def _gla_levels():
    lc = SCAN_CHUNK
    nlev = int(math.log2(lc)) + 1
    t = np.arange(lc)
    diff = np.zeros((2, nlev * lc, lc), np.float32)
    mask = np.zeros((2, nlev, lc, lc), np.float32)
    u = t[None, :]
    for d in range(2):
        cum = (u <= t[:, None]) if d == 0 else (u >= t[:, None])
        diff[d, :lc] = cum
        mask[d, 0] = np.eye(lc)
        for l in range(1, nlev):
            w = 1 << (l - 1)
            pair = t // (2 * w)
            second = (t % (2 * w)) >= w
            if d == 0:
                bd = pair * 2 * w + w - 1
                cum_bd = u <= bd[:, None]
                is_q, is_k = second, ~second
            else:
                bd = pair * 2 * w + w
                cum_bd = u >= bd[:, None]
                is_q, is_k = ~second, second
            diff[d, l * lc:(l + 1) * lc] = cum.astype(np.float32) - cum_bd.astype(np.float32)
            mask[d, l] = ((pair[:, None] == pair[None, :]) & is_q[:, None] & is_k[None, :])
    return diff, mask


def _gla_kernel(q_ref, k_ref, v_ref, sm_ref, up_ref, bias_ref, d_ref, mask_ref, o_ref, s_ref, *, nlev, scale):
    lc = SCAN_CHUNK
    c = pl.program_id(1)

    @pl.when(c == 0)
    def _():
        s_ref[...] = jnp.zeros_like(s_ref)

    z = _dot3(sm_ref[...], up_ref[...]) + bias_ref[...]
    la = _log_sigmoid(z) * (1.0 / GLA_TAU)
    dmat = d_ref[...]
    x = sum(_dot(dmat, piece) for piece in _split_bf16(la, 3))
    b = x[:lc]
    g_row = jnp.sum(la, axis=0, keepdims=True)

    qb = q_ref[...]
    kb = k_ref[...]
    v = v_ref[...]
    q = qb.astype(F32)
    k = kb.astype(F32)
    att = mask_ref[0] * _dot_nt(qb, kb)
    for l in range(1, nlev):
        e = jnp.exp(-jnp.abs(x[l * lc:(l + 1) * lc]))
        att = att + mask_ref[l] * _dot_nt((q * e).astype(BF16), (k * e).astype(BF16))

    o = _dot(att.astype(BF16), v) + _dot((q * jnp.exp(b)).astype(BF16), s_ref[...].astype(BF16))
    o_ref[...] = (o * scale).astype(o_ref.dtype)

    row = lax.broadcasted_iota(jnp.int32, (lc, lc), 0)
    col = lax.broadcasted_iota(jnp.int32, (lc, lc), 1)
    eg_col = jnp.sum(jnp.where(row == col, jnp.broadcast_to(jnp.exp(g_row), (lc, lc)), 0.0),
                     axis=1, keepdims=True)
    kd = (k * jnp.exp(g_row - b)).astype(BF16)
    s_ref[...] = eg_col * s_ref[...] + _dot_tn(kd, v)


def _gla(proj, small, up_pad, bias, *, batch, seq, q_off, k_off, v_off, heads, dk, dv):
    lc = SCAN_CHUNK
    nc = seq // lc
    assert dk == lc and seq % lc == 0
    diff, mask = _gla_levels()
    nlev = mask.shape[1]

    def rowblk(s, c):
        rev = (s // heads) % 2
        return (s // (2 * heads)) * nc + c + rev * (nc - 1 - 2 * c)

    return pl.pallas_call(
        functools.partial(_gla_kernel, nlev=nlev, scale=dk ** -0.5),
        out_shape=jax.ShapeDtypeStruct((2, batch * seq, heads * dv), F32),
        grid=(batch * 2 * heads, nc),
        in_specs=[
            pl.BlockSpec((lc, dk), lambda s, c: (rowblk(s, c), q_off // dk + s % heads)),
            pl.BlockSpec((lc, dk), lambda s, c: (rowblk(s, c), k_off // dk + s % heads)),
            pl.BlockSpec((lc, dv), lambda s, c: (rowblk(s, c), v_off // dv + s % heads)),
            pl.BlockSpec((lc, LANES), lambda s, c: (rowblk(s, c), 0)),
            pl.BlockSpec((None, LANES, dk), lambda s, c: (s % (2 * heads), 0, 0)),
            pl.BlockSpec((None, 1, dk), lambda s, c: (s % (2 * heads), 0, 0)),
            pl.BlockSpec((None, nlev * lc, lc), lambda s, c: ((s // heads) % 2, 0, 0)),
            pl.BlockSpec((None, nlev, lc, lc), lambda s, c: ((s // heads) % 2, 0, 0, 0)),
        ],
        out_specs=pl.BlockSpec((None, lc, dv), lambda s, c: ((s // heads) % 2, rowblk(s, c), s % heads)),
        scratch_shapes=[pltpu.VMEM((dk, dv), F32)],
        compiler_params=_params(("parallel", "arbitrary")),
        name="gla_scan",
    )(proj, proj, proj, small, up_pad, bias, jnp.asarray(diff, BF16), jnp.asarray(mask, F32))


def _headnorm_gate_kernel(h_ref, g_ref, hn_ref, o_ref, *, heads, dv, act):
    h = h_ref[0] + h_ref[1]
    outs = []
    for hh in range(heads):
        blk = h[:, hh * dv:(hh + 1) * dv]
        r = lax.rsqrt(jnp.mean(blk * blk, axis=-1, keepdims=True) + EPS)
        outs.append(blk * r)
    hn = jnp.concatenate(outs, axis=1) * hn_ref[...]
    gate = g_ref[...].astype(F32)
    sig = 1.0 / (1.0 + jnp.exp(-gate))
    gate = sig if act == "sigmoid" else gate * sig
    o_ref[...] = (hn * gate).astype(o_ref.dtype)


def _headnorm_gate(h2, proj, gate_off, head_norm, *, heads, dv, act, tm=512):
    _, m, width = h2.shape
    tm = min(tm, m)
    return pl.pallas_call(
        functools.partial(_headnorm_gate_kernel, heads=heads, dv=dv, act=act),
        out_shape=jax.ShapeDtypeStruct((m, width), BF16),
        grid=(m // tm,),
        in_specs=[pl.BlockSpec((2, tm, width), lambda i: (0, i, 0)),
                  pl.BlockSpec((tm, width), lambda i: (i, gate_off // width)),
                  pl.BlockSpec((1, width), lambda i: (0, 0))],
        out_specs=pl.BlockSpec((tm, width), lambda i: (i, 0)),
        compiler_params=_params(("parallel",)),
        name="headnorm_gate",
    )(h2, proj, head_norm.reshape(1, width).astype(F32))


def _short_conv_kernel(x0_ref, x1_ref, x2_ref, w_ref, b_ref, o0_ref, o1_ref, o2_ref):
    seq = x0_ref.shape[0]
    t = lax.broadcasted_iota(jnp.int32, x0_ref.shape, 0)
    for p, (x_ref, o_ref) in enumerate(((x0_ref, o0_ref), (x1_ref, o1_ref), (x2_ref, o2_ref))):
        x = x_ref[...].astype(F32)
        prev = jnp.where(t == 0, 0.0, pltpu.roll(x, 1, axis=0))
        nxt = jnp.where(t == seq - 1, 0.0, pltpu.roll(x, seq - 1, axis=0))
        w = w_ref[p]
        o_ref[...] = (prev * w[0:1] + x * w[1:2] + nxt * w[2:3] + b_ref[p]).astype(o_ref.dtype)


def _short_conv(proj, conv_w, conv_b, *, batch, seq, off, ch):
    tc = LANES
    nct = ch // tc
    w = conv_w.reshape(3, 3, nct, tc).transpose(1, 2, 0, 3).astype(F32)
    b = conv_b.reshape(3, nct, 1, tc).astype(F32)
    in_specs = [pl.BlockSpec((seq, tc), functools.partial(lambda b_, j, p: (b_, off // tc + p * nct + j), p=p))
                for p in range(3)]
    in_specs += [pl.BlockSpec((3, None, 3, tc), lambda b_, j: (0, j, 0, 0)),
                 pl.BlockSpec((3, None, 1, tc), lambda b_, j: (0, j, 0, 0))]
    out = jax.ShapeDtypeStruct((batch * seq, ch), BF16)
    return pl.pallas_call(
        _short_conv_kernel,
        out_shape=[out, out, out],
        grid=(batch, nct),
        in_specs=in_specs,
        out_specs=[pl.BlockSpec((seq, tc), lambda b_, j: (b_, j))] * 3,
        compiler_params=_params(("parallel", "parallel")),
        name="hyena_short_conv",
    )(proj, proj, proj, w, b)


def _filter_kernel(z_ref, w1_ref, b1_ref, w2_ref, b2_ref, w3_ref, f_ref, win_ref, o_ref, *, groups, ch):
    f = f_ref[...]
    h = jnp.sin(f * (_dot3(z_ref[...], w1_ref[...]) + b1_ref[...]))
    h = jnp.sin(f * (_dot3(h, w2_ref[...]) + b2_ref[...]))
    out = _dot3(h, w3_ref[...])
    win = win_ref[...]
    for gidx in range(groups):
        o_ref[:, gidx * ch:(gidx + 1) * ch] = out[:, gidx * ch:(gidx + 1) * ch] * win


def _hyena_filters(seq, w1, b1, w2, b2, w3, freq, ch, tl=512):
    emb, hid = w1.shape
    n = jnp.arange(seq, dtype=F32)
    t = n / (seq - 1)
    bands = jnp.linspace(1e-4, HYENA_BANDS - 1, HYENA_BANDS, dtype=F32)
    ang = (2.0 * math.pi * n / seq)[:, None] * bands[None, :]
    z = jnp.concatenate([t[:, None], jnp.cos(ang), -jnp.sin(ang)], axis=-1)
    z = jnp.pad(z, ((0, 0), (0, LANES - emb)))
    deltas = jnp.abs(jnp.linspace(HYENA_MIN_DECAY, HYENA_MAX_DECAY, ch, dtype=F32))
    window = jnp.exp(-t[:, None] * deltas[None, :])
    pad_h = LANES - hid
    w1p = jnp.pad(w1.astype(F32), ((0, LANES - emb), (0, pad_h)))
    w2p = jnp.pad(w2.astype(F32), ((0, pad_h), (0, pad_h)))
    w3p = jnp.pad(w3.astype(F32), ((0, pad_h), (0, 0)))
    row = lambda a: jnp.pad(a.astype(F32), (0, pad_h)).reshape(1, LANES)
    groups = w3.shape[1] // ch
    tl = min(tl, seq)
    full = lambda shape: pl.BlockSpec(shape, lambda i: (0, 0))
    return pl.pallas_call(
        functools.partial(_filter_kernel, groups=groups, ch=ch),
        out_shape=jax.ShapeDtypeStruct((seq, groups * ch), F32),
        grid=(seq // tl,),
        in_specs=[pl.BlockSpec((tl, LANES), lambda i: (i, 0)),
                  full((LANES, LANES)), full((1, LANES)), full((LANES, LANES)), full((1, LANES)),
                  full((LANES, groups * ch)), full((1, LANES)),
                  pl.BlockSpec((tl, ch), lambda i: (i, 0))],
        out_specs=pl.BlockSpec((tl, groups * ch), lambda i: (i, 0)),
        compiler_params=_params(("parallel",)),
        name="hyena_filters",
    )(z, w1p, row(b1), w2p, row(b2), w3p, row(freq), window)


def _fft_tables(n1, n2):
    n = n1 * n2
    a = jnp.arange(n1, dtype=jnp.int32)
    ang1 = ((a[:, None] * a[None, :]) % n1).astype(F32) * (2.0 * math.pi / n1)
    f1r, f1i = jnp.cos(ang1), -jnp.sin(ang1)
    c = jnp.arange(n1, dtype=jnp.int32)[:, None, None]
    d = jnp.arange(n2, dtype=jnp.int32)[None, :, None]
    b = jnp.arange(n2, dtype=jnp.int32)[None, None, :]
    ang = ((b * (c + n1 * d)) % n).astype(F32) * (2.0 * math.pi / n)
    gr, gi = jnp.cos(ang), -jnp.sin(ang)
    gblk = jnp.concatenate([jnp.concatenate([gr, -gi], axis=2),
                            jnp.concatenate([gi, gr], axis=2)], axis=1)
    return f1r, f1i, gblk.astype(BF16)


def _fft1_kernel(f_ref, u_ref, o_ref):
    o_ref[...] = _dot(f_ref[...], u_ref[...]).astype(o_ref.dtype)


def _fft_stage1(fstack, u, tl=2048):
    nb, kin, lanes = u.shape
    rows = fstack.shape[0]
    tl = min(tl, lanes)
    return pl.pallas_call(
        _fft1_kernel,
        out_shape=jax.ShapeDtypeStruct((nb, rows, lanes), BF16),
        grid=(nb, lanes // tl),
        in_specs=[pl.BlockSpec((rows, kin), lambda b, j: (0, 0)),
                  pl.BlockSpec((None, kin, tl), lambda b, j: (b, 0, j))],
        out_specs=pl.BlockSpec((None, rows, tl), lambda b, j: (b, 0, j)),
        compiler_params=_params(("parallel", "parallel")),
        name="fft_stage1",
    )(fstack, u)


def _fft_spec_kernel(a_ref, g_ref, o_ref, *, n2, inv_n):
    a = a_ref[...].reshape(2 * n2, a_ref.shape[-1])
    o_ref[...] = _dot(g_ref[...], a) * inv_n


def _fft_spectrum(a, gblk, *, n1, n2, ch, ct=512):
    a5 = a.reshape(2, n1, n2, ch)
    ct = min(ct, ch)
    return pl.pallas_call(
        functools.partial(_fft_spec_kernel, n2=n2, inv_n=1.0 / (n1 * n2)),
        out_shape=jax.ShapeDtypeStruct((n1, 2 * n2, ch), F32),
        grid=(ch // ct, n1),
        in_specs=[pl.BlockSpec((2, None, n2, ct), lambda j, c: (0, c, 0, j)),
                  pl.BlockSpec((None, 2 * n2, 2 * n2), lambda j, c: (c, 0, 0))],
        out_specs=pl.BlockSpec((None, 2 * n2, ct), lambda j, c: (c, 0, j)),
        compiler_params=_params(("parallel", "parallel")),
        name="fft_filter_spectrum",
    )(a5, gblk)


def _fft_mid_kernel(a_ref, g_ref, kf_ref, o_ref, *, n2):
    ct = a_ref.shape[-1]
    g = g_ref[...]
    x = _dot(g, a_ref[...].reshape(2 * n2, ct))
    xr, xi = x[:n2], x[n2:]
    kr, ki = kf_ref[:n2, :], kf_ref[n2:, :]
    y = jnp.concatenate([xr * kr - xi * ki, xr * ki + xi * kr], axis=0).astype(BF16)
    z = _dot_tn(g, y)
    o_ref[...] = z.reshape(2, n2, ct).astype(o_ref.dtype)


def _fft_mid(a, gblk, kf, *, nb, n1, n2, ch, kf_off, ct=512):
    a5 = a.reshape(nb, 2, n1, n2, ch)
    ct = min(ct, ch)
    return pl.pallas_call(
        functools.partial(_fft_mid_kernel, n2=n2),
        out_shape=jax.ShapeDtypeStruct((nb, n1, 2, n2, ch), BF16),
        grid=(nb, ch // ct, n1),
        in_specs=[pl.BlockSpec((None, 2, None, n2, ct), lambda b, j, c: (b, 0, c, 0, j)),
                  pl.BlockSpec((None, 2 * n2, 2 * n2), lambda b, j, c: (c, 0, 0)),
                  pl.BlockSpec((None, 2 * n2, ct), lambda b, j, c: (c, 0, kf_off // ct + j))],
        out_specs=pl.BlockSpec((None, None, 2, n2, ct), lambda b, j, c: (b, c, 0, 0, j)),
        compiler_params=_params(("parallel", "parallel", "parallel")),
        name="fft_spectral",
    )(a5, gblk, kf)


def _fft3_kernel(fr_ref, fi_ref, zr_ref, zi_ref, u_ref, gate_ref, skip_ref, o_ref):
    y = _dot(fr_ref[...], zr_ref[...]) + _dot(fi_ref[...], zi_ref[...])
    y = y + skip_ref[...] * u_ref[...].astype(F32)
    o_ref[...] = (gate_ref[...].astype(F32) * y).astype(o_ref.dtype)


def _fft_stage3(f1r_h, f1i_h, z, u, gate, skip, *, nb, n1, n2, ch, tl=2048):
    lanes = n2 * ch
    tl = min(tl, lanes)
    assert tl % ch == 0 or ch % tl == 0
    nl = lanes // tl
    zf = z.reshape(nb, n1, 2 * lanes)
    reps = max(tl // ch, 1)
    skip_t = jnp.tile(skip.astype(F32), reps).reshape(1, -1)
    nskip = skip_t.shape[1] // tl
    half = n1 // 2
    return pl.pallas_call(
        _fft3_kernel,
        out_shape=jax.ShapeDtypeStruct((nb, half, lanes), BF16),
        grid=(nb, nl),
        in_specs=[pl.BlockSpec((half, n1), lambda b, j: (0, 0)),
                  pl.BlockSpec((half, n1), lambda b, j: (0, 0)),
                  pl.BlockSpec((None, n1, tl), lambda b, j: (b, 0, j)),
                  pl.BlockSpec((None, n1, tl), lambda b, j: (b, 0, nl + j)),
                  pl.BlockSpec((None, half, tl), lambda b, j: (b, 0, j)),
                  pl.BlockSpec((None, half, tl), lambda b, j: (b, 0, j)),
                  pl.BlockSpec((1, tl), lambda b, j: (0, j % nskip))],
        out_specs=pl.BlockSpec((None, half, tl), lambda b, j: (b, 0, j)),
        compiler_params=_params(("parallel", "parallel")),
        name="fft_stage3",
    )(f1r_h, f1i_h, zf, zf, u, gate, skip_t)


def _hyena(proj, hy_off, conv_w, conv_b, w1, b1, w2, b2, w3, freq, skip, *, batch, seq, ch):
    n2 = FFT_N2
    n1 = 2 * seq // n2
    half = n1 // 2
    lanes = n2 * ch
    v, x1, x2 = _short_conv(proj, conv_w, conv_b, batch=batch, seq=seq, off=hy_off, ch=ch)
    filt = _hyena_filters(seq, w1, b1, w2, b2, w3, freq, ch)
    filt = filt.reshape(seq, HYENA_ORDER, 2, ch)
    hf = filt[:, :, 0, :].reshape(seq, HYENA_ORDER * ch)
    hb = filt[:, :, 1, :].reshape(seq, HYENA_ORDER * ch)
    kern = jnp.concatenate([hf, jnp.zeros_like(hf[:1]), hb[:0:-1]], axis=0).astype(BF16)
    f1r, f1i, gblk = _fft_tables(n1, n2)
    fstack = jnp.concatenate([f1r, f1i], axis=0)
    ka = _fft_stage1(fstack.astype(BF16), kern.reshape(1, n1, n2 * HYENA_ORDER * ch))
    kf = _fft_spectrum(ka[0], gblk, n1=n1, n2=n2, ch=HYENA_ORDER * ch)
    fdata = fstack[:, :half].astype(BF16)
    f1r_h, f1i_h = f1r[:half].astype(BF16), f1i[:half].astype(BF16)
    view = lambda a: a.reshape(batch, half, lanes)
    zcur = view(v)
    for o, gate in enumerate((x1, x2)):
        a = _fft_stage1(fdata, zcur)
        zmid = _fft_mid(a, gblk, kf, nb=batch, n1=n1, n2=n2, ch=ch, kf_off=o * ch)
        zcur = _fft_stage3(f1r_h, f1i_h, zmid, zcur, view(gate), skip[o], nb=batch, n1=n1, n2=n2, ch=ch)
    return zcur.reshape(batch * seq, ch)


def _merge_kernel(ya_ref, yb_ref, yc_ref, ga_ref, gb_ref, gc_ref, w_ref, o_ref):
    acc = None
    for n, (y_ref, g_ref) in enumerate(((ya_ref, ga_ref), (yb_ref, gb_ref), (yc_ref, gc_ref))):
        gate = 1.0 / (1.0 + jnp.exp(-g_ref[...].astype(F32)))
        term = gate * _dot(y_ref[...], w_ref[n])
        acc = term if acc is None else acc + term
    o_ref[...] = acc.astype(o_ref.dtype)


def _merge(ys, proj, gate_off, w_branch, *, d_model, tm=1024, tn=512):
    m, width = ys[0].shape
    tm, tn = min(tm, m), min(tn, d_model)
    y_spec = pl.BlockSpec((tm, width), lambda i, j: (i, 0))
    g_specs = [pl.BlockSpec((tm, tn), functools.partial(
        lambda i, j, n: (i, (gate_off + n * d_model) // tn + j), n=n)) for n in range(N_BRANCH)]
    return pl.pallas_call(
        _merge_kernel,
        out_shape=jax.ShapeDtypeStruct((m, d_model), BF16),
        grid=(m // tm, d_model // tn),
        in_specs=[y_spec, y_spec, y_spec] + g_specs +
                 [pl.BlockSpec((N_BRANCH, width, tn), lambda i, j: (0, 0, j))],
        out_specs=pl.BlockSpec((tm, tn), lambda i, j: (i, j)),
        compiler_params=_params(("parallel", "parallel")),
        name="branch_merge",
    )(*ys, proj, proj, proj, w_branch)


def _xattn_kernel(q_ref, k_ref, v_ref, o_ref, *, scale):
    s = _dot_nt(q_ref[...], k_ref[...]) * scale
    s = s - jnp.max(s, axis=-1, keepdims=True)
    p = jnp.exp(s)
    l = jnp.sum(p, axis=-1, keepdims=True)
    o_ref[...] = (_dot(p.astype(BF16), v_ref[...]) / l).astype(o_ref.dtype)


def _xattn(q, kv, *, batch, seq, mem, heads, hd, tq=512):
    tq = min(tq, seq)
    nq = seq // tq
    return pl.pallas_call(
        functools.partial(_xattn_kernel, scale=hd ** -0.5),
        out_shape=jax.ShapeDtypeStruct(q.shape, BF16),
        grid=(batch, heads, nq),
        in_specs=[pl.BlockSpec((tq, hd), lambda b, h, i: (b * nq + i, h)),
                  pl.BlockSpec((mem, hd), lambda b, h, i: (b, h)),
                  pl.BlockSpec((mem, hd), lambda b, h, i: (b, heads + h))],
        out_specs=pl.BlockSpec((tq, hd), lambda b, h, i: (b * nq + i, h)),
        compiler_params=_params(("parallel", "parallel", "parallel")),
        name="xattn_core",
    )(q, kv, kv)


def _layer(h, memf, g, w_in, b_in, mlstm_head_norm, gla_decay_up, gla_decay_bias, gla_head_norm,
           hyena_conv_w, hyena_conv_b, hyena_ffn_w1, hyena_ffn_b1, hyena_ffn_w2, hyena_ffn_b2,
           hyena_ffn_w3, hyena_freq, hyena_skip, w_branch, w_out, xattn_wq, xattn_wkv, xattn_wo,
           mlp_w1, mlp_w2, *, batch, seq, mem):
    d = h.shape[1]
    mix = d // 4
    mh, gh = MLSTM_HEADS, GLA_HEADS
    m_dv, g_dv = mix // mh, mix // gh
    m_dk, g_dk = m_dv // 2, g_dv // 2
    sizes = (mh * m_dk, mh * m_dk, mix, mix, 4 * mh, gh * g_dk, gh * g_dk, mix, mix, 2 * GLA_RANK,
             3 * mix, N_BRANCH * d)
    offs = np.concatenate([[0], np.cumsum(sizes)])
    seg = lambda a, i: a[..., offs[i]:offs[i + 1]]
    big = (0, 1, 2, 3, 5, 6, 7, 8, 10, 11)
    new_off = {}
    pos = 0
    for i in big:
        new_off[i] = pos
        pos += sizes[i]
    w_main = jnp.concatenate([seg(w_in, i) for i in big], axis=1).astype(BF16)
    b_main = jnp.concatenate([seg(b_in, i) for i in big], axis=0)
    n_small = sizes[4] + sizes[9]
    w_small = jnp.pad(jnp.concatenate([seg(w_in, 4), seg(w_in, 9)], axis=1),
                      ((0, 0), (0, LANES - n_small))).astype(BF16)
    b_small = jnp.pad(jnp.concatenate([seg(b_in, 4), seg(b_in, 9)], axis=0), (0, LANES - n_small))

    xn = _rmsnorm(h, g[0], BF16)
    proj = _mm(xn, w_main, bias=b_main, out_dtype=BF16, tn=1024, name="in_proj")
    small = _mm(xn, w_small, bias=b_small, out_dtype=F32, name="in_proj_gates")

    nc = seq // SCAN_CHUNK
    mg = small[:, :4 * mh].reshape(batch, nc, SCAN_CHUNK, 2, 2, mh)
    mgates = mg.transpose(0, 3, 5, 1, 4, 2).reshape(batch * 2 * mh, nc, 2, SCAN_CHUNK)
    h_m = _mlstm(proj, mgates, batch=batch, seq=seq, q_off=new_off[0], k_off=new_off[1], v_off=new_off[2],
                 heads=mh, dk=m_dk, dv=m_dv)
    y_a = _headnorm_gate(h_m, proj, new_off[3], mlstm_head_norm, heads=mh, dv=m_dv, act="sigmoid")

    up = gla_decay_up.reshape(2, GLA_RANK, gh, g_dk).transpose(0, 2, 1, 3).astype(F32)
    up_pad = jnp.zeros((2, gh, LANES, g_dk), F32)
    for dr in range(2):
        r0 = 4 * mh + dr * GLA_RANK
        up_pad = up_pad.at[dr, :, r0:r0 + GLA_RANK, :].set(up[dr])
    up_pad = up_pad.reshape(2 * gh, LANES, g_dk)
    gbias = gla_decay_bias.reshape(2 * gh, 1, g_dk).astype(F32)
    h_g = _gla(proj, small, up_pad, gbias, batch=batch, seq=seq, q_off=new_off[5], k_off=new_off[6],
               v_off=new_off[7], heads=gh, dk=g_dk, dv=g_dv)
    y_b = _headnorm_gate(h_g, proj, new_off[8], gla_head_norm, heads=gh, dv=g_dv, act="silu")

    y_c = _hyena(proj, new_off[10], hyena_conv_w, hyena_conv_b, hyena_ffn_w1, hyena_ffn_b1,
                 hyena_ffn_w2, hyena_ffn_b2, hyena_ffn_w3, hyena_freq, hyena_skip,
                 batch=batch, seq=seq, ch=mix)

    merged = _merge((y_a, y_b, y_c), proj, new_off[11], w_branch.astype(BF16), d_model=d)
    h = _mm(merged, w_out.astype(BF16), resid=h, out_dtype=F32, name="mixer_out")

    xn = _rmsnorm(h, g[1], BF16)
    memn = _rmsnorm(memf, g[2], BF16)
    q = _mm(xn, xattn_wq.astype(BF16), out_dtype=BF16, tn=1024, name="xattn_q")
    kv = _mm(memn, xattn_wkv.astype(BF16), out_dtype=BF16, tn=1024, name="xattn_kv")
    o = _xattn(q, kv, batch=batch, seq=seq, mem=mem, heads=XATTN_HEADS, hd=d // XATTN_HEADS)
    h = _mm(o, xattn_wo.astype(BF16), resid=h, out_dtype=F32, name="xattn_out")

    xn = _rmsnorm(h, g[3], BF16)
    hid = _mm(xn, mlp_w1.astype(BF16), act="relu2", out_dtype=BF16, tn=1024, name="mlp_up")
    h = _mm(hid, mlp_w2.astype(BF16), resid=h, out_dtype=F32, tk=4096, name="mlp_down")
    return h


def kernel(x, mem, norm_gains, final_norm, w_in, b_in, mlstm_head_norm, gla_decay_up, gla_decay_bias, gla_head_norm, hyena_conv_w, hyena_conv_b, hyena_ffn_w1, hyena_ffn_b1, hyena_ffn_w2, hyena_ffn_b2, hyena_ffn_w3, hyena_freq, hyena_skip, w_branch, w_out, xattn_wq, xattn_wkv, xattn_wo, mlp_w1, mlp_w2):
    batch, seq, d = x.shape
    mem_tokens = mem.shape[1]
    h = x.reshape(batch * seq, d)
    memf = mem.reshape(batch * mem_tokens, d)
    per_layer = (w_in, b_in, mlstm_head_norm, gla_decay_up, gla_decay_bias, gla_head_norm,
                 hyena_conv_w, hyena_conv_b, hyena_ffn_w1, hyena_ffn_b1, hyena_ffn_w2, hyena_ffn_b2,
                 hyena_ffn_w3, hyena_freq, hyena_skip, w_branch, w_out, xattn_wq, xattn_wkv, xattn_wo,
                 mlp_w1, mlp_w2)
    for l in range(norm_gains.shape[0]):
        h = _layer(h, memf, norm_gains[l], *(p[l] for p in per_layer),
                   batch=batch, seq=seq, mem=mem_tokens)
    return _rmsnorm(h, final_norm, x.dtype).reshape(batch, seq, d)
```

```python
import functools
import math

import numpy as np
import jax
import jax.numpy as jnp
from jax import lax
from jax.experimental import pallas as pl
from jax.experimental.pallas import tpu as pltpu

F32 = jnp.float32
BF16 = jnp.bfloat16

N_BRANCH = 3
MLSTM_HEADS = 4
GLA_HEADS = 4
GLA_RANK = 16
GLA_TAU = 16.0
HYENA_ORDER = 2
HYENA_BANDS = 16
HYENA_MIN_DECAY = math.log(1e-2) / 1.5
HYENA_MAX_DECAY = math.log(1e-2) / 0.3
XATTN_HEADS = 4
EPS = 1e-6

LANES = 128
BF16_TILE_ROWS = 16
VMEM_LIMIT_BYTES = 56 * 1024 * 1024

SCAN_CHUNK = 128
FFT_N2 = 128
NEG_BIG = -1e30


def _params(sem):
    return pltpu.CompilerParams(dimension_semantics=sem, vmem_limit_bytes=VMEM_LIMIT_BYTES)


def _log_sigmoid(x):
    return -(jnp.maximum(-x, 0.0) + jnp.log(1.0 + jnp.exp(-jnp.abs(x))))


def _split_bf16(a, parts):
    out = []
    r = a
    for _ in range(parts):
        p = r.astype(BF16)
        out.append(p)
        r = r - p.astype(F32)
    return out


def _dot(a, b):
    return jnp.dot(a, b, preferred_element_type=F32)


def _dot_nt(a, b):
    return lax.dot_general(a, b, (((1,), (1,)), ((), ())), preferred_element_type=F32)


def _dot_tn(a, b):
    return lax.dot_general(a, b, (((0,), (0,)), ((), ())), preferred_element_type=F32)


def _dot3(a, b):
    a_hi, a_lo = _split_bf16(a, 2)
    b_hi, b_lo = _split_bf16(b, 2)
    return _dot(a_hi, b_hi) + _dot(a_hi, b_lo) + _dot(a_lo, b_hi)


def _rmsnorm_kernel(x_ref, g_ref, o_ref):
    x = x_ref[...].astype(F32)
    r = lax.rsqrt(jnp.mean(x * x, axis=-1, keepdims=True) + EPS)
    o_ref[...] = ((x * r) * g_ref[...]).astype(o_ref.dtype)


def _rmsnorm(x, g, out_dtype, tm=256):
    m, d = x.shape
    tm = min(tm, m)
    return pl.pallas_call(
        _rmsnorm_kernel,
        out_shape=jax.ShapeDtypeStruct((m, d), out_dtype),
        grid=(m // tm,),
        in_specs=[pl.BlockSpec((tm, d), lambda i: (i, 0)),
                  pl.BlockSpec((1, d), lambda i: (0, 0))],
        out_specs=pl.BlockSpec((tm, d), lambda i: (i, 0)),
        compiler_params=_params(("parallel",)),
        name="rmsnorm",
    )(x, g.reshape(1, d).astype(F32))


def _mm_kernel(*refs, nk, act, has_bias, has_resid):
    x_ref, w_ref = refs[0], refs[1]
    idx = 2
    b_ref = r_ref = None
    if has_bias:
        b_ref = refs[idx]
        idx += 1
    if has_resid:
        r_ref = refs[idx]
        idx += 1
    o_ref = refs[idx]

    def epilogue(acc):
        if has_bias:
            acc = acc + b_ref[...]
        if act == "relu2":
            acc = jnp.square(jnp.maximum(acc, 0.0))
        if has_resid:
            acc = acc + r_ref[...]
        o_ref[...] = acc.astype(o_ref.dtype)

    if nk == 1:
        epilogue(_dot(x_ref[...], w_ref[...]))
    else:
        acc_ref = refs[idx + 1]
        k = pl.program_id(2)

        @pl.when(k == 0)
        def _():
            acc_ref[...] = jnp.zeros_like(acc_ref)

        acc_ref[...] += _dot(x_ref[...], w_ref[...])

        @pl.when(k == nk - 1)
        def _():
            epilogue(acc_ref[...])


def _mm(x, w, layer, *, bias=None, resid=None, act=None, out_dtype=BF16, tm=1024, tn=512, tk=None, name="mm"):
    m, kdim = x.shape
    n = w.shape[2]
    tm, tn = min(tm, m), min(tn, n)
    tk = kdim if tk is None else min(tk, kdim)
    nk = kdim // tk
    assert m % tm == 0 and n % tn == 0 and kdim % tk == 0
    in_specs = [pl.BlockSpec((tm, tk), lambda i, j, k: (i, k)),
                pl.BlockSpec((None, tk, tn), lambda i, j, k: (layer, k, j))]
    args = [x, w]
    if bias is not None:
        in_specs.append(pl.BlockSpec((None, 1, tn), lambda i, j, k: (layer, 0, j)))
        args.append(bias)
    if resid is not None:
        in_specs.append(pl.BlockSpec((tm, tn), lambda i, j, k: (i, j)))
        args.append(resid)
    scratch = [pltpu.VMEM((tm, tn), F32)] if nk > 1 else []
    return pl.pallas_call(
        functools.partial(_mm_kernel, nk=nk, act=act, has_bias=bias is not None,
                          has_resid=resid is not None),
        out_shape=jax.ShapeDtypeStruct((m, n), out_dtype),
        grid=(m // tm, n // tn, nk),
        in_specs=in_specs,
        out_specs=pl.BlockSpec((tm, tn), lambda i, j, k: (i, j)),
        scratch_shapes=scratch,
        compiler_params=_params(("parallel", "parallel", "arbitrary")),
        name=name,
    )(*args)


def _mlstm_kernel(q_ref, k_ref, v_ref, g_ref, o_ref, c_ref, m_ref, *, nc, heads, dv, scale):
    lc = SCAN_CHUNK
    s = pl.program_id(0)
    c = pl.program_id(1)
    rev = (s // heads) % 2
    cc = c + rev * (nc - 1 - 2 * c)

    @pl.when(c == 0)
    def _():
        c_ref[...] = jnp.zeros_like(c_ref)
        m_ref[...] = jnp.zeros_like(m_ref)

    row = lax.broadcasted_iota(jnp.int32, (lc, lc), 0)
    col = lax.broadcasted_iota(jnp.int32, (lc, lc), 1)
    tri = ((col - row) * (1 - 2 * rev)) <= 0
    eye = row == col

    gl = g_ref[cc]
    ig = gl[0:1, :]
    lf = _log_sigmoid(gl[1:2, :])
    b_col = jnp.sum(jnp.where(tri, jnp.broadcast_to(lf, (lc, lc)), 0.0), axis=1, keepdims=True)
    b_row = jnp.sum(jnp.where(eye, b_col, 0.0), axis=0, keepdims=True)
    g = jnp.sum(lf, axis=1, keepdims=True)
    m_prev = m_ref[...]

    log_d = jnp.where(tri, b_col - b_row + ig, NEG_BIG)
    m_inter = b_col + m_prev
    m_j = jnp.maximum(m_inter, jnp.max(log_d, axis=1, keepdims=True))
    p = jnp.exp(log_d - m_j)

    q = q_ref[...]
    k = k_ref[...]
    v = v_ref[...]
    ones_blk = (lax.broadcasted_iota(jnp.int32, (lc, LANES), 1) == 0).astype(BF16)
    v_aug = jnp.concatenate([v, ones_blk], axis=1)
    smat = (_dot_nt(q, k) * p).astype(BF16)
    w_inter = jnp.exp(m_inter - m_j)
    num_aug = (_dot(smat, v_aug) + w_inter * _dot(q, c_ref[...].astype(BF16))) * scale
    num = num_aug[:, :dv]
    den = num_aug[:, dv:dv + 1]
    o_ref[...] = (num / jnp.maximum(jnp.abs(den), jnp.exp(-m_j))).astype(o_ref.dtype)

    a_row = g - b_row + ig
    m_new = jnp.maximum(g + m_prev, jnp.max(a_row, axis=1, keepdims=True))
    wk_row = jnp.exp(a_row - m_new)
    wk_col = jnp.sum(jnp.where(eye, jnp.broadcast_to(wk_row, (lc, lc)), 0.0), axis=1, keepdims=True)
    decay = jnp.exp(g + m_prev - m_new)
    kw = (k.astype(F32) * wk_col).astype(BF16)
    c_ref[...] = decay * c_ref[...] + _dot_tn(kw, v_aug)
    m_ref[...] = m_new


def _mlstm(proj, gates, *, batch, seq, q_off, k_off, v_off, heads, dk, dv):
    lc = SCAN_CHUNK
    nc = seq // lc
    assert dk == lc and seq % lc == 0 and dv % LANES == 0

    def rowblk(s, c):
        rev = (s // heads) % 2
        return (s // (2 * heads)) * nc + c + rev * (nc - 1 - 2 * c)

    return pl.pallas_call(
        functools.partial(_mlstm_kernel, nc=nc, heads=heads, dv=dv, scale=dk ** -0.5),
        out_shape=jax.ShapeDtypeStruct((2, batch * seq, heads * dv), F32),
        grid=(batch * 2 * heads, nc),
        in_specs=[
            pl.BlockSpec((lc, dk), lambda s, c: (rowblk(s, c), q_off // dk + s % heads)),
            pl.BlockSpec((lc, dk), lambda s, c: (rowblk(s, c), k_off // dk + s % heads)),
            pl.BlockSpec((lc, dv), lambda s, c: (rowblk(s, c), v_off // dv + s % heads)),
            pl.BlockSpec((None, nc, 2, lc), lambda s, c: (s, 0, 0, 0)),
        ],
        out_specs=pl.BlockSpec((None, lc, dv), lambda s, c: ((s // heads) % 2, rowblk(s, c), s % heads)),
        scratch_shapes=[pltpu.VMEM((dk, dv + LANES), F32), pltpu.VMEM((1, 1), F32)],
        compiler_params=_params(("parallel", "arbitrary")),
        name="mlstm_scan",
    )(proj, proj, proj, gates)


def _gla_levels():
    lc = SCAN_CHUNK
    nlev = int(math.log2(lc)) + 1
    t = np.arange(lc)
    diff = np.zeros((2, nlev * lc, lc), np.float32)
    mask = np.zeros((2, nlev, lc, lc), np.float32)
    u = t[None, :]
    for d in range(2):
        cum = (u <= t[:, None]) if d == 0 else (u >= t[:, None])
        diff[d, :lc] = cum
        mask[d, 0] = np.eye(lc)
        for l in range(1, nlev):
            w = 1 << (l - 1)
            pair = t // (2 * w)
            second = (t % (2 * w)) >= w
            if d == 0:
                bd = pair * 2 * w + w - 1
                cum_bd = u <= bd[:, None]
                is_q, is_k = second, ~second
            else:
                bd = pair * 2 * w + w
                cum_bd = u >= bd[:, None]
                is_q, is_k = ~second, second
            diff[d, l * lc:(l + 1) * lc] = cum.astype(np.float32) - cum_bd.astype(np.float32)
            mask[d, l] = ((pair[:, None] == pair[None, :]) & is_q[:, None] & is_k[None, :])
    return diff, mask


def _gla_kernel(q_ref, k_ref, v_ref, sm_ref, up_ref, bias_ref, d_ref, mask_ref, o_ref, s_ref, *, nlev, scale):
    lc = SCAN_CHUNK
    c = pl.program_id(1)

    @pl.when(c == 0)
    def _():
        s_ref[...] = jnp.zeros_like(s_ref)

    z = _dot3(sm_ref[...], up_ref[...]) + bias_ref[...]
    la = _log_sigmoid(z) * (1.0 / GLA_TAU)
    dmat = d_ref[...]
    x = sum(_dot(dmat, piece) for piece in _split_bf16(la, 3))
    b = x[:lc]
    g_row = jnp.sum(la, axis=0, keepdims=True)

    qb = q_ref[...]
    kb = k_ref[...]
    v = v_ref[...]
    q = qb.astype(F32)
    k = kb.astype(F32)
    att = mask_ref[0] * _dot_nt(qb, kb)
    for l in range(1, nlev):
        e = jnp.exp(-jnp.abs(x[l * lc:(l + 1) * lc]))
        att = att + mask_ref[l] * _dot_nt((q * e).astype(BF16), (k * e).astype(BF16))

    o = _dot(att.astype(BF16), v) + _dot((q * jnp.exp(b)).astype(BF16), s_ref[...].astype(BF16))
    o_ref[...] = (o * scale).astype(o_ref.dtype)

    row = lax.broadcasted_iota(jnp.int32, (lc, lc), 0)
    col = lax.broadcasted_iota(jnp.int32, (lc, lc), 1)
    eg_col = jnp.sum(jnp.where(row == col, jnp.broadcast_to(jnp.exp(g_row), (lc, lc)), 0.0),
                     axis=1, keepdims=True)
    kd = (k * jnp.exp(g_row - b)).astype(BF16)
    s_ref[...] = eg_col * s_ref[...] + _dot_tn(kd, v)


def _gla(proj, small, up_pad, bias, *, batch, seq, q_off, k_off, v_off, heads, dk, dv):
    lc = SCAN_CHUNK
    nc = seq // lc
    assert dk == lc and seq % lc == 0
    diff, mask = _gla_levels()
    nlev = mask.shape[1]

    def rowblk(s, c):
        rev = (s // heads) % 2
        return (s // (2 * heads)) * nc + c + rev * (nc - 1 - 2 * c)

    return pl.pallas_call(
        functools.partial(_gla_kernel, nlev=nlev, scale=dk ** -0.5),
        out_shape=jax.ShapeDtypeStruct((2, batch * seq, heads * dv), F32),
        grid=(batch * 2 * heads, nc),
        in_specs=[
            pl.BlockSpec((lc, dk), lambda s, c: (rowblk(s, c), q_off // dk + s % heads)),
            pl.BlockSpec((lc, dk), lambda s, c: (rowblk(s, c), k_off // dk + s % heads)),
            pl.BlockSpec((lc, dv), lambda s, c: (rowblk(s, c), v_off // dv + s % heads)),
            pl.BlockSpec((lc, LANES), lambda s, c: (rowblk(s, c), 0)),
            pl.BlockSpec((None, LANES, dk), lambda s, c: (s % (2 * heads), 0, 0)),
            pl.BlockSpec((None, 1, dk), lambda s, c: (s % (2 * heads), 0, 0)),
            pl.BlockSpec((None, nlev * lc, lc), lambda s, c: ((s // heads) % 2, 0, 0)),
            pl.BlockSpec((None, nlev, lc, lc), lambda s, c: ((s // heads) % 2, 0, 0, 0)),
        ],
        out_specs=pl.BlockSpec((None, lc, dv), lambda s, c: ((s // heads) % 2, rowblk(s, c), s % heads)),
        scratch_shapes=[pltpu.VMEM((dk, dv), F32)],
        compiler_params=_params(("parallel", "arbitrary")),
        name="gla_scan",
    )(proj, proj, proj, small, up_pad, bias, jnp.asarray(diff, BF16), jnp.asarray(mask, F32))


def _headnorm_gate_kernel(h_ref, g_ref, hn_ref, o_ref, *, heads, dv, act):
    h = h_ref[0] + h_ref[1]
    outs = []
    for hh in range(heads):
        blk = h[:, hh * dv:(hh + 1) * dv]
        r = lax.rsqrt(jnp.mean(blk * blk, axis=-1, keepdims=True) + EPS)
        outs.append(blk * r)
    hn = jnp.concatenate(outs, axis=1) * hn_ref[...]
    gate = g_ref[...].astype(F32)
    sig = 1.0 / (1.0 + jnp.exp(-gate))
    gate = sig if act == "sigmoid" else gate * sig
    o_ref[...] = (hn * gate).astype(o_ref.dtype)


def _headnorm_gate(h2, proj, gate_off, head_norm, *, heads, dv, act, tm=512):
    _, m, width = h2.shape
    tm = min(tm, m)
    return pl.pallas_call(
        functools.partial(_headnorm_gate_kernel, heads=heads, dv=dv, act=act),
        out_shape=jax.ShapeDtypeStruct((m, width), BF16),
        grid=(m // tm,),
        in_specs=[pl.BlockSpec((2, tm, width), lambda i: (0, i, 0)),
                  pl.BlockSpec((tm, width), lambda i: (i, gate_off // width)),
                  pl.BlockSpec((1, width), lambda i: (0, 0))],
        out_specs=pl.BlockSpec((tm, width), lambda i: (i, 0)),
        compiler_params=_params(("parallel",)),
        name="headnorm_gate",
    )(h2, proj, head_norm.reshape(1, width).astype(F32))


def _short_conv_kernel(x0_ref, x1_ref, x2_ref, w_ref, b_ref, o0_ref, o1_ref, o2_ref, s_ref):
    seq = x0_ref.shape[0]
    n2, half = o0_ref.shape[0], o0_ref.shape[1]
    t = lax.broadcasted_iota(jnp.int32, x0_ref.shape, 0)
    for p, (x_ref, o_ref) in enumerate(((x0_ref, o0_ref), (x1_ref, o1_ref), (x2_ref, o2_ref))):
        x = x_ref[...].astype(F32)
        prev = jnp.where(t == 0, 0.0, pltpu.roll(x, 1, axis=0))
        nxt = jnp.where(t == seq - 1, 0.0, pltpu.roll(x, seq - 1, axis=0))
        w = w_ref[p]
        s_ref[...] = prev * w[0:1] + x * w[1:2] + nxt * w[2:3] + b_ref[p]

        def body(j, carry, o_ref=o_ref):
            o_ref[j] = s_ref[pl.ds(j, half, stride=n2), :].astype(o_ref.dtype)
            return carry

        lax.fori_loop(0, n2, body, 0)


def _short_conv(proj, conv_w, conv_b, *, batch, seq, off, ch, n2):
    tc = LANES
    nct = ch // tc
    half = seq // n2
    w = conv_w.reshape(3, 3, nct, tc).transpose(1, 2, 0, 3).astype(F32)
    b = conv_b.reshape(3, nct, 1, tc).astype(F32)
    in_specs = [pl.BlockSpec((seq, tc), functools.partial(lambda b_, j, p: (b_, off // tc + p * nct + j), p=p))
                for p in range(3)]
    in_specs += [pl.BlockSpec((3, None, 3, tc), lambda b_, j: (0, j, 0, 0)),
                 pl.BlockSpec((3, None, 1, tc), lambda b_, j: (0, j, 0, 0))]
    out = jax.ShapeDtypeStruct((batch, n2, half, ch), BF16)
    return pl.pallas_call(
        _short_conv_kernel,
        out_shape=[out, out, out],
        grid=(batch, nct),
        in_specs=in_specs,
        out_specs=[pl.BlockSpec((None, n2, half, tc), lambda b_, j: (b_, 0, 0, j))] * 3,
        scratch_shapes=[pltpu.VMEM((seq, tc), F32)],
        compiler_params=_params(("parallel", "parallel")),
        name="hyena_short_conv",
    )(proj, proj, proj, w, b)


def _filter_kernel(z_ref, w1_ref, b1_ref, w2_ref, b2_ref, w3_ref, f_ref, dl_ref, o_ref, *, n1, ch, order, emb):
    z = z_ref[...]
    rows = z.shape[0]
    f = f_ref[...]
    h = jnp.sin(f * (_dot3(z, w1_ref[...]) + b1_ref[...]))
    h = jnp.sin(f * (_dot3(h, w2_ref[...]) + b2_ref[...]))
    out = _dot3(h, w3_ref[...])
    win = jnp.exp(-z[:, 0:1] * dl_ref[...]) * z[:, emb:emb + 1]
    outer = lax.broadcasted_iota(jnp.int32, (rows, 1), 0) % n1
    fwd = outer < n1 // 2
    for o in range(order):
        sel = jnp.where(fwd, out[:, (2 * o) * ch:(2 * o + 1) * ch], out[:, (2 * o + 1) * ch:(2 * o + 2) * ch])
        o_ref[:, :, o * ch:(o + 1) * ch] = (sel * win).reshape(rows // n1, n1, ch).astype(o_ref.dtype)


def _hyena_taps(seq, w1, b1, w2, b2, w3, freq, ch, n2, tb=4):
    emb, hid = w1.shape
    n1 = 2 * seq // n2
    a = jnp.arange(n1, dtype=jnp.int32)[None, :]
    bdig = jnp.arange(n2, dtype=jnp.int32)[:, None]
    lag = (n2 * a + bdig).reshape(-1)
    pos = jnp.where(lag < seq, lag, 2 * seq - lag)
    n = pos.astype(F32)
    t = n / (seq - 1)
    bands = jnp.linspace(1e-4, HYENA_BANDS - 1, HYENA_BANDS, dtype=F32)
    ang = (2.0 * math.pi * n / seq)[:, None] * bands[None, :]
    valid = (lag != seq).astype(F32)
    z = jnp.concatenate([t[:, None], jnp.cos(ang), -jnp.sin(ang), valid[:, None]], axis=-1)
    z = jnp.pad(z, ((0, 0), (0, LANES - emb - 1)))
    deltas = jnp.abs(jnp.linspace(HYENA_MIN_DECAY, HYENA_MAX_DECAY, ch, dtype=F32)).reshape(1, ch)
    pad_h = LANES - hid
    w1p = jnp.pad(w1.astype(F32), ((0, LANES - emb), (0, pad_h)))
    w2p = jnp.pad(w2.astype(F32), ((0, pad_h), (0, pad_h)))
    w3p = jnp.pad(w3.astype(F32), ((0, pad_h), (0, 0)))
    row = lambda v: jnp.pad(v.astype(F32), (0, pad_h)).reshape(1, LANES)
    order = w3.shape[1] // (2 * ch)
    tb = min(tb, n2)
    full = lambda shape: pl.BlockSpec(shape, lambda i: (0, 0))
    return pl.pallas_call(
        functools.partial(_filter_kernel, n1=n1, ch=ch, order=order, emb=emb),
        out_shape=jax.ShapeDtypeStruct((n2, n1, order * ch), BF16),
        grid=(n2 // tb,),
        in_specs=[pl.BlockSpec((tb * n1, LANES), lambda i: (i, 0)),
                  full((LANES, LANES)), full((1, LANES)), full((LANES, LANES)), full((1, LANES)),
                  full((LANES, 2 * order * ch)), full((1, LANES)), full((1, ch))],
        out_specs=pl.BlockSpec((tb, n1, order * ch), lambda i: (i, 0, 0)),
        compiler_params=_params(("parallel",)),
        name="hyena_filters",
    )(z, w1p, row(b1), w2p, row(b2), w3p, row(freq), deltas)


def _fft_tables(n1, n2):
    n = n1 * n2
    a = jnp.arange(n1, dtype=jnp.int32)
    ang1 = ((a[:, None] * a[None, :]) % n1).astype(F32) * (2.0 * math.pi / n1)
    f1r, f1i = jnp.cos(ang1), -jnp.sin(ang1)
    c = jnp.arange(n1, dtype=jnp.int32)[:, None, None]
    d = jnp.arange(n2, dtype=jnp.int32)[None, :, None]
    b = jnp.arange(n2, dtype=jnp.int32)[None, None, :]
    ang = ((b * (c + n1 * d)) % n).astype(F32) * (2.0 * math.pi / n)
    gr, gi = jnp.cos(ang), -jnp.sin(ang)
    gblk = jnp.concatenate([jnp.concatenate([gr, -gi], axis=2),
                            jnp.concatenate([gi, gr], axis=2)], axis=1)
    return f1r, f1i, gblk.astype(BF16)


def _stage_rows(t_ref, row0, val):
    rows = val.shape[0]
    for j in range(t_ref.shape[0]):
        t_ref[j, pl.ds(row0, rows), :] = val[:, j * LANES:(j + 1) * LANES]


def _gather_rows(t_ref, start, count, stride, dtype):
    parts = [t_ref[j, pl.ds(start, count, stride=stride), :] for j in range(t_ref.shape[0])]
    return jnp.concatenate(parts, axis=1).astype(dtype)


def _fft1_kernel(f_ref, u_ref, o_ref, t_ref, *, n1):
    tile = u_ref.shape[0]
    rows = 2 * n1
    for s in range(tile):
        _stage_rows(t_ref, s * rows, _dot(f_ref[...], u_ref[s]))

    def body(c, carry):
        for ri in range(2):
            o_ref[ri, c] = _gather_rows(t_ref, ri * n1 + c, tile, rows, o_ref.dtype)
        return carry

    lax.fori_loop(0, n1, body, 0)


def _fft_stage1(fstack, u, *, n1, ct=512):
    nb, n2, kin, ch = u.shape
    tile = min(BF16_TILE_ROWS, n2)
    ct = min(ct, ch)
    return pl.pallas_call(
        functools.partial(_fft1_kernel, n1=n1),
        out_shape=jax.ShapeDtypeStruct((nb, 2, n1, n2, ch), BF16),
        grid=(nb, ch // ct, n2 // tile),
        in_specs=[pl.BlockSpec((2 * n1, kin), lambda b, j, i: (0, 0)),
                  pl.BlockSpec((None, tile, kin, ct), lambda b, j, i: (b, i, 0, j))],
        out_specs=pl.BlockSpec((None, 2, n1, tile, ct), lambda b, j, i: (b, 0, 0, i, j)),
        scratch_shapes=[pltpu.VMEM((ct // LANES, tile * 2 * n1, LANES), F32)],
        compiler_params=_params(("parallel", "parallel", "parallel")),
        name="fft_stage1",
    )(fstack, u)


def _fft_spec_kernel(a_ref, g_ref, o_ref, *, n2, inv_n):
    a = a_ref[...].reshape(2 * n2, a_ref.shape[-1])
    o_ref[...] = _dot(g_ref[...], a) * inv_n


def _fft_spectrum(a, gblk, *, n1, n2, ch, ct=512):
    ct = min(ct, ch)
    return pl.pallas_call(
        functools.partial(_fft_spec_kernel, n2=n2, inv_n=1.0 / (n1 * n2)),
        out_shape=jax.ShapeDtypeStruct((n1, 2 * n2, ch), F32),
        grid=(ch // ct, n1),
        in_specs=[pl.BlockSpec((None, 2, None, n2, ct), lambda j, c: (0, 0, c, 0, j)),
                  pl.BlockSpec((None, 2 * n2, 2 * n2), lambda j, c: (c, 0, 0))],
        out_specs=pl.BlockSpec((None, 2 * n2, ct), lambda j, c: (c, 0, j)),
        compiler_params=_params(("parallel", "parallel")),
        name="fft_filter_spectrum",
    )(a, gblk)


def _fft_mid_kernel(a_ref, g_ref, kf_ref, o_ref, t_ref, *, n2):
    tile = a_ref.shape[1]
    rows = 2 * n2

    def one(i, carry):
        g = g_ref[i]
        a = jnp.concatenate([a_ref[0, i], a_ref[1, i]], axis=0)
        x = _dot(g, a)
        xr, xi = x[:n2], x[n2:]
        kf = kf_ref[i]
        kr, ki = kf[:n2], kf[n2:]
        y = jnp.concatenate([xr * kr - xi * ki, xr * ki + xi * kr], axis=0).astype(BF16)
        _stage_rows(t_ref, pl.multiple_of(i * rows, rows), _dot_tn(g, y))
        return carry

    lax.fori_loop(0, tile, one, 0)

    def body(r, carry):
        for ri in range(2):
            o_ref[ri, r] = _gather_rows(t_ref, ri * n2 + r, tile, rows, o_ref.dtype)
        return carry

    lax.fori_loop(0, n2, body, 0)


def _fft_mid(a, gblk, kf, *, kf_off, ct=256):
    nb, _, n1, n2, ch = a.shape
    tile = min(BF16_TILE_ROWS, n1)
    ct = min(ct, ch)
    return pl.pallas_call(
        functools.partial(_fft_mid_kernel, n2=n2),
        out_shape=jax.ShapeDtypeStruct((nb, 2, n2, n1, ch), BF16),
        grid=(nb, ch // ct, n1 // tile),
        in_specs=[pl.BlockSpec((None, 2, tile, n2, ct), lambda b, j, c: (b, 0, c, 0, j)),
                  pl.BlockSpec((tile, 2 * n2, 2 * n2), lambda b, j, c: (c, 0, 0)),
                  pl.BlockSpec((tile, 2 * n2, ct), lambda b, j, c: (c, 0, kf_off // ct + j))],
        out_specs=pl.BlockSpec((None, 2, n2, tile, ct), lambda b, j, c: (b, 0, 0, c, j)),
        scratch_shapes=[pltpu.VMEM((ct // LANES, tile * 2 * n2, LANES), F32)],
        compiler_params=_params(("parallel", "parallel", "parallel")),
        name="fft_spectral",
    )(a, gblk, kf)


def _fft3_kernel(fr_ref, fi_ref, z_ref, u_ref, gate_ref, skip_ref, o_ref, *scratch, time_major):
    tile, half = u_ref.shape[0], u_ref.shape[1]
    for s in range(tile):
        y = _dot(fr_ref[...], z_ref[0, s]) + _dot(fi_ref[...], z_ref[1, s])
        y = gate_ref[s].astype(F32) * (y + skip_ref[...] * u_ref[s].astype(F32))
        if time_major:
            _stage_rows(scratch[0], s * half, y)
        else:
            o_ref[s] = y.astype(o_ref.dtype)
    if time_major:
        def body(a, carry):
            o_ref[a] = _gather_rows(scratch[0], a, tile, half, o_ref.dtype)
            return carry

        lax.fori_loop(0, half, body, 0)


def _fft_stage3(f1r_h, f1i_h, z, u, gate, skip, *, time_major, ct=512):
    nb, _, n2, n1, ch = z.shape
    half = n1 // 2
    tile = min(BF16_TILE_ROWS, n2)
    ct = min(ct, ch)
    sig_spec = pl.BlockSpec((None, tile, half, ct), lambda b, j, i: (b, i, 0, j))
    if time_major:
        out_shape = jax.ShapeDtypeStruct((nb, half, n2, ch), BF16)
        out_spec = pl.BlockSpec((None, half, tile, ct), lambda b, j, i: (b, 0, i, j))
        scratch = [pltpu.VMEM((ct // LANES, tile * half, LANES), F32)]
    else:
        out_shape = jax.ShapeDtypeStruct((nb, n2, half, ch), BF16)
        out_spec = sig_spec
        scratch = []
    return pl.pallas_call(
        functools.partial(_fft3_kernel, time_major=time_major),
        out_shape=out_shape,
        grid=(nb, ch // ct, n2 // tile),
        in_specs=[pl.BlockSpec((half, n1), lambda b, j, i: (0, 0)),
                  pl.BlockSpec((half, n1), lambda b, j, i: (0, 0)),
                  pl.BlockSpec((None, 2, tile, n1, ct), lambda b, j, i: (b, 0, i, 0, j)),
                  sig_spec, sig_spec,
                  pl.BlockSpec((1, ct), lambda b, j, i: (0, j))],
        out_specs=out_spec,
        scratch_shapes=scratch,
        compiler_params=_params(("parallel", "parallel", "parallel")),
        name="fft_stage3",
    )(f1r_h, f1i_h, z, u, gate, skip.reshape(1, ch).astype(F32))


def _hyena(proj, hy_off, conv_w, conv_b, w1, b1, w2, b2, w3, freq, skip, *, batch, seq, ch):
    n2 = FFT_N2
    n1 = 2 * seq // n2
    half = n1 // 2
    v, x1, x2 = _short_conv(proj, conv_w, conv_b, batch=batch, seq=seq, off=hy_off, ch=ch, n2=n2)
    taps = _hyena_taps(seq, w1, b1, w2, b2, w3, freq, ch, n2)
    f1r, f1i, gblk = _fft_tables(n1, n2)
    fstack = jnp.concatenate([f1r, f1i], axis=0)
    ka = _fft_stage1(fstack.astype(BF16), taps[None], n1=n1)
    kf = _fft_spectrum(ka, gblk, n1=n1, n2=n2, ch=HYENA_ORDER * ch)
    fdata = fstack[:, :half].astype(BF16)
    f1r_h, f1i_h = f1r[:half].astype(BF16), f1i[:half].astype(BF16)
    zcur = v
    for o, gate in enumerate((x1, x2)):
        a = _fft_stage1(fdata, zcur, n1=n1)
        zmid = _fft_mid(a, gblk, kf, kf_off=o * ch)
        zcur = _fft_stage3(f1r_h, f1i_h, zmid, zcur, gate, skip[o], time_major=(o == HYENA_ORDER - 1))
    return zcur.reshape(batch * seq, ch)


def _merge_kernel(ya_ref, yb_ref, yc_ref, ga_ref, gb_ref, gc_ref, w_ref, o_ref):
    acc = None
    for n, (y_ref, g_ref) in enumerate(((ya_ref, ga_ref), (yb_ref, gb_ref), (yc_ref, gc_ref))):
        gate = 1.0 / (1.0 + jnp.exp(-g_ref[...].astype(F32)))
        term = gate * _dot(y_ref[...], w_ref[n])
        acc = term if acc is None else acc + term
    o_ref[...] = acc.astype(o_ref.dtype)


def _merge(ys, proj, gate_off, w_branch, layer, *, d_model, tm=1024, tn=512):
    m, width = ys[0].shape
    tm, tn = min(tm, m), min(tn, d_model)
    y_spec = pl.BlockSpec((tm, width), lambda i, j: (i, 0))
    g_specs = [pl.BlockSpec((tm, tn), functools.partial(
        lambda i, j, n: (i, (gate_off + n * d_model) // tn + j), n=n)) for n in range(N_BRANCH)]
    return pl.pallas_call(
        _merge_kernel,
        out_shape=jax.ShapeDtypeStruct((m, d_model), BF16),
        grid=(m // tm, d_model // tn),
        in_specs=[y_spec, y_spec, y_spec] + g_specs +
                 [pl.BlockSpec((None, N_BRANCH, width, tn), lambda i, j: (layer, 0, 0, j))],
        out_specs=pl.BlockSpec((tm, tn), lambda i, j: (i, j)),
        compiler_params=_params(("parallel", "parallel")),
        name="branch_merge",
    )(*ys, proj, proj, proj, w_branch)


def _xattn_kernel(q_ref, k_ref, v_ref, o_ref, *, scale):
    s = _dot_nt(q_ref[...], k_ref[...]) * scale
    s = s - jnp.max(s, axis=-1, keepdims=True)
    p = jnp.exp(s)
    l = jnp.sum(p, axis=-1, keepdims=True)
    o_ref[...] = (_dot(p.astype(BF16), v_ref[...]) / l).astype(o_ref.dtype)


def _xattn(q, kv, *, batch, seq, mem, heads, hd, tq=512):
    tq = min(tq, seq)
    nq = seq // tq
    return pl.pallas_call(
        functools.partial(_xattn_kernel, scale=hd ** -0.5),
        out_shape=jax.ShapeDtypeStruct(q.shape, BF16),
        grid=(batch, heads, nq),
        in_specs=[pl.BlockSpec((tq, hd), lambda b, h, i: (b * nq + i, h)),
                  pl.BlockSpec((mem, hd), lambda b, h, i: (b, h)),
                  pl.BlockSpec((mem, hd), lambda b, h, i: (b, heads + h))],
        out_specs=pl.BlockSpec((tq, hd), lambda b, h, i: (b * nq + i, h)),
        compiler_params=_params(("parallel", "parallel", "parallel")),
        name="xattn_core",
    )(q, kv, kv)


def _in_proj_layout(d):
    mix = d // 4
    m_dk = mix // MLSTM_HEADS // 2
    g_dk = mix // GLA_HEADS // 2
    sizes = (MLSTM_HEADS * m_dk, MLSTM_HEADS * m_dk, mix, mix, 4 * MLSTM_HEADS,
             GLA_HEADS * g_dk, GLA_HEADS * g_dk, mix, mix, 2 * GLA_RANK, 3 * mix, N_BRANCH * d)
    offs = np.concatenate([[0], np.cumsum(sizes)])
    wide = (0, 1, 2, 3, 5, 6, 7, 8, 10, 11)
    narrow = (4, 9)
    new_off, pos = {}, 0
    for i in wide:
        new_off[i] = pos
        pos += sizes[i]
    return sizes, offs, wide, narrow, new_off


def kernel(x, mem, norm_gains, final_norm, w_in, b_in, mlstm_head_norm, gla_decay_up, gla_decay_bias, gla_head_norm, hyena_conv_w, hyena_conv_b, hyena_ffn_w1, hyena_ffn_b1, hyena_ffn_w2, hyena_ffn_b2, hyena_ffn_w3, hyena_freq, hyena_skip, w_branch, w_out, xattn_wq, xattn_wkv, xattn_wo, mlp_w1, mlp_w2):
    batch, seq, d = x.shape
    mem_tokens = mem.shape[1]
    depth = norm_gains.shape[0]
    mix = d // 4
    mh, gh = MLSTM_HEADS, GLA_HEADS
    m_dv, g_dv = mix // mh, mix // gh
    m_dk, g_dk = m_dv // 2, g_dv // 2
    nc = seq // SCAN_CHUNK

    sizes, offs, wide, narrow, new_off = _in_proj_layout(d)
    seg = lambda a, i: a[..., offs[i]:offs[i + 1]]
    w_main = jnp.concatenate([seg(w_in, i) for i in wide], axis=2).astype(BF16)
    b_main = jnp.concatenate([seg(b_in, i) for i in wide], axis=1)[:, None, :].astype(F32)
    n_small = sum(sizes[i] for i in narrow)
    w_small = jnp.pad(jnp.concatenate([seg(w_in, i) for i in narrow], axis=2),
                      ((0, 0), (0, 0), (0, LANES - n_small))).astype(BF16)
    b_small = jnp.pad(jnp.concatenate([seg(b_in, i) for i in narrow], axis=1),
                      ((0, 0), (0, LANES - n_small)))[:, None, :].astype(F32)
    wb16, wo16 = w_branch.astype(BF16), w_out.astype(BF16)
    wq16, wkv16, wxo16 = xattn_wq.astype(BF16), xattn_wkv.astype(BF16), xattn_wo.astype(BF16)
    w1_16, w2_16 = mlp_w1.astype(BF16), mlp_w2.astype(BF16)

    h = x.reshape(batch * seq, d)
    memf = mem.reshape(batch * mem_tokens, d)
    for l in range(depth):
        g = norm_gains[l]
        xn = _rmsnorm(h, g[0], BF16)
        proj = _mm(xn, w_main, l, bias=b_main, out_dtype=BF16, tn=1024, name="in_proj")
        small = _mm(xn, w_small, l, bias=b_small, out_dtype=F32, name="in_proj_gates")

        mg = small[:, :4 * mh].reshape(batch, nc, SCAN_CHUNK, 2, 2, mh)
        mgates = mg.transpose(0, 3, 5, 1, 4, 2).reshape(batch * 2 * mh, nc, 2, SCAN_CHUNK)
        h_m = _mlstm(proj, mgates, batch=batch, seq=seq, q_off=new_off[0], k_off=new_off[1],
                     v_off=new_off[2], heads=mh, dk=m_dk, dv=m_dv)
        y_a = _headnorm_gate(h_m, proj, new_off[3], mlstm_head_norm[l], heads=mh, dv=m_dv, act="sigmoid")

        up = gla_decay_up[l].reshape(2, GLA_RANK, gh, g_dk).transpose(0, 2, 1, 3).astype(F32)
        up_pad = jnp.zeros((2, gh, LANES, g_dk), F32)
        for dr in range(2):
            r0 = 4 * mh + dr * GLA_RANK
            up_pad = up_pad.at[dr, :, r0:r0 + GLA_RANK, :].set(up[dr])
        up_pad = up_pad.reshape(2 * gh, LANES, g_dk)
        gbias = gla_decay_bias[l].reshape(2 * gh, 1, g_dk).astype(F32)
        h_g = _gla(proj, small, up_pad, gbias, batch=batch, seq=seq, q_off=new_off[5], k_off=new_off[6],
                   v_off=new_off[7], heads=gh, dk=g_dk, dv=g_dv)
        y_b = _headnorm_gate(h_g, proj, new_off[8], gla_head_norm[l], heads=gh, dv=g_dv, act="silu")

        y_c = _hyena(proj, new_off[10], hyena_conv_w[l], hyena_conv_b[l], hyena_ffn_w1[l], hyena_ffn_b1[l],
                     hyena_ffn_w2[l], hyena_ffn_b2[l], hyena_ffn_w3[l], hyena_freq[l], hyena_skip[l],
                     batch=batch, seq=seq, ch=mix)

        merged = _merge((y_a, y_b, y_c), proj, new_off[11], wb16, l, d_model=d)
        h = _mm(merged, wo16, l, resid=h, out_dtype=F32, name="mixer_out")

        xn = _rmsnorm(h, g[1], BF16)
        memn = _rmsnorm(memf, g[2], BF16)
        q = _mm(xn, wq16, l, out_dtype=BF16, tn=1024, name="xattn_q")
        kv = _mm(memn, wkv16, l, out_dtype=BF16, tn=1024, name="xattn_kv")
        o = _xattn(q, kv, batch=batch, seq=seq, mem=mem_tokens, heads=XATTN_HEADS, hd=d // XATTN_HEADS)
        h = _mm(o, wxo16, l, resid=h, out_dtype=F32, name="xattn_out")

        xn = _rmsnorm(h, g[3], BF16)
        hid = _mm(xn, w1_16, l, act="relu2", out_dtype=BF16, tn=1024, name="mlp_up")
        h = _mm(hid, w2_16, l, resid=h, out_dtype=F32, tk=4096, name="mlp_down")
    return _rmsnorm(h, final_norm, x.dtype).reshape(batch, seq, d)
```

```python
import functools
import math

import numpy as np
import jax
import jax.numpy as jnp
from jax import lax
from jax.experimental import pallas as pl
from jax.experimental.pallas import tpu as pltpu

F32 = jnp.float32
BF16 = jnp.bfloat16

N_BRANCH = 3
MLSTM_HEADS = 4
GLA_HEADS = 4
GLA_RANK = 16
GLA_TAU = 16.0
HYENA_ORDER = 2
HYENA_BANDS = 16
HYENA_MIN_DECAY = math.log(1e-2) / 1.5
HYENA_MAX_DECAY = math.log(1e-2) / 0.3
XATTN_HEADS = 4
EPS = 1e-6

LANES = 128
BF16_TILE_ROWS = 16
VMEM_LIMIT_BYTES = 56 * 1024 * 1024

SCAN_CHUNK = 128
FFT_N2 = 128
NEG_BIG = -1e30
GATHER_UNROLL = 8


def _params(sem):
    return pltpu.CompilerParams(dimension_semantics=sem, vmem_limit_bytes=VMEM_LIMIT_BYTES)


def _log_sigmoid(x):
    return -(jnp.maximum(-x, 0.0) + jnp.log(1.0 + jnp.exp(-jnp.abs(x))))


def _split_bf16(a, parts):
    out = []
    r = a
    for _ in range(parts):
        p = r.astype(BF16)
        out.append(p)
        r = r - p.astype(F32)
    return out


def _dot(a, b):
    return jnp.dot(a, b, preferred_element_type=F32)


def _dot_nt(a, b):
    return lax.dot_general(a, b, (((1,), (1,)), ((), ())), preferred_element_type=F32)


def _dot_tn(a, b):
    return lax.dot_general(a, b, (((0,), (0,)), ((), ())), preferred_element_type=F32)


def _dot3(a, b):
    a_hi, a_lo = _split_bf16(a, 2)
    b_hi, b_lo = _split_bf16(b, 2)
    return _dot(a_hi, b_hi) + _dot(a_hi, b_lo) + _dot(a_lo, b_hi)


def _rmsnorm_kernel(x_ref, g_ref, o_ref):
    x = x_ref[...].astype(F32)
    r = lax.rsqrt(jnp.mean(x * x, axis=-1, keepdims=True) + EPS)
    o_ref[...] = ((x * r) * g_ref[...]).astype(o_ref.dtype)


def _rmsnorm(x, g, out_dtype, tm=256):
    m, d = x.shape
    tm = min(tm, m)
    return pl.pallas_call(
        _rmsnorm_kernel,
        out_shape=jax.ShapeDtypeStruct((m, d), out_dtype),
        grid=(m // tm,),
        in_specs=[pl.BlockSpec((tm, d), lambda i: (i, 0)),
                  pl.BlockSpec((1, d), lambda i: (0, 0))],
        out_specs=pl.BlockSpec((tm, d), lambda i: (i, 0)),
        compiler_params=_params(("parallel",)),
        name="rmsnorm",
    )(x, g.reshape(1, d).astype(F32))


def _mm_kernel(*refs, nk, act, has_bias, has_resid):
    x_ref, w_ref = refs[0], refs[1]
    idx = 2
    b_ref = r_ref = None
    if has_bias:
        b_ref = refs[idx]
        idx += 1
    if has_resid:
        r_ref = refs[idx]
        idx += 1
    o_ref = refs[idx]

    def epilogue(acc):
        if has_bias:
            acc = acc + b_ref[...]
        if act == "relu2":
            acc = jnp.square(jnp.maximum(acc, 0.0))
        if has_resid:
            acc = acc + r_ref[...]
        o_ref[...] = acc.astype(o_ref.dtype)

    if nk == 1:
        epilogue(_dot(x_ref[...], w_ref[...]))
    else:
        acc_ref = refs[idx + 1]
        k = pl.program_id(2)

        @pl.when(k == 0)
        def _():
            acc_ref[...] = jnp.zeros_like(acc_ref)

        acc_ref[...] += _dot(x_ref[...], w_ref[...])

        @pl.when(k == nk - 1)
        def _():
            epilogue(acc_ref[...])


def _mm(x, w, layer, *, bias=None, resid=None, act=None, out_dtype=BF16, tm=1024, tn=512, tk=None, name="mm"):
    m, kdim = x.shape
    n = w.shape[2]
    tm, tn = min(tm, m), min(tn, n)
    tk = kdim if tk is None else min(tk, kdim)
    nk = kdim // tk
    assert m % tm == 0 and n % tn == 0 and kdim % tk == 0
    in_specs = [pl.BlockSpec((tm, tk), lambda i, j, k: (i, k)),
                pl.BlockSpec((None, tk, tn), lambda i, j, k: (layer, k, j))]
    args = [x, w]
    if bias is not None:
        in_specs.append(pl.BlockSpec((None, 1, tn), lambda i, j, k: (layer, 0, j)))
        args.append(bias)
    if resid is not None:
        in_specs.append(pl.BlockSpec((tm, tn), lambda i, j, k: (i, j)))
        args.append(resid)
    scratch = [pltpu.VMEM((tm, tn), F32)] if nk > 1 else []
    return pl.pallas_call(
        functools.partial(_mm_kernel, nk=nk, act=act, has_bias=bias is not None,
                          has_resid=resid is not None),
        out_shape=jax.ShapeDtypeStruct((m, n), out_dtype),
        grid=(m // tm, n // tn, nk),
        in_specs=in_specs,
        out_specs=pl.BlockSpec((tm, tn), lambda i, j, k: (i, j)),
        scratch_shapes=scratch,
        compiler_params=_params(("parallel", "parallel", "arbitrary")),
        name=name,
    )(*args)


def _mlstm_kernel(q_ref, k_ref, v_ref, g_ref, o_ref, c_ref, m_ref, *, nc, heads, dk, dv, scale):
    lc = SCAN_CHUNK
    rev = pl.program_id(0) % 2
    c = pl.program_id(1)
    cc = c + rev * (nc - 1 - 2 * c)

    @pl.when(c == 0)
    def _():
        c_ref[...] = jnp.zeros_like(c_ref)
        m_ref[...] = jnp.zeros_like(m_ref)

    row = lax.broadcasted_iota(jnp.int32, (lc, lc), 0)
    col = lax.broadcasted_iota(jnp.int32, (lc, lc), 1)
    tri = ((col - row) * (1 - 2 * rev)) <= 0
    eye = row == col
    ones_blk = (lax.broadcasted_iota(jnp.int32, (lc, LANES), 1) == 0).astype(BF16)
    gl = g_ref[cc]

    for hh in range(heads):
        ig = gl[2 * hh:2 * hh + 1, :]
        lf = _log_sigmoid(gl[2 * hh + 1:2 * hh + 2, :])
        b_col = jnp.sum(jnp.where(tri, jnp.broadcast_to(lf, (lc, lc)), 0.0), axis=1, keepdims=True)
        b_row = jnp.sum(jnp.where(eye, b_col, 0.0), axis=0, keepdims=True)
        g = jnp.sum(lf, axis=1, keepdims=True)
        m_prev = m_ref[hh]

        log_d = jnp.where(tri, b_col - b_row + ig, NEG_BIG)
        m_inter = b_col + m_prev
        m_j = jnp.maximum(m_inter, jnp.max(log_d, axis=1, keepdims=True))
        p = jnp.exp(log_d - m_j)

        q = q_ref[:, hh * dk:(hh + 1) * dk]
        k = k_ref[:, hh * dk:(hh + 1) * dk]
        v_aug = jnp.concatenate([v_ref[:, hh * dv:(hh + 1) * dv], ones_blk], axis=1)
        smat = (_dot_nt(q, k) * p).astype(BF16)
        w_inter = jnp.exp(m_inter - m_j)
        num_aug = (_dot(smat, v_aug) + w_inter * _dot(q, c_ref[hh].astype(BF16))) * scale
        num = num_aug[:, :dv]
        den = num_aug[:, dv:dv + 1]
        o_ref[:, hh * dv:(hh + 1) * dv] = (num / jnp.maximum(jnp.abs(den), jnp.exp(-m_j))).astype(o_ref.dtype)

        a_row = g - b_row + ig
        m_new = jnp.maximum(g + m_prev, jnp.max(a_row, axis=1, keepdims=True))
        wk_row = jnp.exp(a_row - m_new)
        wk_col = jnp.sum(jnp.where(eye, jnp.broadcast_to(wk_row, (lc, lc)), 0.0), axis=1, keepdims=True)
        decay = jnp.exp(g + m_prev - m_new)
        kw = (k.astype(F32) * wk_col).astype(BF16)
        c_ref[hh] = decay * c_ref[hh] + _dot_tn(kw, v_aug)
        m_ref[hh] = m_new


def _mlstm(proj, gates, *, batch, seq, q_off, k_off, v_off, heads, dk, dv):
    lc = SCAN_CHUNK
    nc = seq // lc
    assert dk == lc and seq % lc == 0 and dv % LANES == 0

    def rowblk(s, c):
        return (s // 2) * nc + c + (s % 2) * (nc - 1 - 2 * c)

    wqk, wv = heads * dk, heads * dv
    return pl.pallas_call(
        functools.partial(_mlstm_kernel, nc=nc, heads=heads, dk=dk, dv=dv, scale=dk ** -0.5),
        out_shape=jax.ShapeDtypeStruct((2, batch * seq, wv), F32),
        grid=(batch * 2, nc),
        in_specs=[
            pl.BlockSpec((lc, wqk), lambda s, c: (rowblk(s, c), q_off // wqk)),
            pl.BlockSpec((lc, wqk), lambda s, c: (rowblk(s, c), k_off // wqk)),
            pl.BlockSpec((lc, wv), lambda s, c: (rowblk(s, c), v_off // wv)),
            pl.BlockSpec((None, nc, 2 * heads, lc), lambda s, c: (s, 0, 0, 0)),
        ],
        out_specs=pl.BlockSpec((None, lc, wv), lambda s, c: (s % 2, rowblk(s, c), 0)),
        scratch_shapes=[pltpu.VMEM((heads, dk, dv + LANES), F32), pltpu.VMEM((heads, 1, 1), F32)],
        compiler_params=_params(("parallel", "arbitrary")),
        name="mlstm_scan",
    )(proj, proj, proj, gates)


def _gla_levels():
    lc = SCAN_CHUNK
    nlev = int(math.log2(lc)) + 1
    t = np.arange(lc)
    diff = np.zeros((2, nlev * lc, lc), np.float32)
    mask = np.zeros((2, nlev, lc, lc), np.float32)
    u = t[None, :]
    for d in range(2):
        cum = (u <= t[:, None]) if d == 0 else (u >= t[:, None])
        diff[d, :lc] = cum
        mask[d, 0] = np.eye(lc)
        for l in range(1, nlev):
            w = 1 << (l - 1)
            pair = t // (2 * w)
            second = (t % (2 * w)) >= w
            if d == 0:
                bd = pair * 2 * w + w - 1
                cum_bd = u <= bd[:, None]
                is_q, is_k = second, ~second
            else:
                bd = pair * 2 * w + w
                cum_bd = u >= bd[:, None]
                is_q, is_k = ~second, second
            diff[d, l * lc:(l + 1) * lc] = cum.astype(np.float32) - cum_bd.astype(np.float32)
            mask[d, l] = ((pair[:, None] == pair[None, :]) & is_q[:, None] & is_k[None, :])
    return diff, mask


def _gla_kernel(q_ref, k_ref, v_ref, sm_ref, up_ref, bias_ref, d_ref, mask_ref, o_ref, s_ref, *,
                heads, dk, dv, nlev, scale):
    lc = SCAN_CHUNK
    c = pl.program_id(1)

    @pl.when(c == 0)
    def _():
        s_ref[...] = jnp.zeros_like(s_ref)

    row = lax.broadcasted_iota(jnp.int32, (lc, lc), 0)
    col = lax.broadcasted_iota(jnp.int32, (lc, lc), 1)
    eye = row == col
    sm = sm_ref[...]
    dmat = d_ref[...]

    for hh in range(heads):
        z = _dot3(sm, up_ref[hh]) + bias_ref[hh]
        la = _log_sigmoid(z) * (1.0 / GLA_TAU)
        x = sum(_dot(dmat, piece) for piece in _split_bf16(la, 3))
        b = x[:lc]
        g_row = jnp.sum(la, axis=0, keepdims=True)

        qb = q_ref[:, hh * dk:(hh + 1) * dk]
        kb = k_ref[:, hh * dk:(hh + 1) * dk]
        v = v_ref[:, hh * dv:(hh + 1) * dv]
        q = qb.astype(F32)
        k = kb.astype(F32)
        att = mask_ref[0] * _dot_nt(qb, kb)
        for l in range(1, nlev):
            e = jnp.exp(-jnp.abs(x[l * lc:(l + 1) * lc]))
            att = att + mask_ref[l] * _dot_nt((q * e).astype(BF16), (k * e).astype(BF16))

        o = _dot(att.astype(BF16), v) + _dot((q * jnp.exp(b)).astype(BF16), s_ref[hh].astype(BF16))
        o_ref[:, hh * dv:(hh + 1) * dv] = (o * scale).astype(o_ref.dtype)

        eg_col = jnp.sum(jnp.where(eye, jnp.broadcast_to(jnp.exp(g_row), (lc, lc)), 0.0),
                         axis=1, keepdims=True)
        kd = (k * jnp.exp(g_row - b)).astype(BF16)
        s_ref[hh] = eg_col * s_ref[hh] + _dot_tn(kd, v)


def _gla(proj, small, up_pad, bias, *, batch, seq, q_off, k_off, v_off, heads, dk, dv):
    lc = SCAN_CHUNK
    nc = seq // lc
    assert dk == lc and seq % lc == 0
    diff, mask = _gla_levels()
    nlev = mask.shape[1]

    def rowblk(s, c):
        return (s // 2) * nc + c + (s % 2) * (nc - 1 - 2 * c)

    wqk, wv = heads * dk, heads * dv
    return pl.pallas_call(
        functools.partial(_gla_kernel, heads=heads, dk=dk, dv=dv, nlev=nlev, scale=dk ** -0.5),
        out_shape=jax.ShapeDtypeStruct((2, batch * seq, wv), F32),
        grid=(batch * 2, nc),
        in_specs=[
            pl.BlockSpec((lc, wqk), lambda s, c: (rowblk(s, c), q_off // wqk)),
            pl.BlockSpec((lc, wqk), lambda s, c: (rowblk(s, c), k_off // wqk)),
            pl.BlockSpec((lc, wv), lambda s, c: (rowblk(s, c), v_off // wv)),
            pl.BlockSpec((lc, LANES), lambda s, c: (rowblk(s, c), 0)),
            pl.BlockSpec((None, heads, LANES, dk), lambda s, c: (s % 2, 0, 0, 0)),
            pl.BlockSpec((None, heads, 1, dk), lambda s, c: (s % 2, 0, 0, 0)),
            pl.BlockSpec((None, nlev * lc, lc), lambda s, c: (s % 2, 0, 0)),
            pl.BlockSpec((None, nlev, lc, lc), lambda s, c: (s % 2, 0, 0, 0)),
        ],
        out_specs=pl.BlockSpec((None, lc, wv), lambda s, c: (s % 2, rowblk(s, c), 0)),
        scratch_shapes=[pltpu.VMEM((heads, dk, dv), F32)],
        compiler_params=_params(("parallel", "arbitrary")),
        name="gla_scan",
    )(proj, proj, proj, small, up_pad, bias, jnp.asarray(diff, BF16), jnp.asarray(mask, F32))


def _headnorm_gate_kernel(h_ref, g_ref, hn_ref, o_ref, *, heads, dv, act):
    h = h_ref[0] + h_ref[1]
    outs = []
    for hh in range(heads):
        blk = h[:, hh * dv:(hh + 1) * dv]
        r = lax.rsqrt(jnp.mean(blk * blk, axis=-1, keepdims=True) + EPS)
        outs.append(blk * r)
    hn = jnp.concatenate(outs, axis=1) * hn_ref[...]
    gate = g_ref[...].astype(F32)
    sig = 1.0 / (1.0 + jnp.exp(-gate))
    gate = sig if act == "sigmoid" else gate * sig
    o_ref[...] = (hn * gate).astype(o_ref.dtype)


def _headnorm_gate(h2, proj, gate_off, head_norm, *, heads, dv, act, tm=512):
    _, m, width = h2.shape
    tm = min(tm, m)
    return pl.pallas_call(
        functools.partial(_headnorm_gate_kernel, heads=heads, dv=dv, act=act),
        out_shape=jax.ShapeDtypeStruct((m, width), BF16),
        grid=(m // tm,),
        in_specs=[pl.BlockSpec((2, tm, width), lambda i: (0, i, 0)),
                  pl.BlockSpec((tm, width), lambda i: (i, gate_off // width)),
                  pl.BlockSpec((1, width), lambda i: (0, 0))],
        out_specs=pl.BlockSpec((tm, width), lambda i: (i, 0)),
        compiler_params=_params(("parallel",)),
        name="headnorm_gate",
    )(h2, proj, head_norm.reshape(1, width).astype(F32))


def _short_conv_kernel(x0_ref, x1_ref, x2_ref, w_ref, b_ref, o0_ref, o1_ref, o2_ref, s_ref):
    seq = x0_ref.shape[0]
    n2, half = o0_ref.shape[0], o0_ref.shape[1]
    for p, (x_ref, o_ref) in enumerate(((x0_ref, o0_ref), (x1_ref, o1_ref), (x2_ref, o2_ref))):
        x = x_ref[...].astype(F32)
        w = w_ref[p]
        bias = b_ref[p]
        s_ref[...] = (pltpu.roll(x, 1, axis=0) * w[0:1] + x * w[1:2]
                      + pltpu.roll(x, seq - 1, axis=0) * w[2:3] + bias)
        s_ref[0:1, :] = x[0:1] * w[1:2] + x[1:2] * w[2:3] + bias
        s_ref[seq - 1:seq, :] = x[seq - 2:seq - 1] * w[0:1] + x[seq - 1:seq] * w[1:2] + bias

        def body(j, carry, o_ref=o_ref):
            o_ref[j] = s_ref[pl.ds(j, half, stride=n2), :].astype(o_ref.dtype)
            return carry

        lax.fori_loop(0, n2, body, 0, unroll=GATHER_UNROLL)


def _short_conv(proj, conv_w, conv_b, *, batch, seq, off, ch, n2):
    tc = LANES
    nct = ch // tc
    half = seq // n2
    w = conv_w.reshape(3, 3, nct, tc).transpose(1, 2, 0, 3).astype(F32)
    b = conv_b.reshape(3, nct, 1, tc).astype(F32)
    in_specs = [pl.BlockSpec((seq, tc), functools.partial(lambda b_, j, p: (b_, off // tc + p * nct + j), p=p))
                for p in range(3)]
    in_specs += [pl.BlockSpec((3, None, 3, tc), lambda b_, j: (0, j, 0, 0)),
                 pl.BlockSpec((3, None, 1, tc), lambda b_, j: (0, j, 0, 0))]
    out = jax.ShapeDtypeStruct((batch, n2, half, ch), BF16)
    return pl.pallas_call(
        _short_conv_kernel,
        out_shape=[out, out, out],
        grid=(batch, nct),
        in_specs=in_specs,
        out_specs=[pl.BlockSpec((None, n2, half, tc), lambda b_, j: (b_, 0, 0, j))] * 3,
        scratch_shapes=[pltpu.VMEM((seq, tc), F32)],
        compiler_params=_params(("parallel", "parallel")),
        name="hyena_short_conv",
    )(proj, proj, proj, w, b)


def _filter_kernel(z_ref, w1_ref, b1_ref, w2_ref, b2_ref, w3_ref, f_ref, dl_ref, o_ref, *, n1, ch, order, emb):
    z = z_ref[...]
    rows = z.shape[0]
    f = f_ref[...]
    h = jnp.sin(f * (_dot3(z, w1_ref[...]) + b1_ref[...]))
    h = jnp.sin(f * (_dot3(h, w2_ref[...]) + b2_ref[...]))
    out = _dot3(h, w3_ref[...])
    win = jnp.exp(-z[:, 0:1] * dl_ref[...]) * z[:, emb:emb + 1]
    outer = lax.broadcasted_iota(jnp.int32, (rows, 1), 0) % n1
    fwd = outer < n1 // 2
    for o in range(order):
        sel = jnp.where(fwd, out[:, (2 * o) * ch:(2 * o + 1) * ch], out[:, (2 * o + 1) * ch:(2 * o + 2) * ch])
        o_ref[:, :, o * ch:(o + 1) * ch] = (sel * win).reshape(rows // n1, n1, ch).astype(o_ref.dtype)


def _hyena_taps(seq, w1, b1, w2, b2, w3, freq, ch, n2, tb=4):
    emb, hid = w1.shape
    n1 = 2 * seq // n2
    a = jnp.arange(n1, dtype=jnp.int32)[None, :]
    bdig = jnp.arange(n2, dtype=jnp.int32)[:, None]
    lag = (n2 * a + bdig).reshape(-1)
    pos = jnp.where(lag < seq, lag, 2 * seq - lag)
    n = pos.astype(F32)
    t = n / (seq - 1)
    bands = jnp.linspace(1e-4, HYENA_BANDS - 1, HYENA_BANDS, dtype=F32)
    ang = (2.0 * math.pi * n / seq)[:, None] * bands[None, :]
    valid = (lag != seq).astype(F32)
    z = jnp.concatenate([t[:, None], jnp.cos(ang), -jnp.sin(ang), valid[:, None]], axis=-1)
    z = jnp.pad(z, ((0, 0), (0, LANES - emb - 1)))
    deltas = jnp.abs(jnp.linspace(HYENA_MIN_DECAY, HYENA_MAX_DECAY, ch, dtype=F32)).reshape(1, ch)
    pad_h = LANES - hid
    w1p = jnp.pad(w1.astype(F32), ((0, LANES - emb), (0, pad_h)))
    w2p = jnp.pad(w2.astype(F32), ((0, pad_h), (0, pad_h)))
    w3p = jnp.pad(w3.astype(F32), ((0, pad_h), (0, 0)))
    row = lambda v: jnp.pad(v.astype(F32), (0, pad_h)).reshape(1, LANES)
    order = w3.shape[1] // (2 * ch)
    tb = min(tb, n2)
    full = lambda shape: pl.BlockSpec(shape, lambda i: (0, 0))
    return pl.pallas_call(
        functools.partial(_filter_kernel, n1=n1, ch=ch, order=order, emb=emb),
        out_shape=jax.ShapeDtypeStruct((n2, n1, order * ch), BF16),
        grid=(n2 // tb,),
        in_specs=[pl.BlockSpec((tb * n1, LANES), lambda i: (i, 0)),
                  full((LANES, LANES)), full((1, LANES)), full((LANES, LANES)), full((1, LANES)),
                  full((LANES, 2 * order * ch)), full((1, LANES)), full((1, ch))],
        out_specs=pl.BlockSpec((tb, n1, order * ch), lambda i: (i, 0, 0)),
        compiler_params=_params(("parallel",)),
        name="hyena_filters",
    )(z, w1p, row(b1), w2p, row(b2), w3p, row(freq), deltas)


def _fft_tables(n1, n2):
    n = n1 * n2
    a = jnp.arange(n1, dtype=jnp.int32)
    ang1 = ((a[:, None] * a[None, :]) % n1).astype(F32) * (2.0 * math.pi / n1)
    f1r, f1i = jnp.cos(ang1), -jnp.sin(ang1)
    c = jnp.arange(n1, dtype=jnp.int32)[:, None, None]
    d = jnp.arange(n2, dtype=jnp.int32)[None, :, None]
    b = jnp.arange(n2, dtype=jnp.int32)[None, None, :]
    ang = ((b * (c + n1 * d)) % n).astype(F32) * (2.0 * math.pi / n)
    gr, gi = jnp.cos(ang), -jnp.sin(ang)
    gblk = jnp.concatenate([jnp.concatenate([gr, -gi], axis=2),
                            jnp.concatenate([gi, gr], axis=2)], axis=1).astype(BF16)
    return f1r, f1i, gblk, gblk.transpose(0, 2, 1)


def _stage_rows(t_ref, row0, val):
    rows = val.shape[0]
    for j in range(t_ref.shape[0]):
        t_ref[j, pl.ds(row0, rows), :] = val[:, j * LANES:(j + 1) * LANES]


def _gather_rows(t_ref, start, count, stride, dtype):
    parts = [t_ref[j, pl.ds(start, count, stride=stride), :] for j in range(t_ref.shape[0])]
    return jnp.concatenate(parts, axis=1).astype(dtype)


def _fft1_kernel(f_ref, u_ref, o_ref, t_ref, *, n1):
    tile = u_ref.shape[0]
    rows = 2 * n1
    for s in range(tile):
        _stage_rows(t_ref, s * rows, _dot(f_ref[...], u_ref[s]))

    def body(c, carry):
        for ri in range(2):
            o_ref[ri, c] = _gather_rows(t_ref, ri * n1 + c, tile, rows, o_ref.dtype)
        return carry

    lax.fori_loop(0, n1, body, 0, unroll=min(GATHER_UNROLL, n1))


def _fft_stage1(fstack, u, *, n1, ct=512):
    nb, n2, kin, ch = u.shape
    tile = min(BF16_TILE_ROWS, n2)
    ct = min(ct, ch)
    return pl.pallas_call(
        functools.partial(_fft1_kernel, n1=n1),
        out_shape=jax.ShapeDtypeStruct((nb, 2, n1, n2, ch), BF16),
        grid=(nb, ch // ct, n2 // tile),
        in_specs=[pl.BlockSpec((2 * n1, kin), lambda b, j, i: (0, 0)),
                  pl.BlockSpec((None, tile, kin, ct), lambda b, j, i: (b, i, 0, j))],
        out_specs=pl.BlockSpec((None, 2, n1, tile, ct), lambda b, j, i: (b, 0, 0, i, j)),
        scratch_shapes=[pltpu.VMEM((ct // LANES, tile * 2 * n1, LANES), F32)],
        compiler_params=_params(("parallel", "parallel", "parallel")),
        name="fft_stage1",
    )(fstack, u)


def _fft_spec_kernel(a_ref, g_ref, o_ref, *, inv_n):
    for i in range(a_ref.shape[1]):
        a = jnp.concatenate([a_ref[0, i], a_ref[1, i]], axis=0)
        o_ref[i] = _dot(g_ref[i], a) * inv_n


def _fft_spectrum(a, gblk, *, n1, n2, ch, ct=512):
    ct = min(ct, ch)
    tile = min(BF16_TILE_ROWS, n1)
    return pl.pallas_call(
        functools.partial(_fft_spec_kernel, inv_n=1.0 / (n1 * n2)),
        out_shape=jax.ShapeDtypeStruct((n1, 2 * n2, ch), F32),
        grid=(ch // ct, n1 // tile),
        in_specs=[pl.BlockSpec((None, 2, tile, n2, ct), lambda j, c: (0, 0, c, 0, j)),
                  pl.BlockSpec((tile, 2 * n2, 2 * n2), lambda j, c: (c, 0, 0))],
        out_specs=pl.BlockSpec((tile, 2 * n2, ct), lambda j, c: (c, 0, j)),
        compiler_params=_params(("parallel", "parallel")),
        name="fft_filter_spectrum",
    )(a, gblk)


def _fft_mid_kernel(a_ref, g_ref, gt_ref, kf_ref, o_ref, t_ref, *, n2):
    tile = a_ref.shape[1]
    rows = 2 * n2

    def one(i, carry):
        a = jnp.concatenate([a_ref[0, i], a_ref[1, i]], axis=0)
        x = _dot(g_ref[i], a)
        xr, xi = x[:n2], x[n2:]
        kf = kf_ref[i]
        kr, ki = kf[:n2], kf[n2:]
        y = jnp.concatenate([xr * kr - xi * ki, xr * ki + xi * kr], axis=0).astype(BF16)
        _stage_rows(t_ref, pl.multiple_of(i * rows, rows), _dot(gt_ref[i], y))
        return carry

    lax.fori_loop(0, tile, one, 0, unroll=min(4, tile))

    def body(r, carry):
        for ri in range(2):
            o_ref[ri, r] = _gather_rows(t_ref, ri * n2 + r, tile, rows, o_ref.dtype)
        return carry

    lax.fori_loop(0, n2, body, 0, unroll=min(GATHER_UNROLL, n2))


def _fft_mid(a, gblk, gblk_t, kf, *, kf_off, ct=256):
    nb, _, n1, n2, ch = a.shape
    tile = min(BF16_TILE_ROWS, n1)
    ct = min(ct, ch)
    g_spec = pl.BlockSpec((tile, 2 * n2, 2 * n2), lambda b, j, c: (c, 0, 0))
    return pl.pallas_call(
        functools.partial(_fft_mid_kernel, n2=n2),
        out_shape=jax.ShapeDtypeStruct((nb, 2, n2, n1, ch), BF16),
        grid=(nb, ch // ct, n1 // tile),
        in_specs=[pl.BlockSpec((None, 2, tile, n2, ct), lambda b, j, c: (b, 0, c, 0, j)),
                  g_spec, g_spec,
                  pl.BlockSpec((tile, 2 * n2, ct), lambda b, j, c: (c, 0, kf_off // ct + j))],
        out_specs=pl.BlockSpec((None, 2, n2, tile, ct), lambda b, j, c: (b, 0, 0, c, j)),
        scratch_shapes=[pltpu.VMEM((ct // LANES, tile * 2 * n2, LANES), F32)],
        compiler_params=_params(("parallel", "parallel", "parallel")),
        name="fft_spectral",
    )(a, gblk, gblk_t, kf)


def _fft3_kernel(fr_ref, fi_ref, z_ref, u_ref, gate_ref, skip_ref, o_ref, *scratch, time_major):
    tile, half = u_ref.shape[0], u_ref.shape[1]
    for s in range(tile):
        y = _dot(fr_ref[...], z_ref[0, s]) + _dot(fi_ref[...], z_ref[1, s])
        y = gate_ref[s].astype(F32) * (y + skip_ref[...] * u_ref[s].astype(F32))
        if time_major:
            _stage_rows(scratch[0], s * half, y)
        else:
            o_ref[s] = y.astype(o_ref.dtype)
    if time_major:
        def body(a, carry):
            o_ref[a] = _gather_rows(scratch[0], a, tile, half, o_ref.dtype)
            return carry

        lax.fori_loop(0, half, body, 0, unroll=min(GATHER_UNROLL, half))


def _fft_stage3(f1r_h, f1i_h, z, u, gate, skip, *, time_major, ct=512):
    nb, _, n2, n1, ch = z.shape
    half = n1 // 2
    tile = min(BF16_TILE_ROWS, n2)
    ct = min(ct, ch)
    sig_spec = pl.BlockSpec((None, tile, half, ct), lambda b, j, i: (b, i, 0, j))
    if time_major:
        out_shape = jax.ShapeDtypeStruct((nb, half, n2, ch), BF16)
        out_spec = pl.BlockSpec((None, half, tile, ct), lambda b, j, i: (b, 0, i, j))
        scratch = [pltpu.VMEM((ct // LANES, tile * half, LANES), F32)]
    else:
        out_shape = jax.ShapeDtypeStruct((nb, n2, half, ch), BF16)
        out_spec = sig_spec
        scratch = []
    return pl.pallas_call(
        functools.partial(_fft3_kernel, time_major=time_major),
        out_shape=out_shape,
        grid=(nb, ch // ct, n2 // tile),
        in_specs=[pl.BlockSpec((half, n1), lambda b, j, i: (0, 0)),
                  pl.BlockSpec((half, n1), lambda b, j, i: (0, 0)),
                  pl.BlockSpec((None, 2, tile, n1, ct), lambda b, j, i: (b, 0, i, 0, j)),
                  sig_spec, sig_spec,
                  pl.BlockSpec((1, ct), lambda b, j, i: (0, j))],
        out_specs=out_spec,
        scratch_shapes=scratch,
        compiler_params=_params(("parallel", "parallel", "parallel")),
        name="fft_stage3",
    )(f1r_h, f1i_h, z, u, gate, skip.reshape(1, ch).astype(F32))


def _hyena(proj, hy_off, conv_w, conv_b, w1, b1, w2, b2, w3, freq, skip, *, batch, seq, ch):
    n2 = FFT_N2
    n1 = 2 * seq // n2
    half = n1 // 2
    v, x1, x2 = _short_conv(proj, conv_w, conv_b, batch=batch, seq=seq, off=hy_off, ch=ch, n2=n2)
    taps = _hyena_taps(seq, w1, b1, w2, b2, w3, freq, ch, n2)
    f1r, f1i, gblk, gblk_t = _fft_tables(n1, n2)
    fstack = jnp.concatenate([f1r, f1i], axis=0)
    ka = _fft_stage1(fstack.astype(BF16), taps[None], n1=n1)
    kf = _fft_spectrum(ka, gblk, n1=n1, n2=n2, ch=HYENA_ORDER * ch)
    fdata = fstack[:, :half].astype(BF16)
    f1r_h, f1i_h = f1r[:half].astype(BF16), f1i[:half].astype(BF16)
    zcur = v
    for o, gate in enumerate((x1, x2)):
        a = _fft_stage1(fdata, zcur, n1=n1)
        zmid = _fft_mid(a, gblk, gblk_t, kf, kf_off=o * ch)
        zcur = _fft_stage3(f1r_h, f1i_h, zmid, zcur, gate, skip[o], time_major=(o == HYENA_ORDER - 1))
    return zcur.reshape(batch * seq, ch)


def _merge_kernel(ya_ref, yb_ref, yc_ref, ga_ref, gb_ref, gc_ref, w_ref, o_ref):
    acc = None
    for n, (y_ref, g_ref) in enumerate(((ya_ref, ga_ref), (yb_ref, gb_ref), (yc_ref, gc_ref))):
        gate = 1.0 / (1.0 + jnp.exp(-g_ref[...].astype(F32)))
        term = gate * _dot(y_ref[...], w_ref[n])
        acc = term if acc is None else acc + term
    o_ref[...] = acc.astype(o_ref.dtype)


def _merge(ys, proj, gate_off, w_branch, layer, *, d_model, tm=1024, tn=512):
    m, width = ys[0].shape
    tm, tn = min(tm, m), min(tn, d_model)
    y_spec = pl.BlockSpec((tm, width), lambda i, j: (i, 0))
    g_specs = [pl.BlockSpec((tm, tn), functools.partial(
        lambda i, j, n: (i, (gate_off + n * d_model) // tn + j), n=n)) for n in range(N_BRANCH)]
    return pl.pallas_call(
        _merge_kernel,
        out_shape=jax.ShapeDtypeStruct((m, d_model), BF16),
        grid=(m // tm, d_model // tn),
        in_specs=[y_spec, y_spec, y_spec] + g_specs +
                 [pl.BlockSpec((None, N_BRANCH, width, tn), lambda i, j: (layer, 0, 0, j))],
        out_specs=pl.BlockSpec((tm, tn), lambda i, j: (i, j)),
        compiler_params=_params(("parallel", "parallel")),
        name="branch_merge",
    )(*ys, proj, proj, proj, w_branch)


def _xattn_kernel(q_ref, k_ref, v_ref, o_ref, *, scale):
    s = _dot_nt(q_ref[...], k_ref[...]) * scale
    s = s - jnp.max(s, axis=-1, keepdims=True)
    p = jnp.exp(s)
    l = jnp.sum(p, axis=-1, keepdims=True)
    o_ref[...] = (_dot(p.astype(BF16), v_ref[...]) / l).astype(o_ref.dtype)


def _xattn(q, kv, *, batch, seq, mem, heads, hd, tq=512):
    tq = min(tq, seq)
    nq = seq // tq
    return pl.pallas_call(
        functools.partial(_xattn_kernel, scale=hd ** -0.5),
        out_shape=jax.ShapeDtypeStruct(q.shape, BF16),
        grid=(batch, heads, nq),
        in_specs=[pl.BlockSpec((tq, hd), lambda b, h, i: (b * nq + i, h)),
                  pl.BlockSpec((mem, hd), lambda b, h, i: (b, h)),
                  pl.BlockSpec((mem, hd), lambda b, h, i: (b, heads + h))],
        out_specs=pl.BlockSpec((tq, hd), lambda b, h, i: (b * nq + i, h)),
        compiler_params=_params(("parallel", "parallel", "parallel")),
        name="xattn_core",
    )(q, kv, kv)


def _in_proj_layout(d):
    mix = d // 4
    m_dk = mix // MLSTM_HEADS // 2
    g_dk = mix // GLA_HEADS // 2
    sizes = (MLSTM_HEADS * m_dk, MLSTM_HEADS * m_dk, mix, mix, 4 * MLSTM_HEADS,
             GLA_HEADS * g_dk, GLA_HEADS * g_dk, mix, mix, 2 * GLA_RANK, 3 * mix, N_BRANCH * d)
    offs = np.concatenate([[0], np.cumsum(sizes)])
    wide = (0, 1, 2, 3, 5, 6, 7, 8, 10, 11)
    narrow = (4, 9)
    new_off, pos = {}, 0
    for i in wide:
        new_off[i] = pos
        pos += sizes[i]
    return sizes, offs, wide, narrow, new_off


def kernel(x, mem, norm_gains, final_norm, w_in, b_in, mlstm_head_norm, gla_decay_up, gla_decay_bias, gla_head_norm, hyena_conv_w, hyena_conv_b, hyena_ffn_w1, hyena_ffn_b1, hyena_ffn_w2, hyena_ffn_b2, hyena_ffn_w3, hyena_freq, hyena_skip, w_branch, w_out, xattn_wq, xattn_wkv, xattn_wo, mlp_w1, mlp_w2):
    batch, seq, d = x.shape
    mem_tokens = mem.shape[1]
    depth = norm_gains.shape[0]
    mix = d // 4
    mh, gh = MLSTM_HEADS, GLA_HEADS
    m_dv, g_dv = mix // mh, mix // gh
    m_dk, g_dk = m_dv // 2, g_dv // 2
    nc = seq // SCAN_CHUNK

    sizes, offs, wide, narrow, new_off = _in_proj_layout(d)
    seg = lambda a, i: a[..., offs[i]:offs[i + 1]]
    w_main = jnp.concatenate([seg(w_in, i) for i in wide], axis=2).astype(BF16)
    b_main = jnp.concatenate([seg(b_in, i) for i in wide], axis=1)[:, None, :].astype(F32)
    n_small = sum(sizes[i] for i in narrow)
    w_small = jnp.pad(jnp.concatenate([seg(w_in, i) for i in narrow], axis=2),
                      ((0, 0), (0, 0), (0, LANES - n_small))).astype(BF16)
    b_small = jnp.pad(jnp.concatenate([seg(b_in, i) for i in narrow], axis=1),
                      ((0, 0), (0, LANES - n_small)))[:, None, :].astype(F32)
    wb16, wo16 = w_branch.astype(BF16), w_out.astype(BF16)
    wq16, wkv16, wxo16 = xattn_wq.astype(BF16), xattn_wkv.astype(BF16), xattn_wo.astype(BF16)
    w1_16, w2_16 = mlp_w1.astype(BF16), mlp_w2.astype(BF16)

    h = x.reshape(batch * seq, d)
    memf = mem.reshape(batch * mem_tokens, d)
    for l in range(depth):
        g = norm_gains[l]
        xn = _rmsnorm(h, g[0], BF16)
        proj = _mm(xn, w_main, l, bias=b_main, out_dtype=BF16, tn=1024, name="in_proj")
        small = _mm(xn, w_small, l, bias=b_small, out_dtype=F32, name="in_proj_gates")

        mg = small[:, :4 * mh].reshape(batch, nc, SCAN_CHUNK, 2, 2, mh)
        mgates = mg.transpose(0, 3, 1, 5, 4, 2).reshape(batch * 2, nc, 2 * mh, SCAN_CHUNK)
        h_m = _mlstm(proj, mgates, batch=batch, seq=seq, q_off=new_off[0], k_off=new_off[1],
                     v_off=new_off[2], heads=mh, dk=m_dk, dv=m_dv)
        y_a = _headnorm_gate(h_m, proj, new_off[3], mlstm_head_norm[l], heads=mh, dv=m_dv, act="sigmoid")

        up = gla_decay_up[l].reshape(2, GLA_RANK, gh, g_dk).transpose(0, 2, 1, 3).astype(F32)
        up_pad = jnp.zeros((2, gh, LANES, g_dk), F32)
        for dr in range(2):
            r0 = 4 * mh + dr * GLA_RANK
            up_pad = up_pad.at[dr, :, r0:r0 + GLA_RANK, :].set(up[dr])
        gbias = gla_decay_bias[l].reshape(2, gh, 1, g_dk).astype(F32)
        h_g = _gla(proj, small, up_pad, gbias, batch=batch, seq=seq, q_off=new_off[5], k_off=new_off[6],
                   v_off=new_off[7], heads=gh, dk=g_dk, dv=g_dv)
        y_b = _headnorm_gate(h_g, proj, new_off[8], gla_head_norm[l], heads=gh, dv=g_dv, act="silu")

        y_c = _hyena(proj, new_off[10], hyena_conv_w[l], hyena_conv_b[l], hyena_ffn_w1[l], hyena_ffn_b1[l],
                     hyena_ffn_w2[l], hyena_ffn_b2[l], hyena_ffn_w3[l], hyena_freq[l], hyena_skip[l],
                     batch=batch, seq=seq, ch=mix)

        merged = _merge((y_a, y_b, y_c), proj, new_off[11], wb16, l, d_model=d)
        h = _mm(merged, wo16, l, resid=h, out_dtype=F32, name="mixer_out")

        xn = _rmsnorm(h, g[1], BF16)
        memn = _rmsnorm(memf, g[2], BF16)
        q = _mm(xn, wq16, l, out_dtype=BF16, tn=1024, name="xattn_q")
        kv = _mm(memn, wkv16, l, out_dtype=BF16, tn=1024, name="xattn_kv")
        o = _xattn(q, kv, batch=batch, seq=seq, mem=mem_tokens, heads=XATTN_HEADS, hd=d // XATTN_HEADS)
        h = _mm(o, wxo16, l, resid=h, out_dtype=F32, name="xattn_out")

        xn = _rmsnorm(h, g[3], BF16)
        hid = _mm(xn, w1_16, l, act="relu2", out_dtype=BF16, tn=1024, name="mlp_up")
        h = _mm(hid, w2_16, l, resid=h, out_dtype=F32, tk=4096, name="mlp_down")
    return _rmsnorm(h, final_norm, x.dtype).reshape(batch, seq, d)
```

```python
import functools
import math

import numpy as np
import jax
import jax.numpy as jnp
from jax import lax
from jax.experimental import pallas as pl
from jax.experimental.pallas import tpu as pltpu

F32 = jnp.float32
BF16 = jnp.bfloat16

N_BRANCH = 3
MLSTM_HEADS = 4
GLA_HEADS = 4
GLA_RANK = 16
GLA_TAU = 16.0
HYENA_ORDER = 2
HYENA_BANDS = 16
HYENA_MIN_DECAY = math.log(1e-2) / 1.5
HYENA_MAX_DECAY = math.log(1e-2) / 0.3
XATTN_HEADS = 4
EPS = 1e-6

LANES = 128
BF16_TILE_ROWS = 16
VMEM_LIMIT_BYTES = 56 * 1024 * 1024

SCAN_CHUNK = 128
FFT_N2 = 128
NEG_BIG = -1e30
GATHER_UNROLL = 8


def _params(sem):
    return pltpu.CompilerParams(dimension_semantics=sem, vmem_limit_bytes=VMEM_LIMIT_BYTES)


def _log_sigmoid(x):
    return -(jnp.maximum(-x, 0.0) + jnp.log(1.0 + jnp.exp(-jnp.abs(x))))


def _split_bf16(a, parts):
    out = []
    r = a
    for _ in range(parts):
        p = r.astype(BF16)
        out.append(p)
        r = r - p.astype(F32)
    return out


def _dot(a, b):
    return jnp.dot(a, b, preferred_element_type=F32)


def _dot_nt(a, b):
    return lax.dot_general(a, b, (((1,), (1,)), ((), ())), preferred_element_type=F32)


def _dot_tn(a, b):
    return lax.dot_general(a, b, (((0,), (0,)), ((), ())), preferred_element_type=F32)


def _dot3(a, b):
    a_hi, a_lo = _split_bf16(a, 2)
    b_hi, b_lo = _split_bf16(b, 2)
    return _dot(a_hi, b_hi) + _dot(a_hi, b_lo) + _dot(a_lo, b_hi)


def _rmsnorm_kernel(x_ref, g_ref, o_ref):
    x = x_ref[...].astype(F32)
    r = lax.rsqrt(jnp.mean(x * x, axis=-1, keepdims=True) + EPS)
    o_ref[...] = ((x * r) * g_ref[...]).astype(o_ref.dtype)


def _rmsnorm(x, g, out_dtype, tm=256):
    m, d = x.shape
    tm = min(tm, m)
    return pl.pallas_call(
        _rmsnorm_kernel,
        out_shape=jax.ShapeDtypeStruct((m, d), out_dtype),
        grid=(m // tm,),
        in_specs=[pl.BlockSpec((tm, d), lambda i: (i, 0)),
                  pl.BlockSpec((1, d), lambda i: (0, 0))],
        out_specs=pl.BlockSpec((tm, d), lambda i: (i, 0)),
        compiler_params=_params(("parallel",)),
        name="rmsnorm",
    )(x, g.reshape(1, d).astype(F32))


def _mm_kernel(*refs, nk, act, has_bias, has_resid):
    x_ref, w_ref = refs[0], refs[1]
    idx = 2
    b_ref = r_ref = None
    if has_bias:
        b_ref = refs[idx]
        idx += 1
    if has_resid:
        r_ref = refs[idx]
        idx += 1
    o_ref = refs[idx]

    def epilogue(acc):
        if has_bias:
            acc = acc + b_ref[...]
        if act == "relu2":
            acc = jnp.square(jnp.maximum(acc, 0.0))
        if has_resid:
            acc = acc + r_ref[...]
        o_ref[...] = acc.astype(o_ref.dtype)

    if nk == 1:
        epilogue(_dot(x_ref[...], w_ref[...]))
    else:
        acc_ref = refs[idx + 1]
        k = pl.program_id(2)

        @pl.when(k == 0)
        def _():
            acc_ref[...] = jnp.zeros_like(acc_ref)

        acc_ref[...] += _dot(x_ref[...], w_ref[...])

        @pl.when(k == nk - 1)
        def _():
            epilogue(acc_ref[...])


def _mm(x, w, layer, *, bias=None, resid=None, act=None, out_dtype=BF16, tm=1024, tn=512, tk=None, name="mm"):
    m, kdim = x.shape
    n = w.shape[2]
    tm, tn = min(tm, m), min(tn, n)
    tk = kdim if tk is None else min(tk, kdim)
    nk = kdim // tk
    assert m % tm == 0 and n % tn == 0 and kdim % tk == 0
    in_specs = [pl.BlockSpec((tm, tk), lambda i, j, k: (i, k)),
                pl.BlockSpec((None, tk, tn), lambda i, j, k: (layer, k, j))]
    args = [x, w]
    if bias is not None:
        in_specs.append(pl.BlockSpec((None, 1, tn), lambda i, j, k: (layer, 0, j)))
        args.append(bias)
    if resid is not None:
        in_specs.append(pl.BlockSpec((tm, tn), lambda i, j, k: (i, j)))
        args.append(resid)
    scratch = [pltpu.VMEM((tm, tn), F32)] if nk > 1 else []
    return pl.pallas_call(
        functools.partial(_mm_kernel, nk=nk, act=act, has_bias=bias is not None,
                          has_resid=resid is not None),
        out_shape=jax.ShapeDtypeStruct((m, n), out_dtype),
        grid=(m // tm, n // tn, nk),
        in_specs=in_specs,
        out_specs=pl.BlockSpec((tm, tn), lambda i, j, k: (i, j)),
        scratch_shapes=scratch,
        compiler_params=_params(("parallel", "parallel", "arbitrary")),
        name=name,
    )(*args)


def _mlstm_kernel(q_ref, k_ref, v_ref, g_ref, o_ref, c_ref, m_ref, *, nc, heads, dk, dv, scale):
    lc = SCAN_CHUNK
    rev = pl.program_id(0) % 2
    c = pl.program_id(1)
    cc = c + rev * (nc - 1 - 2 * c)

    @pl.when(c == 0)
    def _():
        c_ref[...] = jnp.zeros_like(c_ref)
        m_ref[...] = jnp.zeros_like(m_ref)

    row = lax.broadcasted_iota(jnp.int32, (lc, lc), 0)
    col = lax.broadcasted_iota(jnp.int32, (lc, lc), 1)
    tri = ((col - row) * (1 - 2 * rev)) <= 0
    eye = row == col
    ones_blk = (lax.broadcasted_iota(jnp.int32, (lc, LANES), 1) == 0).astype(BF16)
    gl = g_ref[cc]

    for hh in range(heads):
        ig = gl[2 * hh:2 * hh + 1, :]
        lf = _log_sigmoid(gl[2 * hh + 1:2 * hh + 2, :])
        b_col = jnp.sum(jnp.where(tri, jnp.broadcast_to(lf, (lc, lc)), 0.0), axis=1, keepdims=True)
        b_row = jnp.sum(jnp.where(eye, b_col, 0.0), axis=0, keepdims=True)
        g = jnp.sum(lf, axis=1, keepdims=True)
        m_prev = m_ref[hh]

        log_d = jnp.where(tri, b_col - b_row + ig, NEG_BIG)
        m_inter = b_col + m_prev
        m_j = jnp.maximum(m_inter, jnp.max(log_d, axis=1, keepdims=True))
        p = jnp.exp(log_d - m_j)

        q = q_ref[:, hh * dk:(hh + 1) * dk]
        k = k_ref[:, hh * dk:(hh + 1) * dk]
        v_aug = jnp.concatenate([v_ref[:, hh * dv:(hh + 1) * dv], ones_blk], axis=1)
        smat = (_dot_nt(q, k) * p).astype(BF16)
        w_inter = jnp.exp(m_inter - m_j)
        num_aug = (_dot(smat, v_aug) + w_inter * _dot(q, c_ref[hh].astype(BF16))) * scale
        num = num_aug[:, :dv]
        den = num_aug[:, dv:dv + 1]
        o_ref[:, hh * dv:(hh + 1) * dv] = (num / jnp.maximum(jnp.abs(den), jnp.exp(-m_j))).astype(o_ref.dtype)

        a_row = g - b_row + ig
        m_new = jnp.maximum(g + m_prev, jnp.max(a_row, axis=1, keepdims=True))
        wk_row = jnp.exp(a_row - m_new)
        wk_col = jnp.sum(jnp.where(eye, jnp.broadcast_to(wk_row, (lc, lc)), 0.0), axis=1, keepdims=True)
        decay = jnp.exp(g + m_prev - m_new)
        kw = (k.astype(F32) * wk_col).astype(BF16)
        c_ref[hh] = decay * c_ref[hh] + _dot_tn(kw, v_aug)
        m_ref[hh] = m_new


def _mlstm(proj, gates, *, batch, seq, q_off, k_off, v_off, heads, dk, dv):
    lc = SCAN_CHUNK
    nc = seq // lc
    assert dk == lc and seq % lc == 0 and dv % LANES == 0

    def rowblk(s, c):
        return (s // 2) * nc + c + (s % 2) * (nc - 1 - 2 * c)

    wqk, wv = heads * dk, heads * dv
    return pl.pallas_call(
        functools.partial(_mlstm_kernel, nc=nc, heads=heads, dk=dk, dv=dv, scale=dk ** -0.5),
        out_shape=jax.ShapeDtypeStruct((2, batch * seq, wv), F32),
        grid=(batch * 2, nc),
        in_specs=[
            pl.BlockSpec((lc, wqk), lambda s, c: (rowblk(s, c), q_off // wqk)),
            pl.BlockSpec((lc, wqk), lambda s, c: (rowblk(s, c), k_off // wqk)),
            pl.BlockSpec((lc, wv), lambda s, c: (rowblk(s, c), v_off // wv)),
            pl.BlockSpec((None, nc, 2 * heads, lc), lambda s, c: (s, 0, 0, 0)),
        ],
        out_specs=pl.BlockSpec((None, lc, wv), lambda s, c: (s % 2, rowblk(s, c), 0)),
        scratch_shapes=[pltpu.VMEM((heads, dk, dv + LANES), F32), pltpu.VMEM((heads, 1, 1), F32)],
        compiler_params=_params(("parallel", "arbitrary")),
        name="mlstm_scan",
    )(proj, proj, proj, gates)


def _gla_levels():
    lc = SCAN_CHUNK
    nlev = int(math.log2(lc)) + 1
    t = np.arange(lc)
    diff = np.zeros((2, nlev * lc, lc), np.float32)
    mask = np.zeros((2, nlev, lc, lc), np.float32)
    u = t[None, :]
    for d in range(2):
        cum = (u <= t[:, None]) if d == 0 else (u >= t[:, None])
        diff[d, :lc] = cum
        mask[d, 0] = np.eye(lc)
        for l in range(1, nlev):
            w = 1 << (l - 1)
            pair = t // (2 * w)
            second = (t % (2 * w)) >= w
            if d == 0:
                bd = pair * 2 * w + w - 1
                cum_bd = u <= bd[:, None]
                is_q, is_k = second, ~second
            else:
                bd = pair * 2 * w + w
                cum_bd = u >= bd[:, None]
                is_q, is_k = ~second, second
            diff[d, l * lc:(l + 1) * lc] = cum.astype(np.float32) - cum_bd.astype(np.float32)
            mask[d, l] = ((pair[:, None] == pair[None, :]) & is_q[:, None] & is_k[None, :])
    return diff, mask


def _gla_kernel(q_ref, k_ref, v_ref, sm_ref, up_ref, bias_ref, d_ref, mask_ref, o_ref, s_ref, *,
                heads, dk, dv, nlev, scale):
    lc = SCAN_CHUNK
    c = pl.program_id(1)

    @pl.when(c == 0)
    def _():
        s_ref[...] = jnp.zeros_like(s_ref)

    row = lax.broadcasted_iota(jnp.int32, (lc, lc), 0)
    col = lax.broadcasted_iota(jnp.int32, (lc, lc), 1)
    eye = row == col
    sm = sm_ref[...]
    dmat = d_ref[...]

    for hh in range(heads):
        z = _dot3(sm, up_ref[hh]) + bias_ref[hh]
        la = _log_sigmoid(z) * (1.0 / GLA_TAU)
        x = sum(_dot(dmat, piece) for piece in _split_bf16(la, 3))
        b = x[:lc]
        g_row = jnp.sum(la, axis=0, keepdims=True)

        qb = q_ref[:, hh * dk:(hh + 1) * dk]
        kb = k_ref[:, hh * dk:(hh + 1) * dk]
        v = v_ref[:, hh * dv:(hh + 1) * dv]
        q = qb.astype(F32)
        k = kb.astype(F32)
        att = mask_ref[0] * _dot_nt(qb, kb)
        for l in range(1, nlev):
            e = jnp.exp(-jnp.abs(x[l * lc:(l + 1) * lc]))
            att = att + mask_ref[l] * _dot_nt((q * e).astype(BF16), (k * e).astype(BF16))

        o = _dot(att.astype(BF16), v) + _dot((q * jnp.exp(b)).astype(BF16), s_ref[hh].astype(BF16))
        o_ref[:, hh * dv:(hh + 1) * dv] = (o * scale).astype(o_ref.dtype)

        eg_col = jnp.sum(jnp.where(eye, jnp.broadcast_to(jnp.exp(g_row), (lc, lc)), 0.0),
                         axis=1, keepdims=True)
        kd = (k * jnp.exp(g_row - b)).astype(BF16)
        s_ref[hh] = eg_col * s_ref[hh] + _dot_tn(kd, v)


def _gla(proj, small, up_pad, bias, *, batch, seq, q_off, k_off, v_off, heads, dk, dv):
    lc = SCAN_CHUNK
    nc = seq // lc
    assert dk == lc and seq % lc == 0
    diff, mask = _gla_levels()
    nlev = mask.shape[1]

    def rowblk(s, c):
        return (s // 2) * nc + c + (s % 2) * (nc - 1 - 2 * c)

    wqk, wv = heads * dk, heads * dv
    return pl.pallas_call(
        functools.partial(_gla_kernel, heads=heads, dk=dk, dv=dv, nlev=nlev, scale=dk ** -0.5),
        out_shape=jax.ShapeDtypeStruct((2, batch * seq, wv), F32),
        grid=(batch * 2, nc),
        in_specs=[
            pl.BlockSpec((lc, wqk), lambda s, c: (rowblk(s, c), q_off // wqk)),
            pl.BlockSpec((lc, wqk), lambda s, c: (rowblk(s, c), k_off // wqk)),
            pl.BlockSpec((lc, wv), lambda s, c: (rowblk(s, c), v_off // wv)),
            pl.BlockSpec((lc, LANES), lambda s, c: (rowblk(s, c), 0)),
            pl.BlockSpec((None, heads, LANES, dk), lambda s, c: (s % 2, 0, 0, 0)),
            pl.BlockSpec((None, heads, 1, dk), lambda s, c: (s % 2, 0, 0, 0)),
            pl.BlockSpec((None, nlev * lc, lc), lambda s, c: (s % 2, 0, 0)),
            pl.BlockSpec((None, nlev, lc, lc), lambda s, c: (s % 2, 0, 0, 0)),
        ],
        out_specs=pl.BlockSpec((None, lc, wv), lambda s, c: (s % 2, rowblk(s, c), 0)),
        scratch_shapes=[pltpu.VMEM((heads, dk, dv), F32)],
        compiler_params=_params(("parallel", "arbitrary")),
        name="gla_scan",
    )(proj, proj, proj, small, up_pad, bias, jnp.asarray(diff, BF16), jnp.asarray(mask, F32))


def _headnorm_gate_kernel(h_ref, g_ref, hn_ref, o_ref, *, heads, dv, act):
    h = h_ref[0] + h_ref[1]
    outs = []
    for hh in range(heads):
        blk = h[:, hh * dv:(hh + 1) * dv]
        r = lax.rsqrt(jnp.mean(blk * blk, axis=-1, keepdims=True) + EPS)
        outs.append(blk * r)
    hn = jnp.concatenate(outs, axis=1) * hn_ref[...]
    gate = g_ref[...].astype(F32)
    sig = 1.0 / (1.0 + jnp.exp(-gate))
    gate = sig if act == "sigmoid" else gate * sig
    o_ref[...] = (hn * gate).astype(o_ref.dtype)


def _headnorm_gate(h2, proj, gate_off, head_norm, *, heads, dv, act, tm=512):
    _, m, width = h2.shape
    tm = min(tm, m)
    return pl.pallas_call(
        functools.partial(_headnorm_gate_kernel, heads=heads, dv=dv, act=act),
        out_shape=jax.ShapeDtypeStruct((m, width), BF16),
        grid=(m // tm,),
        in_specs=[pl.BlockSpec((2, tm, width), lambda i: (0, i, 0)),
                  pl.BlockSpec((tm, width), lambda i: (i, gate_off // width)),
                  pl.BlockSpec((1, width), lambda i: (0, 0))],
        out_specs=pl.BlockSpec((tm, width), lambda i: (i, 0)),
        compiler_params=_params(("parallel",)),
        name="headnorm_gate",
    )(h2, proj, head_norm.reshape(1, width).astype(F32))


def _cm_short_conv_kernel(x0_ref, x1_ref, x2_ref, w_ref, b_ref, o0_ref, o1_ref, o2_ref, s_ref, t_ref):
    seq, tc = x0_ref.shape
    half, n2 = o0_ref.shape[1], o0_ref.shape[2]
    for p, (x_ref, o_ref) in enumerate(((x0_ref, o0_ref), (x1_ref, o1_ref), (x2_ref, o2_ref))):
        x = x_ref[...].astype(F32)
        w = w_ref[p]
        bias = b_ref[p]
        s_ref[...] = (pltpu.roll(x, 1, axis=0) * w[0:1] + x * w[1:2]
                      + pltpu.roll(x, seq - 1, axis=0) * w[2:3] + bias)
        s_ref[0:1, :] = x[0:1] * w[1:2] + x[1:2] * w[2:3] + bias
        s_ref[seq - 1:seq, :] = x[seq - 2:seq - 1] * w[0:1] + x[seq - 1:seq] * w[1:2] + bias

        def flip(a, carry):
            slab = s_ref[pl.ds(pl.multiple_of(a * n2, n2), n2), :]
            t_ref[pl.ds(pl.multiple_of(a * tc, tc), tc), :] = slab.T
            return carry

        lax.fori_loop(0, half, flip, 0, unroll=min(4, half))

        def body(ch, carry, o_ref=o_ref):
            o_ref[ch] = t_ref[pl.ds(ch, half, stride=tc), :].astype(o_ref.dtype)
            return carry

        lax.fori_loop(0, tc, body, 0, unroll=GATHER_UNROLL)


def _cm_short_conv(proj, conv_w, conv_b, *, batch, seq, off, ch, n2):
    tc = LANES
    nct = ch // tc
    half = seq // n2
    w = conv_w.reshape(3, 3, nct, tc).transpose(1, 2, 0, 3).astype(F32)
    b = conv_b.reshape(3, nct, 1, tc).astype(F32)
    in_specs = [pl.BlockSpec((seq, tc), functools.partial(lambda b_, j, p: (b_, off // tc + p * nct + j), p=p))
                for p in range(3)]
    in_specs += [pl.BlockSpec((3, None, 3, tc), lambda b_, j: (0, j, 0, 0)),
                 pl.BlockSpec((3, None, 1, tc), lambda b_, j: (0, j, 0, 0))]
    out = jax.ShapeDtypeStruct((batch, ch, half, n2), BF16)
    return pl.pallas_call(
        _cm_short_conv_kernel,
        out_shape=[out, out, out],
        grid=(batch, nct),
        in_specs=in_specs,
        out_specs=[pl.BlockSpec((None, tc, half, n2), lambda b_, j: (b_, j, 0, 0))] * 3,
        scratch_shapes=[pltpu.VMEM((seq, tc), F32), pltpu.VMEM((half * tc, n2), F32)],
        compiler_params=_params(("parallel", "parallel")),
        name="hyena_short_conv",
    )(proj, proj, proj, w, b)


def _cm_taps_kernel(z_ref, w1_ref, b1_ref, w2_ref, b2_ref, w3_ref, f_ref, dl_ref, o_ref, t_ref, h_ref, *, emb, n2):
    @pl.when(pl.program_id(1) == 0)
    def _():
        f = f_ref[...]
        h = jnp.sin(f * (_dot3(w1_ref[...], z_ref[...]) + b1_ref[...]))
        h_ref[...] = jnp.sin(f * (_dot3(w2_ref[...], h) + b2_ref[...]))

    win = jnp.exp(-dl_ref[...] * z_ref[0:1, :]) * z_ref[emb:emb + 1, :]
    out = _dot3(w3_ref[...], h_ref[...]) * win
    gc = out.shape[0]
    tile = out.shape[1] // n2
    for s in range(tile):
        t_ref[s * gc:(s + 1) * gc, :] = out[:, s * n2:(s + 1) * n2]

    def body(ch, carry):
        o_ref[ch] = t_ref[pl.ds(ch, tile, stride=gc), :].astype(o_ref.dtype)
        return carry

    lax.fori_loop(0, gc, body, 0, unroll=GATHER_UNROLL)


def _cm_taps(seq, w1, b1, w2, b2, w3, freq, ch, n2, gc=256):
    emb, hid = w1.shape
    n1 = 2 * seq // n2
    half = n1 // 2
    tile = min(BF16_TILE_ROWS, half)
    order = w3.shape[1] // (2 * ch)
    lag = jnp.arange(2 * seq, dtype=jnp.int32)
    pos = jnp.where(lag < seq, lag, 2 * seq - lag)
    n = pos.astype(F32)
    t = n / (seq - 1)
    bands = jnp.linspace(1e-4, HYENA_BANDS - 1, HYENA_BANDS, dtype=F32)
    ang = (2.0 * math.pi * n / seq)[:, None] * bands[None, :]
    valid = (lag != seq).astype(F32)
    z = jnp.concatenate([t[:, None], jnp.cos(ang), -jnp.sin(ang), valid[:, None]], axis=-1)
    zt = jnp.pad(z, ((0, 0), (0, LANES - emb - 1))).T
    deltas = jnp.abs(jnp.linspace(HYENA_MIN_DECAY, HYENA_MAX_DECAY, ch, dtype=F32))
    dl = jnp.tile(deltas, order).reshape(order * ch, 1)
    pad_h = LANES - hid
    w1t = jnp.pad(w1.astype(F32), ((0, LANES - emb), (0, pad_h))).T
    w2t = jnp.pad(w2.astype(F32), ((0, pad_h), (0, pad_h))).T
    w3t = jnp.pad(w3.astype(F32), ((0, pad_h), (0, 0))).T.reshape(order, 2, ch, LANES)
    w3t = w3t.transpose(1, 0, 2, 3).reshape(2, order * ch, LANES)
    colv = lambda v: jnp.pad(v.astype(F32), (0, pad_h)).reshape(LANES, 1)
    gc = min(gc, order * ch)
    nat = n1 // tile
    full = lambda shape: pl.BlockSpec(shape, lambda a, j: (0, 0))
    return pl.pallas_call(
        functools.partial(_cm_taps_kernel, emb=emb, n2=n2),
        out_shape=jax.ShapeDtypeStruct((order * ch, n1, n2), BF16),
        grid=(nat, order * ch // gc),
        in_specs=[pl.BlockSpec((LANES, tile * n2), lambda a, j: (0, a)),
                  full((LANES, LANES)), full((LANES, 1)), full((LANES, LANES)), full((LANES, 1)),
                  pl.BlockSpec((None, gc, LANES), lambda a, j: (a // (nat // 2), j, 0)),
                  full((LANES, 1)),
                  pl.BlockSpec((gc, 1), lambda a, j: (j, 0))],
        out_specs=pl.BlockSpec((gc, tile, n2), lambda a, j: (j, a, 0)),
        scratch_shapes=[pltpu.VMEM((tile * gc, n2), F32), pltpu.VMEM((LANES, tile * n2), F32)],
        compiler_params=_params(("parallel", "arbitrary")),
        name="hyena_filters",
    )(zt, w1t, colv(b1), w2t, colv(b2), w3t, colv(freq), dl)


def _cm_tables(n1, n2):
    a = jnp.arange(n1, dtype=jnp.int32)
    ang1 = ((a[:, None] * a[None, :]) % n1).astype(F32) * (2.0 * math.pi / n1)
    f1r, f1i = jnp.cos(ang1), -jnp.sin(ang1)
    b = jnp.arange(n2, dtype=jnp.int32)
    angt = (a[:, None] * b[None, :]).astype(F32) * (2.0 * math.pi / (n1 * n2))
    tw = jnp.stack([jnp.cos(angt), -jnp.sin(angt)])
    ang2 = ((b[:, None] * b[None, :]) % n2).astype(F32) * (2.0 * math.pi / n2)
    wr, wi = jnp.cos(ang2), -jnp.sin(ang2)
    wb = jnp.concatenate([jnp.concatenate([wr, wi], axis=1),
                          jnp.concatenate([-wi, wr], axis=1)], axis=0).astype(BF16)
    return f1r, f1i, tw, wb, wb.T


def _cm_forward(x_ref, f1, twr, twi, wb, n1):
    ys = []
    for g in range(x_ref.shape[0]):
        p = _dot(f1, x_ref[g])
        pr, pi = p[:n1], p[n1:]
        ys.append(jnp.concatenate([pr * twr - pi * twi, pr * twi + pi * twr], axis=1))
    y = jnp.concatenate(ys, axis=0).astype(BF16)
    return _dot(y, wb)


def _cm_spec_kernel(t_ref, f1_ref, tw_ref, wb_ref, o_ref, *, n1, inv_n):
    z = _cm_forward(t_ref, f1_ref[...], tw_ref[0], tw_ref[1], wb_ref[...], n1)
    o_ref[...] = (z * inv_n).reshape(o_ref.shape)


def _cm_spectrum(taps, f1, tw, wb, *, group=16):
    nch, n1, n2 = taps.shape
    group = min(group, nch)
    full2 = lambda shape: pl.BlockSpec(shape, lambda j: (0,) * len(shape))
    return pl.pallas_call(
        functools.partial(_cm_spec_kernel, n1=n1, inv_n=1.0 / (n1 * n2)),
        out_shape=jax.ShapeDtypeStruct((nch, n1, 2 * n2), F32),
        grid=(nch // group,),
        in_specs=[pl.BlockSpec((group, n1, n2), lambda j: (j, 0, 0)),
                  full2((2 * n1, n1)), full2((2, n1, n2)), full2((2 * n2, 2 * n2))],
        out_specs=pl.BlockSpec((group, n1, 2 * n2), lambda j: (j, 0, 0)),
        compiler_params=_params(("parallel",)),
        name="hyena_filter_spectrum",
    )(taps, f1, tw, wb)


def _cm_conv_kernel(x_ref, gate_ref, kf_ref, skip_ref, f1_ref, f1h_ref, tw_ref, wb_ref, wbt_ref, o_ref, *, n1, n2):
    group = x_ref.shape[0]
    twr, twi = tw_ref[0], tw_ref[1]
    z = _cm_forward(x_ref, f1_ref[...], twr, twi, wb_ref[...], n1)
    kf = kf_ref[...].reshape(group * n1, 2 * n2)
    zr, zi, kr, ki = z[:, :n2], z[:, n2:], kf[:, :n2], kf[:, n2:]
    s = jnp.concatenate([zr * kr - zi * ki, zr * ki + zi * kr], axis=1).astype(BF16)
    v = _dot(s, wbt_ref[...])
    f1h = f1h_ref[...]
    for g in range(group):
        vr, vi = v[g * n1:(g + 1) * n1, :n2], v[g * n1:(g + 1) * n1, n2:]
        u = jnp.concatenate([vr * twr + vi * twi, vi * twr - vr * twi], axis=0).astype(BF16)
        y = _dot(f1h, u)
        y = y + skip_ref[g] * x_ref[g].astype(F32)
        o_ref[g] = (gate_ref[g].astype(F32) * y).astype(o_ref.dtype)


def _cm_conv(x, gate, kf, skip, f1d, f1h, tw, wb, wbt, *, kf_off, out_dtype, group=16):
    batch, ch, half, n2 = x.shape
    n1 = 2 * half
    group = min(group, ch)
    sig = pl.BlockSpec((None, group, half, n2), lambda j, b: (b, j, 0, 0))
    full2 = lambda shape: pl.BlockSpec(shape, lambda j, b: (0,) * len(shape))
    skip_b = jnp.broadcast_to(skip.astype(F32)[:, None, None], (ch, 1, n2))
    return pl.pallas_call(
        functools.partial(_cm_conv_kernel, n1=n1, n2=n2),
        out_shape=jax.ShapeDtypeStruct((batch, ch, half, n2), out_dtype),
        grid=(ch // group, batch),
        in_specs=[sig, sig,
                  pl.BlockSpec((group, n1, 2 * n2), lambda j, b: (kf_off // group + j, 0, 0)),
                  pl.BlockSpec((group, 1, n2), lambda j, b: (j, 0, 0)),
                  full2((2 * n1, half)), full2((half, 2 * n1)), full2((2, n1, n2)),
                  full2((2 * n2, 2 * n2)), full2((2 * n2, 2 * n2))],
        out_specs=sig,
        compiler_params=_params(("parallel", "parallel")),
        name="hyena_fft_conv",
    )(x, gate, kf, skip_b, f1d, f1h, tw, wb, wbt)


def _cm_to_time_major_kernel(y_ref, o_ref, *, half):
    tc = o_ref.shape[1]
    n2 = y_ref.shape[1]

    def body(a, carry):
        slab = y_ref[pl.ds(a, tc, stride=half), :]
        o_ref[pl.ds(pl.multiple_of(a * n2, n2), n2), :] = slab.T.astype(o_ref.dtype)
        return carry

    lax.fori_loop(0, half, body, 0, unroll=min(4, half))


def _cm_to_time_major(y, *, batch, seq, ch):
    half, n2 = y.shape[2], y.shape[3]
    tc = LANES
    return pl.pallas_call(
        functools.partial(_cm_to_time_major_kernel, half=half),
        out_shape=jax.ShapeDtypeStruct((batch * seq, ch), BF16),
        grid=(batch, ch // tc),
        in_specs=[pl.BlockSpec((None, tc * half, n2), lambda b, j: (b, j, 0))],
        out_specs=pl.BlockSpec((seq, tc), lambda b, j: (b, j)),
        compiler_params=_params(("parallel", "parallel")),
        name="hyena_to_time_major",
    )(y.reshape(batch, ch * half, n2))


def _hyena_cm(proj, hy_off, conv_w, conv_b, w1, b1, w2, b2, w3, freq, skip, *, batch, seq, ch):
    n2 = FFT_N2
    n1 = 2 * seq // n2
    half = n1 // 2
    v, x1, x2 = _cm_short_conv(proj, conv_w, conv_b, batch=batch, seq=seq, off=hy_off, ch=ch, n2=n2)
    taps = _cm_taps(seq, w1, b1, w2, b2, w3, freq, ch, n2)
    f1r, f1i, tw, wb, wbt = _cm_tables(n1, n2)
    fstack = jnp.concatenate([f1r, f1i], axis=0)
    kf = _cm_spectrum(taps, fstack.astype(BF16), tw, wb)
    f1d = fstack[:, :half].astype(BF16)
    f1h = jnp.concatenate([f1r[:half], f1i[:half]], axis=1).astype(BF16)
    zcur = v
    for o, gate in enumerate((x1, x2)):
        last = o == HYENA_ORDER - 1
        zcur = _cm_conv(zcur, gate, kf, skip[o], f1d, f1h, tw, wb, wbt, kf_off=o * ch,
                        out_dtype=F32 if last else BF16)
    return _cm_to_time_major(zcur, batch=batch, seq=seq, ch=ch)


def _merge_kernel(ya_ref, yb_ref, yc_ref, ga_ref, gb_ref, gc_ref, w_ref, o_ref):
    acc = None
    for n, (y_ref, g_ref) in enumerate(((ya_ref, ga_ref), (yb_ref, gb_ref), (yc_ref, gc_ref))):
        gate = 1.0 / (1.0 + jnp.exp(-g_ref[...].astype(F32)))
        term = gate * _dot(y_ref[...], w_ref[n])
        acc = term if acc is None else acc + term
    o_ref[...] = acc.astype(o_ref.dtype)


def _merge(ys, proj, gate_off, w_branch, layer, *, d_model, tm=1024, tn=512):
    m, width = ys[0].shape
    tm, tn = min(tm, m), min(tn, d_model)
    y_spec = pl.BlockSpec((tm, width), lambda i, j: (i, 0))
    g_specs = [pl.BlockSpec((tm, tn), functools.partial(
        lambda i, j, n: (i, (gate_off + n * d_model) // tn + j), n=n)) for n in range(N_BRANCH)]
    return pl.pallas_call(
        _merge_kernel,
        out_shape=jax.ShapeDtypeStruct((m, d_model), BF16),
        grid=(m // tm, d_model // tn),
        in_specs=[y_spec, y_spec, y_spec] + g_specs +
                 [pl.BlockSpec((None, N_BRANCH, width, tn), lambda i, j: (layer, 0, 0, j))],
        out_specs=pl.BlockSpec((tm, tn), lambda i, j: (i, j)),
        compiler_params=_params(("parallel", "parallel")),
        name="branch_merge",
    )(*ys, proj, proj, proj, w_branch)


def _xattn_kernel(q_ref, k_ref, v_ref, o_ref, *, scale):
    s = _dot_nt(q_ref[...], k_ref[...]) * scale
    s = s - jnp.max(s, axis=-1, keepdims=True)
    p = jnp.exp(s)
    l = jnp.sum(p, axis=-1, keepdims=True)
    o_ref[...] = (_dot(p.astype(BF16), v_ref[...]) / l).astype(o_ref.dtype)


def _xattn(q, kv, *, batch, seq, mem, heads, hd, tq=512):
    tq = min(tq, seq)
    nq = seq // tq
    return pl.pallas_call(
        functools.partial(_xattn_kernel, scale=hd ** -0.5),
        out_shape=jax.ShapeDtypeStruct(q.shape, BF16),
        grid=(batch, heads, nq),
        in_specs=[pl.BlockSpec((tq, hd), lambda b, h, i: (b * nq + i, h)),
                  pl.BlockSpec((mem, hd), lambda b, h, i: (b, h)),
                  pl.BlockSpec((mem, hd), lambda b, h, i: (b, heads + h))],
        out_specs=pl.BlockSpec((tq, hd), lambda b, h, i: (b * nq + i, h)),
        compiler_params=_params(("parallel", "parallel", "parallel")),
        name="xattn_core",
    )(q, kv, kv)


def _in_proj_layout(d):
    mix = d // 4
    m_dk = mix // MLSTM_HEADS // 2
    g_dk = mix // GLA_HEADS // 2
    sizes = (MLSTM_HEADS * m_dk, MLSTM_HEADS * m_dk, mix, mix, 4 * MLSTM_HEADS,
             GLA_HEADS * g_dk, GLA_HEADS * g_dk, mix, mix, 2 * GLA_RANK, 3 * mix, N_BRANCH * d)
    offs = np.concatenate([[0], np.cumsum(sizes)])
    wide = (0, 1, 2, 3, 5, 6, 7, 8, 10, 11)
    narrow = (4, 9)
    new_off, pos = {}, 0
    for i in wide:
        new_off[i] = pos
        pos += sizes[i]
    return sizes, offs, wide, narrow, new_off


def kernel(x, mem, norm_gains, final_norm, w_in, b_in, mlstm_head_norm, gla_decay_up, gla_decay_bias, gla_head_norm, hyena_conv_w, hyena_conv_b, hyena_ffn_w1, hyena_ffn_b1, hyena_ffn_w2, hyena_ffn_b2, hyena_ffn_w3, hyena_freq, hyena_skip, w_branch, w_out, xattn_wq, xattn_wkv, xattn_wo, mlp_w1, mlp_w2):
    batch, seq, d = x.shape
    mem_tokens = mem.shape[1]
    depth = norm_gains.shape[0]
    mix = d // 4
    mh, gh = MLSTM_HEADS, GLA_HEADS
    m_dv, g_dv = mix // mh, mix // gh
    m_dk, g_dk = m_dv // 2, g_dv // 2
    nc = seq // SCAN_CHUNK

    sizes, offs, wide, narrow, new_off = _in_proj_layout(d)
    seg = lambda a, i: a[..., offs[i]:offs[i + 1]]
    w_main = jnp.concatenate([seg(w_in, i) for i in wide], axis=2).astype(BF16)
    b_main = jnp.concatenate([seg(b_in, i) for i in wide], axis=1)[:, None, :].astype(F32)
    n_small = sum(sizes[i] for i in narrow)
    w_small = jnp.pad(jnp.concatenate([seg(w_in, i) for i in narrow], axis=2),
                      ((0, 0), (0, 0), (0, LANES - n_small))).astype(BF16)
    b_small = jnp.pad(jnp.concatenate([seg(b_in, i) for i in narrow], axis=1),
                      ((0, 0), (0, LANES - n_small)))[:, None, :].astype(F32)
    wb16, wo16 = w_branch.astype(BF16), w_out.astype(BF16)
    wq16, wkv16, wxo16 = xattn_wq.astype(BF16), xattn_wkv.astype(BF16), xattn_wo.astype(BF16)
    w1_16, w2_16 = mlp_w1.astype(BF16), mlp_w2.astype(BF16)

    h = x.reshape(batch * seq, d)
    memf = mem.reshape(batch * mem_tokens, d)
    for l in range(depth):
        g = norm_gains[l]
        xn = _rmsnorm(h, g[0], BF16)
        proj = _mm(xn, w_main, l, bias=b_main, out_dtype=BF16, tn=1024, name="in_proj")
        small = _mm(xn, w_small, l, bias=b_small, out_dtype=F32, name="in_proj_gates")

        mg = small[:, :4 * mh].reshape(batch, nc, SCAN_CHUNK, 2, 2, mh)
        mgates = mg.transpose(0, 3, 1, 5, 4, 2).reshape(batch * 2, nc, 2 * mh, SCAN_CHUNK)
        h_m = _mlstm(proj, mgates, batch=batch, seq=seq, q_off=new_off[0], k_off=new_off[1],
                     v_off=new_off[2], heads=mh, dk=m_dk, dv=m_dv)
        y_a = _headnorm_gate(h_m, proj, new_off[3], mlstm_head_norm[l], heads=mh, dv=m_dv, act="sigmoid")

        up = gla_decay_up[l].reshape(2, GLA_RANK, gh, g_dk).transpose(0, 2, 1, 3).astype(F32)
        up_pad = jnp.zeros((2, gh, LANES, g_dk), F32)
        for dr in range(2):
            r0 = 4 * mh + dr * GLA_RANK
            up_pad = up_pad.at[dr, :, r0:r0 + GLA_RANK, :].set(up[dr])
        gbias = gla_decay_bias[l].reshape(2, gh, 1, g_dk).astype(F32)
        h_g = _gla(proj, small, up_pad, gbias, batch=batch, seq=seq, q_off=new_off[5], k_off=new_off[6],
                   v_off=new_off[7], heads=gh, dk=g_dk, dv=g_dv)
        y_b = _headnorm_gate(h_g, proj, new_off[8], gla_head_norm[l], heads=gh, dv=g_dv, act="silu")

        y_c = _hyena_cm(proj, new_off[10], hyena_conv_w[l], hyena_conv_b[l], hyena_ffn_w1[l], hyena_ffn_b1[l],
                     hyena_ffn_w2[l], hyena_ffn_b2[l], hyena_ffn_w3[l], hyena_freq[l], hyena_skip[l],
                     batch=batch, seq=seq, ch=mix)

        merged = _merge((y_a, y_b, y_c), proj, new_off[11], wb16, l, d_model=d)
        h = _mm(merged, wo16, l, resid=h, out_dtype=F32, name="mixer_out")

        xn = _rmsnorm(h, g[1], BF16)
        memn = _rmsnorm(memf, g[2], BF16)
        q = _mm(xn, wq16, l, out_dtype=BF16, tn=1024, name="xattn_q")
        kv = _mm(memn, wkv16, l, out_dtype=BF16, tn=1024, name="xattn_kv")
        o = _xattn(q, kv, batch=batch, seq=seq, mem=mem_tokens, heads=XATTN_HEADS, hd=d // XATTN_HEADS)
        h = _mm(o, wxo16, l, resid=h, out_dtype=F32, name="xattn_out")

        xn = _rmsnorm(h, g[3], BF16)
        hid = _mm(xn, w1_16, l, act="relu2", out_dtype=BF16, tn=1024, name="mlp_up")
        h = _mm(hid, w2_16, l, resid=h, out_dtype=F32, tk=4096, name="mlp_down")
    return _rmsnorm(h, final_norm, x.dtype).reshape(batch, seq, d)
```

```python
import functools
import math

import numpy as np
import jax
import jax.numpy as jnp
from jax import lax
from jax.experimental import pallas as pl
from jax.experimental.pallas import tpu as pltpu

F32 = jnp.float32
BF16 = jnp.bfloat16

N_BRANCH = 3
MLSTM_HEADS = 4
GLA_HEADS = 4
GLA_RANK = 16
GLA_TAU = 16.0
HYENA_ORDER = 2
HYENA_BANDS = 16
HYENA_MIN_DECAY = math.log(1e-2) / 1.5
HYENA_MAX_DECAY = math.log(1e-2) / 0.3
XATTN_HEADS = 4
EPS = 1e-6

LANES = 128
BF16_TILE_ROWS = 16
VMEM_LIMIT_BYTES = 56 * 1024 * 1024

SCAN_CHUNK = 128
FFT_N2 = 128
NEG_BIG = -1e30
GATHER_UNROLL = 8


def _params(sem):
    return pltpu.CompilerParams(dimension_semantics=sem, vmem_limit_bytes=VMEM_LIMIT_BYTES)


def _log_sigmoid(x):
    return -(jnp.maximum(-x, 0.0) + jnp.log(1.0 + jnp.exp(-jnp.abs(x))))


def _split_bf16(a, parts):
    out = []
    r = a
    for _ in range(parts):
        p = r.astype(BF16)
        out.append(p)
        r = r - p.astype(F32)
    return out


def _dot(a, b):
    return jnp.dot(a, b, preferred_element_type=F32)


def _dot_nt(a, b):
    return lax.dot_general(a, b, (((1,), (1,)), ((), ())), preferred_element_type=F32)


def _dot_tn(a, b):
    return lax.dot_general(a, b, (((0,), (0,)), ((), ())), preferred_element_type=F32)


def _dot3(a, b):
    a_hi, a_lo = _split_bf16(a, 2)
    b_hi, b_lo = _split_bf16(b, 2)
    return _dot(a_hi, b_hi) + _dot(a_hi, b_lo) + _dot(a_lo, b_hi)


def _rmsnorm_kernel(x_ref, g_ref, o_ref):
    x = x_ref[...].astype(F32)
    r = lax.rsqrt(jnp.mean(x * x, axis=-1, keepdims=True) + EPS)
    o_ref[...] = ((x * r) * g_ref[...]).astype(o_ref.dtype)


def _rmsnorm(x, g, out_dtype, tm=256):
    m, d = x.shape
    tm = min(tm, m)
    return pl.pallas_call(
        _rmsnorm_kernel,
        out_shape=jax.ShapeDtypeStruct((m, d), out_dtype),
        grid=(m // tm,),
        in_specs=[pl.BlockSpec((tm, d), lambda i: (i, 0)),
                  pl.BlockSpec((1, d), lambda i: (0, 0))],
        out_specs=pl.BlockSpec((tm, d), lambda i: (i, 0)),
        compiler_params=_params(("parallel",)),
        name="rmsnorm",
    )(x, g.reshape(1, d).astype(F32))


def _mm_kernel(*refs, nk, act, has_norm, has_bias, has_resid, emit_stats, inv_d):
    refs = list(refs)
    x_ref, w_ref = refs.pop(0), refs.pop(0)
    g_ref, ssq_ref = (refs.pop(0), refs.pop(0)) if has_norm else (None, None)
    b_ref = refs.pop(0) if has_bias else None
    r_ref = refs.pop(0) if has_resid else None
    o_ref = refs.pop(0)
    o16_ref, stat_ref = (refs.pop(0), refs.pop(0)) if emit_stats else (None, None)

    def partial_product():
        w = w_ref[...]
        if has_norm:
            w = w.astype(F32) * g_ref[...]
        return _dot(x_ref[...], w.astype(BF16))

    def epilogue(acc):
        if has_norm:
            acc = acc * lax.rsqrt(ssq_ref[:, 0:1] * inv_d + EPS)
        if has_bias:
            acc = acc + b_ref[...]
        if act == "relu2":
            acc = jnp.square(jnp.maximum(acc, 0.0))
        if has_resid:
            acc = acc + r_ref[...]
        o_ref[...] = acc.astype(o_ref.dtype)
        if emit_stats:
            o16_ref[...] = acc.astype(BF16)
            part = jnp.broadcast_to(jnp.sum(acc * acc, axis=1, keepdims=True), stat_ref.shape)
            j = pl.program_id(1)

            @pl.when(j == 0)
            def _():
                stat_ref[...] = part

            @pl.when(j != 0)
            def _():
                stat_ref[...] += part

    if nk == 1:
        epilogue(partial_product())
    else:
        acc_ref = refs.pop(0)
        k = pl.program_id(2)

        @pl.when(k == 0)
        def _():
            acc_ref[...] = jnp.zeros_like(acc_ref)

        acc_ref[...] += partial_product()

        @pl.when(k == nk - 1)
        def _():
            epilogue(acc_ref[...])


def _mm(x, w, layer, *, norm=None, bias=None, resid=None, act=None, emit_stats=False, out_dtype=BF16,
        tm=1024, tn=512, tk=None, name="mm"):
    m, kdim = x.shape
    n = w.shape[2]
    tm, tn = min(tm, m), min(tn, n)
    tk = kdim if tk is None else min(tk, kdim)
    nk = kdim // tk
    assert m % tm == 0 and n % tn == 0 and kdim % tk == 0
    in_specs = [pl.BlockSpec((tm, tk), lambda i, j, k: (i, k)),
                pl.BlockSpec((None, tk, tn), lambda i, j, k: (layer, k, j))]
    args = [x, w]
    if norm is not None:
        gain, ssq = norm
        in_specs += [pl.BlockSpec((tk, 1), lambda i, j, k: (k, 0)),
                     pl.BlockSpec((tm, LANES), lambda i, j, k: (i, 0))]
        args += [gain.reshape(kdim, 1).astype(F32), ssq]
    if bias is not None:
        in_specs.append(pl.BlockSpec((None, 1, tn), lambda i, j, k: (layer, 0, j)))
        args.append(bias)
    if resid is not None:
        in_specs.append(pl.BlockSpec((tm, tn), lambda i, j, k: (i, j)))
        args.append(resid)
    tile_spec = pl.BlockSpec((tm, tn), lambda i, j, k: (i, j))
    out_shape = jax.ShapeDtypeStruct((m, n), out_dtype)
    out_specs = tile_spec
    if emit_stats:
        out_shape = [out_shape, jax.ShapeDtypeStruct((m, n), BF16), jax.ShapeDtypeStruct((m, LANES), F32)]
        out_specs = [tile_spec, tile_spec, pl.BlockSpec((tm, LANES), lambda i, j, k: (i, 0))]
    scratch = [pltpu.VMEM((tm, tn), F32)] if nk > 1 else []
    return pl.pallas_call(
        functools.partial(_mm_kernel, nk=nk, act=act, has_norm=norm is not None, has_bias=bias is not None,
                          has_resid=resid is not None, emit_stats=emit_stats, inv_d=1.0 / kdim),
        out_shape=out_shape,
        grid=(m // tm, n // tn, nk),
        in_specs=in_specs,
        out_specs=out_specs,
        scratch_shapes=scratch,
        compiler_params=_params(("parallel", "arbitrary", "arbitrary")),
        name=name,
    )(*args)


def _stream_stats_kernel(x_ref, o16_ref, stat_ref):
    x = x_ref[...]
    o16_ref[...] = x.astype(BF16)
    stat_ref[...] = jnp.broadcast_to(jnp.sum(x * x, axis=1, keepdims=True), stat_ref.shape)


def _stream_stats(x, tm=256):
    m, d = x.shape
    tm = min(tm, m)
    return pl.pallas_call(
        _stream_stats_kernel,
        out_shape=[jax.ShapeDtypeStruct((m, d), BF16), jax.ShapeDtypeStruct((m, LANES), F32)],
        grid=(m // tm,),
        in_specs=[pl.BlockSpec((tm, d), lambda i: (i, 0))],
        out_specs=[pl.BlockSpec((tm, d), lambda i: (i, 0)), pl.BlockSpec((tm, LANES), lambda i: (i, 0))],
        compiler_params=_params(("parallel",)),
        name="stream_stats",
    )(x)


def _mlstm_kernel(q_ref, k_ref, v_ref, g_ref, o_ref, c_ref, m_ref, *, nc, heads, dk, dv, scale):
    lc = SCAN_CHUNK
    rev = pl.program_id(0) % 2
    c = pl.program_id(1)
    cc = c + rev * (nc - 1 - 2 * c)

    @pl.when(c == 0)
    def _():
        c_ref[...] = jnp.zeros_like(c_ref)
        m_ref[...] = jnp.zeros_like(m_ref)

    row = lax.broadcasted_iota(jnp.int32, (lc, lc), 0)
    col = lax.broadcasted_iota(jnp.int32, (lc, lc), 1)
    tri = ((col - row) * (1 - 2 * rev)) <= 0
    eye = row == col
    ones_blk = (lax.broadcasted_iota(jnp.int32, (lc, LANES), 1) == 0).astype(BF16)
    gl = g_ref[cc]

    for hh in range(heads):
        ig = gl[2 * hh:2 * hh + 1, :]
        lf = _log_sigmoid(gl[2 * hh + 1:2 * hh + 2, :])
        b_col = jnp.sum(jnp.where(tri, jnp.broadcast_to(lf, (lc, lc)), 0.0), axis=1, keepdims=True)
        b_row = jnp.sum(jnp.where(eye, b_col, 0.0), axis=0, keepdims=True)
        g = jnp.sum(lf, axis=1, keepdims=True)
        m_prev = m_ref[hh]

        log_d = jnp.where(tri, b_col - b_row + ig, NEG_BIG)
        m_inter = b_col + m_prev
        m_j = jnp.maximum(m_inter, jnp.max(log_d, axis=1, keepdims=True))
        p = jnp.exp(log_d - m_j)

        q = q_ref[:, hh * dk:(hh + 1) * dk]
        k = k_ref[:, hh * dk:(hh + 1) * dk]
        v_aug = jnp.concatenate([v_ref[:, hh * dv:(hh + 1) * dv], ones_blk], axis=1)
        smat = (_dot_nt(q, k) * p).astype(BF16)
        w_inter = jnp.exp(m_inter - m_j)
        num_aug = (_dot(smat, v_aug) + w_inter * _dot(q, c_ref[hh].astype(BF16))) * scale
        num = num_aug[:, :dv]
        den = num_aug[:, dv:dv + 1]
        o_ref[:, hh * dv:(hh + 1) * dv] = (num / jnp.maximum(jnp.abs(den), jnp.exp(-m_j))).astype(o_ref.dtype)

        a_row = g - b_row + ig
        m_new = jnp.maximum(g + m_prev, jnp.max(a_row, axis=1, keepdims=True))
        wk_row = jnp.exp(a_row - m_new)
        wk_col = jnp.sum(jnp.where(eye, jnp.broadcast_to(wk_row, (lc, lc)), 0.0), axis=1, keepdims=True)
        decay = jnp.exp(g + m_prev - m_new)
        kw = (k.astype(F32) * wk_col).astype(BF16)
        c_ref[hh] = decay * c_ref[hh] + _dot_tn(kw, v_aug)
        m_ref[hh] = m_new


def _mlstm(proj, gates, *, batch, seq, q_off, k_off, v_off, heads, dk, dv):
    lc = SCAN_CHUNK
    nc = seq // lc
    assert dk == lc and seq % lc == 0 and dv % LANES == 0

    def rowblk(s, c):
        return (s // 2) * nc + c + (s % 2) * (nc - 1 - 2 * c)

    wqk, wv = heads * dk, heads * dv
    return pl.pallas_call(
        functools.partial(_mlstm_kernel, nc=nc, heads=heads, dk=dk, dv=dv, scale=dk ** -0.5),
        out_shape=jax.ShapeDtypeStruct((2, batch * seq, wv), F32),
        grid=(batch * 2, nc),
        in_specs=[
            pl.BlockSpec((lc, wqk), lambda s, c: (rowblk(s, c), q_off // wqk)),
            pl.BlockSpec((lc, wqk), lambda s, c: (rowblk(s, c), k_off // wqk)),
            pl.BlockSpec((lc, wv), lambda s, c: (rowblk(s, c), v_off // wv)),
            pl.BlockSpec((None, nc, 2 * heads, lc), lambda s, c: (s, 0, 0, 0)),
        ],
        out_specs=pl.BlockSpec((None, lc, wv), lambda s, c: (s % 2, rowblk(s, c), 0)),
        scratch_shapes=[pltpu.VMEM((heads, dk, dv + LANES), F32), pltpu.VMEM((heads, 1, 1), F32)],
        compiler_params=_params(("parallel", "arbitrary")),
        name="mlstm_scan",
    )(proj, proj, proj, gates)


def _gla_levels():
    lc = SCAN_CHUNK
    nlev = int(math.log2(lc)) + 1
    t = np.arange(lc)
    diff = np.zeros((2, nlev * lc, lc), np.float32)
    mask = np.zeros((2, nlev, lc, lc), np.float32)
    u = t[None, :]
    for d in range(2):
        cum = (u <= t[:, None]) if d == 0 else (u >= t[:, None])
        diff[d, :lc] = cum
        mask[d, 0] = np.eye(lc)
        for l in range(1, nlev):
            w = 1 << (l - 1)
            pair = t // (2 * w)
            second = (t % (2 * w)) >= w
            if d == 0:
                bd = pair * 2 * w + w - 1
                cum_bd = u <= bd[:, None]
                is_q, is_k = second, ~second
            else:
                bd = pair * 2 * w + w
                cum_bd = u >= bd[:, None]
                is_q, is_k = ~second, second
            diff[d, l * lc:(l + 1) * lc] = cum.astype(np.float32) - cum_bd.astype(np.float32)
            mask[d, l] = ((pair[:, None] == pair[None, :]) & is_q[:, None] & is_k[None, :])
    return diff, mask


def _gla_kernel(q_ref, k_ref, v_ref, sm_ref, up_ref, bias_ref, d_ref, mask_ref, o_ref, s_ref, *,
                heads, dk, dv, nlev, scale):
    lc = SCAN_CHUNK
    c = pl.program_id(1)

    @pl.when(c == 0)
    def _():
        s_ref[...] = jnp.zeros_like(s_ref)

    row = lax.broadcasted_iota(jnp.int32, (lc, lc), 0)
    col = lax.broadcasted_iota(jnp.int32, (lc, lc), 1)
    eye = row == col
    sm = sm_ref[...]
    dmat = d_ref[...]

    for hh in range(heads):
        z = _dot3(sm, up_ref[hh]) + bias_ref[hh]
        la = _log_sigmoid(z) * (1.0 / GLA_TAU)
        x = sum(_dot(dmat, piece) for piece in _split_bf16(la, 3))
        b = x[:lc]
        g_row = jnp.sum(la, axis=0, keepdims=True)

        qb = q_ref[:, hh * dk:(hh + 1) * dk]
        kb = k_ref[:, hh * dk:(hh + 1) * dk]
        v = v_ref[:, hh * dv:(hh + 1) * dv]
        q = qb.astype(F32)
        k = kb.astype(F32)
        att = mask_ref[0] * _dot_nt(qb, kb)
        for l in range(1, nlev):
            e = jnp.exp(-jnp.abs(x[l * lc:(l + 1) * lc]))
            att = att + mask_ref[l] * _dot_nt((q * e).astype(BF16), (k * e).astype(BF16))

        o = _dot(att.astype(BF16), v) + _dot((q * jnp.exp(b)).astype(BF16), s_ref[hh].astype(BF16))
        o_ref[:, hh * dv:(hh + 1) * dv] = (o * scale).astype(o_ref.dtype)

        eg_col = jnp.sum(jnp.where(eye, jnp.broadcast_to(jnp.exp(g_row), (lc, lc)), 0.0),
                         axis=1, keepdims=True)
        kd = (k * jnp.exp(g_row - b)).astype(BF16)
        s_ref[hh] = eg_col * s_ref[hh] + _dot_tn(kd, v)


def _gla(proj, small, up_pad, bias, *, batch, seq, q_off, k_off, v_off, heads, dk, dv):
    lc = SCAN_CHUNK
    nc = seq // lc
    assert dk == lc and seq % lc == 0
    diff, mask = _gla_levels()
    nlev = mask.shape[1]

    def rowblk(s, c):
        return (s // 2) * nc + c + (s % 2) * (nc - 1 - 2 * c)

    wqk, wv = heads * dk, heads * dv
    return pl.pallas_call(
        functools.partial(_gla_kernel, heads=heads, dk=dk, dv=dv, nlev=nlev, scale=dk ** -0.5),
        out_shape=jax.ShapeDtypeStruct((2, batch * seq, wv), F32),
        grid=(batch * 2, nc),
        in_specs=[
            pl.BlockSpec((lc, wqk), lambda s, c: (rowblk(s, c), q_off // wqk)),
            pl.BlockSpec((lc, wqk), lambda s, c: (rowblk(s, c), k_off // wqk)),
            pl.BlockSpec((lc, wv), lambda s, c: (rowblk(s, c), v_off // wv)),
            pl.BlockSpec((lc, LANES), lambda s, c: (rowblk(s, c), 0)),
            pl.BlockSpec((None, heads, LANES, dk), lambda s, c: (s % 2, 0, 0, 0)),
            pl.BlockSpec((None, heads, 1, dk), lambda s, c: (s % 2, 0, 0, 0)),
            pl.BlockSpec((None, nlev * lc, lc), lambda s, c: (s % 2, 0, 0)),
            pl.BlockSpec((None, nlev, lc, lc), lambda s, c: (s % 2, 0, 0, 0)),
        ],
        out_specs=pl.BlockSpec((None, lc, wv), lambda s, c: (s % 2, rowblk(s, c), 0)),
        scratch_shapes=[pltpu.VMEM((heads, dk, dv), F32)],
        compiler_params=_params(("parallel", "arbitrary")),
        name="gla_scan",
    )(proj, proj, proj, small, up_pad, bias, jnp.asarray(diff, BF16), jnp.asarray(mask, F32))


def _headnorm_gate_kernel(h_ref, g_ref, hn_ref, o_ref, *, heads, dv, act):
    h = h_ref[0] + h_ref[1]
    outs = []
    for hh in range(heads):
        blk = h[:, hh * dv:(hh + 1) * dv]
        r = lax.rsqrt(jnp.mean(blk * blk, axis=-1, keepdims=True) + EPS)
        outs.append(blk * r)
    hn = jnp.concatenate(outs, axis=1) * hn_ref[...]
    gate = g_ref[...].astype(F32)
    sig = 1.0 / (1.0 + jnp.exp(-gate))
    gate = sig if act == "sigmoid" else gate * sig
    o_ref[...] = (hn * gate).astype(o_ref.dtype)


def _headnorm_gate(h2, proj, gate_off, head_norm, *, heads, dv, act, tm=512):
    _, m, width = h2.shape
    tm = min(tm, m)
    return pl.pallas_call(
        functools.partial(_headnorm_gate_kernel, heads=heads, dv=dv, act=act),
        out_shape=jax.ShapeDtypeStruct((m, width), BF16),
        grid=(m // tm,),
        in_specs=[pl.BlockSpec((2, tm, width), lambda i: (0, i, 0)),
                  pl.BlockSpec((tm, width), lambda i: (i, gate_off // width)),
                  pl.BlockSpec((1, width), lambda i: (0, 0))],
        out_specs=pl.BlockSpec((tm, width), lambda i: (i, 0)),
        compiler_params=_params(("parallel",)),
        name="headnorm_gate",
    )(h2, proj, head_norm.reshape(1, width).astype(F32))


def _cm_short_conv_kernel(x0_ref, x1_ref, x2_ref, w_ref, b_ref, o0_ref, o1_ref, o2_ref, s_ref, t_ref):
    seq, tc = x0_ref.shape
    half, n2 = o0_ref.shape[1], o0_ref.shape[2]
    for p, (x_ref, o_ref) in enumerate(((x0_ref, o0_ref), (x1_ref, o1_ref), (x2_ref, o2_ref))):
        x = x_ref[...].astype(F32)
        w = w_ref[p]
        bias = b_ref[p]
        s_ref[...] = (pltpu.roll(x, 1, axis=0) * w[0:1] + x * w[1:2]
                      + pltpu.roll(x, seq - 1, axis=0) * w[2:3] + bias)
        s_ref[0:1, :] = x[0:1] * w[1:2] + x[1:2] * w[2:3] + bias
        s_ref[seq - 1:seq, :] = x[seq - 2:seq - 1] * w[0:1] + x[seq - 1:seq] * w[1:2] + bias

        def flip(a, carry):
            slab = s_ref[pl.ds(pl.multiple_of(a * n2, n2), n2), :]
            t_ref[pl.ds(pl.multiple_of(a * tc, tc), tc), :] = slab.T
            return carry

        lax.fori_loop(0, half, flip, 0, unroll=min(4, half))

        def body(ch, carry, o_ref=o_ref):
            o_ref[ch] = t_ref[pl.ds(ch, half, stride=tc), :].astype(o_ref.dtype)
            return carry

        lax.fori_loop(0, tc, body, 0, unroll=GATHER_UNROLL)


def _cm_short_conv(proj, conv_w, conv_b, *, batch, seq, off, ch, n2):
    tc = LANES
    nct = ch // tc
    half = seq // n2
    w = conv_w.reshape(3, 3, nct, tc).transpose(1, 2, 0, 3).astype(F32)
    b = conv_b.reshape(3, nct, 1, tc).astype(F32)
    in_specs = [pl.BlockSpec((seq, tc), functools.partial(lambda b_, j, p: (b_, off // tc + p * nct + j), p=p))
                for p in range(3)]
    in_specs += [pl.BlockSpec((3, None, 3, tc), lambda b_, j: (0, j, 0, 0)),
                 pl.BlockSpec((3, None, 1, tc), lambda b_, j: (0, j, 0, 0))]
    out = jax.ShapeDtypeStruct((batch, ch, half, n2), BF16)
    return pl.pallas_call(
        _cm_short_conv_kernel,
        out_shape=[out, out, out],
        grid=(batch, nct),
        in_specs=in_specs,
        out_specs=[pl.BlockSpec((None, tc, half, n2), lambda b_, j: (b_, j, 0, 0))] * 3,
        scratch_shapes=[pltpu.VMEM((seq, tc), F32), pltpu.VMEM((half * tc, n2), F32)],
        compiler_params=_params(("parallel", "parallel")),
        name="hyena_short_conv",
    )(proj, proj, proj, w, b)


def _cm_taps_kernel(z_ref, w1_ref, b1_ref, w2_ref, b2_ref, w3_ref, f_ref, dl_ref, o_ref, t_ref, h_ref, *, emb, n2):
    @pl.when(pl.program_id(1) == 0)
    def _():
        f = f_ref[...]
        h = jnp.sin(f * (_dot3(w1_ref[...], z_ref[...]) + b1_ref[...]))
        h_ref[...] = jnp.sin(f * (_dot3(w2_ref[...], h) + b2_ref[...]))

    win = jnp.exp(-dl_ref[...] * z_ref[0:1, :]) * z_ref[emb:emb + 1, :]
    out = _dot3(w3_ref[...], h_ref[...]) * win
    gc = out.shape[0]
    tile = out.shape[1] // n2
    for s in range(tile):
        t_ref[s * gc:(s + 1) * gc, :] = out[:, s * n2:(s + 1) * n2]

    def body(ch, carry):
        o_ref[ch] = t_ref[pl.ds(ch, tile, stride=gc), :].astype(o_ref.dtype)
        return carry

    lax.fori_loop(0, gc, body, 0, unroll=GATHER_UNROLL)


def _cm_taps(seq, w1, b1, w2, b2, w3, freq, ch, n2, gc=256):
    emb, hid = w1.shape
    n1 = 2 * seq // n2
    half = n1 // 2
    tile = min(BF16_TILE_ROWS, half)
    order = w3.shape[1] // (2 * ch)
    lag = jnp.arange(2 * seq, dtype=jnp.int32)
    pos = jnp.where(lag < seq, lag, 2 * seq - lag)
    n = pos.astype(F32)
    t = n / (seq - 1)
    bands = jnp.linspace(1e-4, HYENA_BANDS - 1, HYENA_BANDS, dtype=F32)
    ang = (2.0 * math.pi * n / seq)[:, None] * bands[None, :]
    valid = (lag != seq).astype(F32)
    z = jnp.concatenate([t[:, None], jnp.cos(ang), -jnp.sin(ang), valid[:, None]], axis=-1)
    zt = jnp.pad(z, ((0, 0), (0, LANES - emb - 1))).T
    deltas = jnp.abs(jnp.linspace(HYENA_MIN_DECAY, HYENA_MAX_DECAY, ch, dtype=F32))
    dl = jnp.tile(deltas, order).reshape(order * ch, 1)
    pad_h = LANES - hid
    w1t = jnp.pad(w1.astype(F32), ((0, LANES - emb), (0, pad_h))).T
    w2t = jnp.pad(w2.astype(F32), ((0, pad_h), (0, pad_h))).T
    w3t = jnp.pad(w3.astype(F32), ((0, pad_h), (0, 0))).T.reshape(order, 2, ch, LANES)
    w3t = w3t.transpose(1, 0, 2, 3).reshape(2, order * ch, LANES)
    colv = lambda v: jnp.pad(v.astype(F32), (0, pad_h)).reshape(LANES, 1)
    gc = min(gc, order * ch)
    nat = n1 // tile
    full = lambda shape: pl.BlockSpec(shape, lambda a, j: (0, 0))
    return pl.pallas_call(
        functools.partial(_cm_taps_kernel, emb=emb, n2=n2),
        out_shape=jax.ShapeDtypeStruct((order * ch, n1, n2), BF16),
        grid=(nat, order * ch // gc),
        in_specs=[pl.BlockSpec((LANES, tile * n2), lambda a, j: (0, a)),
                  full((LANES, LANES)), full((LANES, 1)), full((LANES, LANES)), full((LANES, 1)),
                  pl.BlockSpec((None, gc, LANES), lambda a, j: (a // (nat // 2), j, 0)),
                  full((LANES, 1)),
                  pl.BlockSpec((gc, 1), lambda a, j: (j, 0))],
        out_specs=pl.BlockSpec((gc, tile, n2), lambda a, j: (j, a, 0)),
        scratch_shapes=[pltpu.VMEM((tile * gc, n2), F32), pltpu.VMEM((LANES, tile * n2), F32)],
        compiler_params=_params(("parallel", "arbitrary")),
        name="hyena_filters",
    )(zt, w1t, colv(b1), w2t, colv(b2), w3t, colv(freq), dl)


def _cm_tables(n1, n2):
    a = jnp.arange(n1, dtype=jnp.int32)
    ang1 = ((a[:, None] * a[None, :]) % n1).astype(F32) * (2.0 * math.pi / n1)
    f1r, f1i = jnp.cos(ang1), -jnp.sin(ang1)
    b = jnp.arange(n2, dtype=jnp.int32)
    angt = (a[:, None] * b[None, :]).astype(F32) * (2.0 * math.pi / (n1 * n2))
    tw = jnp.stack([jnp.cos(angt), -jnp.sin(angt)])
    ang2 = ((b[:, None] * b[None, :]) % n2).astype(F32) * (2.0 * math.pi / n2)
    wr, wi = jnp.cos(ang2), -jnp.sin(ang2)
    wb = jnp.concatenate([jnp.concatenate([wr, wi], axis=1),
                          jnp.concatenate([-wi, wr], axis=1)], axis=0).astype(BF16)
    return f1r, f1i, tw, wb, wb.T


def _cm_forward(x_ref, f1, twr, twi, wb, n1):
    ys = []
    for g in range(x_ref.shape[0]):
        p = _dot(f1, x_ref[g])
        pr, pi = p[:n1], p[n1:]
        ys.append(jnp.concatenate([pr * twr - pi * twi, pr * twi + pi * twr], axis=1))
    y = jnp.concatenate(ys, axis=0).astype(BF16)
    return _dot(y, wb)


def _cm_spec_kernel(t_ref, f1_ref, tw_ref, wb_ref, o_ref, *, n1, inv_n):
    z = _cm_forward(t_ref, f1_ref[...], tw_ref[0], tw_ref[1], wb_ref[...], n1)
    o_ref[...] = (z * inv_n).reshape(o_ref.shape)


def _cm_spectrum(taps, f1, tw, wb, *, group=16):
    nch, n1, n2 = taps.shape
    group = min(group, nch)
    full2 = lambda shape: pl.BlockSpec(shape, lambda j: (0,) * len(shape))
    return pl.pallas_call(
        functools.partial(_cm_spec_kernel, n1=n1, inv_n=1.0 / (n1 * n2)),
        out_shape=jax.ShapeDtypeStruct((nch, n1, 2 * n2), F32),
        grid=(nch // group,),
        in_specs=[pl.BlockSpec((group, n1, n2), lambda j: (j, 0, 0)),
                  full2((2 * n1, n1)), full2((2, n1, n2)), full2((2 * n2, 2 * n2))],
        out_specs=pl.BlockSpec((group, n1, 2 * n2), lambda j: (j, 0, 0)),
        compiler_params=_params(("parallel",)),
        name="hyena_filter_spectrum",
    )(taps, f1, tw, wb)


def _cm_conv_kernel(x_ref, gate_ref, kf_ref, skip_ref, f1_ref, f1h_ref, tw_ref, wb_ref, wbt_ref, o_ref, *, n1, n2):
    group = x_ref.shape[0]
    twr, twi = tw_ref[0], tw_ref[1]
    z = _cm_forward(x_ref, f1_ref[...], twr, twi, wb_ref[...], n1)
    kf = kf_ref[...].reshape(group * n1, 2 * n2)
    zr, zi, kr, ki = z[:, :n2], z[:, n2:], kf[:, :n2], kf[:, n2:]
    s = jnp.concatenate([zr * kr - zi * ki, zr * ki + zi * kr], axis=1).astype(BF16)
    v = _dot(s, wbt_ref[...])
    f1h = f1h_ref[...]
    for g in range(group):
        vr, vi = v[g * n1:(g + 1) * n1, :n2], v[g * n1:(g + 1) * n1, n2:]
        u = jnp.concatenate([vr * twr + vi * twi, vi * twr - vr * twi], axis=0).astype(BF16)
        y = _dot(f1h, u)
        y = y + skip_ref[g] * x_ref[g].astype(F32)
        o_ref[g] = (gate_ref[g].astype(F32) * y).astype(o_ref.dtype)


def _cm_conv(x, gate, kf, skip, f1d, f1h, tw, wb, wbt, *, kf_off, out_dtype, group=16):
    batch, ch, half, n2 = x.shape
    n1 = 2 * half
    group = min(group, ch)
    sig = pl.BlockSpec((None, group, half, n2), lambda j, b: (b, j, 0, 0))
    full2 = lambda shape: pl.BlockSpec(shape, lambda j, b: (0,) * len(shape))
    skip_b = jnp.broadcast_to(skip.astype(F32)[:, None, None], (ch, 1, n2))
    return pl.pallas_call(
        functools.partial(_cm_conv_kernel, n1=n1, n2=n2),
        out_shape=jax.ShapeDtypeStruct((batch, ch, half, n2), out_dtype),
        grid=(ch // group, batch),
        in_specs=[sig, sig,
                  pl.BlockSpec((group, n1, 2 * n2), lambda j, b: (kf_off // group + j, 0, 0)),
                  pl.BlockSpec((group, 1, n2), lambda j, b: (j, 0, 0)),
                  full2((2 * n1, half)), full2((half, 2 * n1)), full2((2, n1, n2)),
                  full2((2 * n2, 2 * n2)), full2((2 * n2, 2 * n2))],
        out_specs=sig,
        compiler_params=_params(("parallel", "parallel")),
        name="hyena_fft_conv",
    )(x, gate, kf, skip_b, f1d, f1h, tw, wb, wbt)


def _cm_to_time_major_kernel(y_ref, o_ref, *, half):
    tc = o_ref.shape[1]
    n2 = y_ref.shape[1]

    def body(a, carry):
        slab = y_ref[pl.ds(a, tc, stride=half), :]
        o_ref[pl.ds(pl.multiple_of(a * n2, n2), n2), :] = slab.T.astype(o_ref.dtype)
        return carry

    lax.fori_loop(0, half, body, 0, unroll=min(4, half))


def _cm_to_time_major(y, *, batch, seq, ch):
    half, n2 = y.shape[2], y.shape[3]
    tc = LANES
    return pl.pallas_call(
        functools.partial(_cm_to_time_major_kernel, half=half),
        out_shape=jax.ShapeDtypeStruct((batch * seq, ch), BF16),
        grid=(batch, ch // tc),
        in_specs=[pl.BlockSpec((None, tc * half, n2), lambda b, j: (b, j, 0))],
        out_specs=pl.BlockSpec((seq, tc), lambda b, j: (b, j)),
        compiler_params=_params(("parallel", "parallel")),
        name="hyena_to_time_major",
    )(y.reshape(batch, ch * half, n2))


def _hyena_cm(proj, hy_off, conv_w, conv_b, w1, b1, w2, b2, w3, freq, skip, *, batch, seq, ch):
    n2 = FFT_N2
    n1 = 2 * seq // n2
    half = n1 // 2
    v, x1, x2 = _cm_short_conv(proj, conv_w, conv_b, batch=batch, seq=seq, off=hy_off, ch=ch, n2=n2)
    taps = _cm_taps(seq, w1, b1, w2, b2, w3, freq, ch, n2)
    f1r, f1i, tw, wb, wbt = _cm_tables(n1, n2)
    fstack = jnp.concatenate([f1r, f1i], axis=0)
    kf = _cm_spectrum(taps, fstack.astype(BF16), tw, wb)
    f1d = fstack[:, :half].astype(BF16)
    f1h = jnp.concatenate([f1r[:half], f1i[:half]], axis=1).astype(BF16)
    zcur = v
    for o, gate in enumerate((x1, x2)):
        last = o == HYENA_ORDER - 1
        zcur = _cm_conv(zcur, gate, kf, skip[o], f1d, f1h, tw, wb, wbt, kf_off=o * ch,
                        out_dtype=F32 if last else BF16)
    return _cm_to_time_major(zcur, batch=batch, seq=seq, ch=ch)


def _merge_kernel(ya_ref, yb_ref, yc_ref, ga_ref, gb_ref, gc_ref, w_ref, o_ref):
    acc = None
    for n, (y_ref, g_ref) in enumerate(((ya_ref, ga_ref), (yb_ref, gb_ref), (yc_ref, gc_ref))):
        gate = 1.0 / (1.0 + jnp.exp(-g_ref[...].astype(F32)))
        term = gate * _dot(y_ref[...], w_ref[n].astype(BF16))
        acc = term if acc is None else acc + term
    o_ref[...] = acc.astype(o_ref.dtype)


def _merge(ys, proj, gate_off, w_branch, layer, *, d_model, tm=1024, tn=512):
    m, width = ys[0].shape
    tm, tn = min(tm, m), min(tn, d_model)
    y_spec = pl.BlockSpec((tm, width), lambda i, j: (i, 0))
    g_specs = [pl.BlockSpec((tm, tn), functools.partial(
        lambda i, j, n: (i, (gate_off + n * d_model) // tn + j), n=n)) for n in range(N_BRANCH)]
    return pl.pallas_call(
        _merge_kernel,
        out_shape=jax.ShapeDtypeStruct((m, d_model), BF16),
        grid=(m // tm, d_model // tn),
        in_specs=[y_spec, y_spec, y_spec] + g_specs +
                 [pl.BlockSpec((None, N_BRANCH, width, tn), lambda i, j: (layer, 0, 0, j))],
        out_specs=pl.BlockSpec((tm, tn), lambda i, j: (i, j)),
        compiler_params=_params(("parallel", "parallel")),
        name="branch_merge",
    )(*ys, proj, proj, proj, w_branch)


def _xattn_kernel(q_ref, k_ref, v_ref, o_ref, *, scale):
    s = _dot_nt(q_ref[...], k_ref[...]) * scale
    s = s - jnp.max(s, axis=-1, keepdims=True)
    p = jnp.exp(s)
    l = jnp.sum(p, axis=-1, keepdims=True)
    o_ref[...] = (_dot(p.astype(BF16), v_ref[...]) / l).astype(o_ref.dtype)


def _xattn(q, kv, *, batch, seq, mem, heads, hd, tq=512):
    tq = min(tq, seq)
    nq = seq // tq
    return pl.pallas_call(
        functools.partial(_xattn_kernel, scale=hd ** -0.5),
        out_shape=jax.ShapeDtypeStruct(q.shape, BF16),
        grid=(batch, heads, nq),
        in_specs=[pl.BlockSpec((tq, hd), lambda b, h, i: (b * nq + i, h)),
                  pl.BlockSpec((mem, hd), lambda b, h, i: (b, h)),
                  pl.BlockSpec((mem, hd), lambda b, h, i: (b, heads + h))],
        out_specs=pl.BlockSpec((tq, hd), lambda b, h, i: (b * nq + i, h)),
        compiler_params=_params(("parallel", "parallel", "parallel")),
        name="xattn_core",
    )(q, kv, kv)


def _in_proj_layout(d):
    mix = d // 4
    m_dk = mix // MLSTM_HEADS // 2
    g_dk = mix // GLA_HEADS // 2
    sizes = (MLSTM_HEADS * m_dk, MLSTM_HEADS * m_dk, mix, mix, 4 * MLSTM_HEADS,
             GLA_HEADS * g_dk, GLA_HEADS * g_dk, mix, mix, 2 * GLA_RANK, 3 * mix, N_BRANCH * d)
    offs = np.concatenate([[0], np.cumsum(sizes)])
    wide = (0, 1, 2, 3, 5, 6, 7, 8, 10, 11)
    narrow = (4, 9)
    new_off, pos = {}, 0
    for i in wide:
        new_off[i] = pos
        pos += sizes[i]
    return sizes, offs, wide, narrow, new_off


def kernel(x, mem, norm_gains, final_norm, w_in, b_in, mlstm_head_norm, gla_decay_up, gla_decay_bias, gla_head_norm, hyena_conv_w, hyena_conv_b, hyena_ffn_w1, hyena_ffn_b1, hyena_ffn_w2, hyena_ffn_b2, hyena_ffn_w3, hyena_freq, hyena_skip, w_branch, w_out, xattn_wq, xattn_wkv, xattn_wo, mlp_w1, mlp_w2):
    batch, seq, d = x.shape
    mem_tokens = mem.shape[1]
    depth = norm_gains.shape[0]
    mix = d // 4
    mh, gh = MLSTM_HEADS, GLA_HEADS
    m_dv, g_dv = mix // mh, mix // gh
    m_dk, g_dk = m_dv // 2, g_dv // 2
    nc = seq // SCAN_CHUNK

    sizes, offs, wide, narrow, new_off = _in_proj_layout(d)
    seg = lambda a, i: a[..., offs[i]:offs[i + 1]]
    w_main = jnp.concatenate([seg(w_in, i) for i in wide], axis=2)
    b_main = jnp.concatenate([seg(b_in, i) for i in wide], axis=1)[:, None, :].astype(F32)
    n_small = sum(sizes[i] for i in narrow)
    w_small = jnp.pad(jnp.concatenate([seg(w_in, i) for i in narrow], axis=2),
                      ((0, 0), (0, 0), (0, LANES - n_small)))
    b_small = jnp.pad(jnp.concatenate([seg(b_in, i) for i in narrow], axis=1),
                      ((0, 0), (0, LANES - n_small)))[:, None, :].astype(F32)

    h = x.reshape(batch * seq, d)
    memf = mem.reshape(batch * mem_tokens, d)
    h16, ssq = _stream_stats(h)
    for l in range(depth):
        g = norm_gains[l]
        proj = _mm(h16, w_main, l, norm=(g[0], ssq), bias=b_main, out_dtype=BF16, name="in_proj")
        small = _mm(h16, w_small, l, norm=(g[0], ssq), bias=b_small, out_dtype=F32, name="in_proj_gates")

        mg = small[:, :4 * mh].reshape(batch, nc, SCAN_CHUNK, 2, 2, mh)
        mgates = mg.transpose(0, 3, 1, 5, 4, 2).reshape(batch * 2, nc, 2 * mh, SCAN_CHUNK)
        h_m = _mlstm(proj, mgates, batch=batch, seq=seq, q_off=new_off[0], k_off=new_off[1],
                     v_off=new_off[2], heads=mh, dk=m_dk, dv=m_dv)
        y_a = _headnorm_gate(h_m, proj, new_off[3], mlstm_head_norm[l], heads=mh, dv=m_dv, act="sigmoid")

        up = gla_decay_up[l].reshape(2, GLA_RANK, gh, g_dk).transpose(0, 2, 1, 3).astype(F32)
        up_pad = jnp.zeros((2, gh, LANES, g_dk), F32)
        for dr in range(2):
            r0 = 4 * mh + dr * GLA_RANK
            up_pad = up_pad.at[dr, :, r0:r0 + GLA_RANK, :].set(up[dr])
        gbias = gla_decay_bias[l].reshape(2, gh, 1, g_dk).astype(F32)
        h_g = _gla(proj, small, up_pad, gbias, batch=batch, seq=seq, q_off=new_off[5], k_off=new_off[6],
                   v_off=new_off[7], heads=gh, dk=g_dk, dv=g_dv)
        y_b = _headnorm_gate(h_g, proj, new_off[8], gla_head_norm[l], heads=gh, dv=g_dv, act="silu")

        y_c = _hyena_cm(proj, new_off[10], hyena_conv_w[l], hyena_conv_b[l], hyena_ffn_w1[l], hyena_ffn_b1[l],
                     hyena_ffn_w2[l], hyena_ffn_b2[l], hyena_ffn_w3[l], hyena_freq[l], hyena_skip[l],
                     batch=batch, seq=seq, ch=mix)

        merged = _merge((y_a, y_b, y_c), proj, new_off[11], w_branch, l, d_model=d)
        h, h16, ssq = _mm(merged, w_out, l, resid=h, emit_stats=True, out_dtype=F32, name="mixer_out")

        memn = _rmsnorm(memf, g[2], BF16)
        q = _mm(h16, xattn_wq, l, norm=(g[1], ssq), out_dtype=BF16, name="xattn_q")
        kv = _mm(memn, xattn_wkv, l, out_dtype=BF16, name="xattn_kv")
        o = _xattn(q, kv, batch=batch, seq=seq, mem=mem_tokens, heads=XATTN_HEADS, hd=d // XATTN_HEADS)
        h, h16, ssq = _mm(o, xattn_wo, l, resid=h, emit_stats=True, out_dtype=F32, name="xattn_out")

        hid = _mm(h16, mlp_w1, l, norm=(g[3], ssq), act="relu2", out_dtype=BF16, name="mlp_up")
        if l + 1 < depth:
            h, h16, ssq = _mm(hid, mlp_w2, l, resid=h, emit_stats=True, out_dtype=F32, tk=4096, name="mlp_down")
        else:
            h = _mm(hid, mlp_w2, l, resid=h, out_dtype=F32, tk=4096, name="mlp_down")
    return _rmsnorm(h, final_norm, x.dtype).reshape(batch, seq, d)
```

```python
import functools
import math

import numpy as np
import jax
import jax.numpy as jnp
from jax import lax
from jax.experimental import pallas as pl
from jax.experimental.pallas import tpu as pltpu

F32 = jnp.float32
BF16 = jnp.bfloat16

N_BRANCH = 3
MLSTM_HEADS = 4
GLA_HEADS = 4
GLA_RANK = 16
GLA_TAU = 16.0
HYENA_ORDER = 2
HYENA_BANDS = 16
HYENA_MIN_DECAY = math.log(1e-2) / 1.5
HYENA_MAX_DECAY = math.log(1e-2) / 0.3
XATTN_HEADS = 4
EPS = 1e-6

LANES = 128
BF16_TILE_ROWS = 16
VMEM_LIMIT_BYTES = 56 * 1024 * 1024

SCAN_CHUNK = 128
FFT_N2 = 128
NEG_BIG = -1e30
GATHER_UNROLL = 8


def _params(sem):
    return pltpu.CompilerParams(dimension_semantics=sem, vmem_limit_bytes=VMEM_LIMIT_BYTES)


def _log_sigmoid(x):
    return -(jnp.maximum(-x, 0.0) + jnp.log(1.0 + jnp.exp(-jnp.abs(x))))


def _split_bf16(a, parts):
    out = []
    r = a
    for _ in range(parts):
        p = r.astype(BF16)
        out.append(p)
        r = r - p.astype(F32)
    return out


def _dot(a, b):
    return jnp.dot(a, b, preferred_element_type=F32)


def _dot_nt(a, b):
    return lax.dot_general(a, b, (((1,), (1,)), ((), ())), preferred_element_type=F32)


def _dot_tn(a, b):
    return lax.dot_general(a, b, (((0,), (0,)), ((), ())), preferred_element_type=F32)


def _dot3(a, b):
    a_hi, a_lo = _split_bf16(a, 2)
    b_hi, b_lo = _split_bf16(b, 2)
    return _dot(a_hi, b_hi) + _dot(a_hi, b_lo) + _dot(a_lo, b_hi)


def _rmsnorm_kernel(x_ref, g_ref, o_ref):
    x = x_ref[...].astype(F32)
    r = lax.rsqrt(jnp.mean(x * x, axis=-1, keepdims=True) + EPS)
    o_ref[...] = ((x * r) * g_ref[...]).astype(o_ref.dtype)


def _rmsnorm(x, g, out_dtype, tm=256):
    m, d = x.shape
    tm = min(tm, m)
    return pl.pallas_call(
        _rmsnorm_kernel,
        out_shape=jax.ShapeDtypeStruct((m, d), out_dtype),
        grid=(m // tm,),
        in_specs=[pl.BlockSpec((tm, d), lambda i: (i, 0)),
                  pl.BlockSpec((1, d), lambda i: (0, 0))],
        out_specs=pl.BlockSpec((tm, d), lambda i: (i, 0)),
        compiler_params=_params(("parallel",)),
        name="rmsnorm",
    )(x, g.reshape(1, d).astype(F32))


def _mm_kernel(*refs, nk, act, has_bias, has_resid):
    x_ref, w_ref = refs[0], refs[1]
    idx = 2
    b_ref = r_ref = None
    if has_bias:
        b_ref = refs[idx]
        idx += 1
    if has_resid:
        r_ref = refs[idx]
        idx += 1
    o_ref = refs[idx]

    def epilogue(acc):
        if has_bias:
            acc = acc + b_ref[...]
        if act == "relu2":
            acc = jnp.square(jnp.maximum(acc, 0.0))
        if has_resid:
            acc = acc + r_ref[...]
        o_ref[...] = acc.astype(o_ref.dtype)

    if nk == 1:
        epilogue(_dot(x_ref[...], w_ref[...]))
    else:
        acc_ref = refs[idx + 1]
        k = pl.program_id(2)

        @pl.when(k == 0)
        def _():
            acc_ref[...] = jnp.zeros_like(acc_ref)

        acc_ref[...] += _dot(x_ref[...], w_ref[...])

        @pl.when(k == nk - 1)
        def _():
            epilogue(acc_ref[...])


def _mm(x, w, layer, *, bias=None, resid=None, act=None, out_dtype=BF16, tm=1024, tn=512, tk=None, name="mm"):
    m, kdim = x.shape
    n = w.shape[2]
    tm, tn = min(tm, m), min(tn, n)
    tk = kdim if tk is None else min(tk, kdim)
    nk = kdim // tk
    assert m % tm == 0 and n % tn == 0 and kdim % tk == 0
    in_specs = [pl.BlockSpec((tm, tk), lambda i, j, k: (i, k)),
                pl.BlockSpec((None, tk, tn), lambda i, j, k: (layer, k, j))]
    args = [x, w]
    if bias is not None:
        in_specs.append(pl.BlockSpec((None, 1, tn), lambda i, j, k: (layer, 0, j)))
        args.append(bias)
    if resid is not None:
        in_specs.append(pl.BlockSpec((tm, tn), lambda i, j, k: (i, j)))
        args.append(resid)
    scratch = [pltpu.VMEM((tm, tn), F32)] if nk > 1 else []
    return pl.pallas_call(
        functools.partial(_mm_kernel, nk=nk, act=act, has_bias=bias is not None,
                          has_resid=resid is not None),
        out_shape=jax.ShapeDtypeStruct((m, n), out_dtype),
        grid=(m // tm, n // tn, nk),
        in_specs=in_specs,
        out_specs=pl.BlockSpec((tm, tn), lambda i, j, k: (i, j)),
        scratch_shapes=scratch,
        compiler_params=_params(("parallel", "parallel", "arbitrary")),
        name=name,
    )(*args)


def _rows_to_cols(sel, rows):
    return sum(_dot_nt(sel, piece) for piece in _split_bf16(rows, 3))


def _mlstm_kernel(q_ref, k_ref, v_ref, g_ref, o_ref, c_ref, m_ref, *, nc, heads, dk, dv, scale):
    lc = SCAN_CHUNK
    rev = pl.program_id(0) % 2
    c = pl.program_id(1)
    cc = c + rev * (nc - 1 - 2 * c)

    @pl.when(c == 0)
    def _():
        c_ref[...] = jnp.zeros_like(c_ref)
        m_ref[...] = jnp.zeros_like(m_ref)

    row = lax.broadcasted_iota(jnp.int32, (lc, lc), 0)
    col = lax.broadcasted_iota(jnp.int32, (lc, lc), 1)
    tri = ((col - row) * (1 - 2 * rev)) <= 0
    tri_b = jnp.where(tri, 1.0, 0.0).astype(BF16)
    eye_b = jnp.where(row == col, 1.0, 0.0).astype(BF16)
    ones_blk = (lax.broadcasted_iota(jnp.int32, (lc, LANES), 1) == 0).astype(BF16)

    gl = g_ref[cc]
    lf_all = _log_sigmoid(gl)
    lf_pieces = _split_bf16(lf_all, 3)
    b_rows = sum(_dot_nt(piece, tri_b) for piece in lf_pieces)
    b_cols = sum(_dot_nt(tri_b, piece) for piece in lf_pieces)
    g_all = jnp.sum(lf_all, axis=1, keepdims=True)
    wk_rows, m_news, decays = [], [], []
    for hh in range(heads):
        ig = gl[2 * hh:2 * hh + 1, :]
        g = g_all[2 * hh + 1:2 * hh + 2, :]
        m_prev = m_ref[hh]
        a_row = g - b_rows[2 * hh + 1:2 * hh + 2, :] + ig
        m_new = jnp.maximum(g + m_prev, jnp.max(a_row, axis=1, keepdims=True))
        wk_rows.append(jnp.exp(a_row - m_new))
        m_news.append(m_new)
        decays.append(jnp.exp(g + m_prev - m_new))
    wk_cols = _rows_to_cols(eye_b, jnp.concatenate(wk_rows + [jnp.zeros((heads, lc), F32)], axis=0))

    rng = range(heads)
    qs = [q_ref[:, hh * dk:(hh + 1) * dk] for hh in rng]
    ks = [k_ref[:, hh * dk:(hh + 1) * dk] for hh in rng]
    v_augs = [jnp.concatenate([v_ref[:, hh * dv:(hh + 1) * dv], ones_blk], axis=1) for hh in rng]
    qk = [_dot_nt(qs[hh], ks[hh]) for hh in rng]
    qc = [_dot(qs[hh], c_ref[hh].astype(BF16)) for hh in rng]
    kv = [_dot_tn((ks[hh].astype(F32) * wk_cols[:, hh:hh + 1]).astype(BF16), v_augs[hh]) for hh in rng]
    for hh in rng:
        b_col = b_cols[:, 2 * hh + 1:2 * hh + 2]
        log_d = jnp.where(tri, b_col - b_rows[2 * hh + 1:2 * hh + 2, :] + gl[2 * hh:2 * hh + 1, :], NEG_BIG)
        m_inter = b_col + m_ref[hh]
        m_j = jnp.maximum(m_inter, jnp.max(log_d, axis=1, keepdims=True))
        smat = (qk[hh] * jnp.exp(log_d - m_j)).astype(BF16)
        num_aug = (_dot(smat, v_augs[hh]) + jnp.exp(m_inter - m_j) * qc[hh]) * scale
        num = num_aug[:, :dv]
        den = num_aug[:, dv:dv + 1]
        o_ref[:, hh * dv:(hh + 1) * dv] = (num / jnp.maximum(jnp.abs(den), jnp.exp(-m_j))).astype(o_ref.dtype)
        c_ref[hh] = decays[hh] * c_ref[hh] + kv[hh]
        m_ref[hh] = m_news[hh]


def _mlstm(proj, gates, *, batch, seq, q_off, k_off, v_off, heads, dk, dv):
    lc = SCAN_CHUNK
    nc = seq // lc
    assert dk == lc and seq % lc == 0 and dv % LANES == 0

    def rowblk(s, c):
        return (s // 2) * nc + c + (s % 2) * (nc - 1 - 2 * c)

    wqk, wv = heads * dk, heads * dv
    return pl.pallas_call(
        functools.partial(_mlstm_kernel, nc=nc, heads=heads, dk=dk, dv=dv, scale=dk ** -0.5),
        out_shape=jax.ShapeDtypeStruct((2, batch * seq, wv), F32),
        grid=(batch * 2, nc),
        in_specs=[
            pl.BlockSpec((lc, wqk), lambda s, c: (rowblk(s, c), q_off // wqk)),
            pl.BlockSpec((lc, wqk), lambda s, c: (rowblk(s, c), k_off // wqk)),
            pl.BlockSpec((lc, wv), lambda s, c: (rowblk(s, c), v_off // wv)),
            pl.BlockSpec((None, nc, 2 * heads, lc), lambda s, c: (s, 0, 0, 0)),
        ],
        out_specs=pl.BlockSpec((None, lc, wv), lambda s, c: (s % 2, rowblk(s, c), 0)),
        scratch_shapes=[pltpu.VMEM((heads, dk, dv + LANES), F32), pltpu.VMEM((heads, 1, 1), F32)],
        compiler_params=_params(("parallel", "arbitrary")),
        name="mlstm_scan",
    )(proj, proj, proj, gates)


def _gla_levels():
    lc = SCAN_CHUNK
    nlev = int(math.log2(lc)) + 1
    t = np.arange(lc)
    diff = np.zeros((2, nlev * lc, lc), np.float32)
    mask = np.zeros((2, nlev, lc, lc), np.float32)
    u = t[None, :]
    for d in range(2):
        cum = (u <= t[:, None]) if d == 0 else (u >= t[:, None])
        diff[d, :lc] = cum
        mask[d, 0] = np.eye(lc)
        for l in range(1, nlev):
            w = 1 << (l - 1)
            pair = t // (2 * w)
            second = (t % (2 * w)) >= w
            if d == 0:
                bd = pair * 2 * w + w - 1
                cum_bd = u <= bd[:, None]
                is_q, is_k = second, ~second
            else:
                bd = pair * 2 * w + w
                cum_bd = u >= bd[:, None]
                is_q, is_k = ~second, second
            diff[d, l * lc:(l + 1) * lc] = cum.astype(np.float32) - cum_bd.astype(np.float32)
            mask[d, l] = ((pair[:, None] == pair[None, :]) & is_q[:, None] & is_k[None, :])
    return diff, mask


def _gla_kernel(q_ref, k_ref, v_ref, sm_ref, up_ref, bias_ref, d_ref, mask_ref, o_ref, s_ref, *,
                heads, dk, dv, nlev, scale):
    lc = SCAN_CHUNK
    c = pl.program_id(1)

    @pl.when(c == 0)
    def _():
        s_ref[...] = jnp.zeros_like(s_ref)

    z = _dot3(sm_ref[...], up_ref[...]) + bias_ref[...]
    la = _log_sigmoid(z) * (1.0 / GLA_TAU)
    dmat = d_ref[...]
    x = sum(_dot(dmat, piece) for piece in _split_bf16(la, 3))
    b = x[:lc]
    g_row = jnp.sum(la, axis=0, keepdims=True)

    qb = q_ref[...]
    kb = k_ref[...]
    q = qb.astype(F32)
    k = kb.astype(F32)
    hs = [slice(hh * dk, (hh + 1) * dk) for hh in range(heads)]
    atts = [mask_ref[0] * _dot_nt(qb[:, s], kb[:, s]) for s in hs]
    for l in range(1, nlev):
        e = jnp.exp(-jnp.abs(x[l * lc:(l + 1) * lc]))
        qe = (q * e).astype(BF16)
        ke = (k * e).astype(BF16)
        m = mask_ref[l]
        atts = [att + m * _dot_nt(qe[:, s], ke[:, s]) for att, s in zip(atts, hs)]

    qs = (q * jnp.exp(b)).astype(BF16)
    kd = (k * jnp.exp(g_row - b)).astype(BF16)
    eg = jnp.exp(g_row)
    for hh, s in enumerate(hs):
        v = v_ref[:, hh * dv:(hh + 1) * dv]
        o = _dot(atts[hh].astype(BF16), v) + _dot_nt(qs[:, s], s_ref[hh].astype(BF16))
        o_ref[:, hh * dv:(hh + 1) * dv] = (o * scale).astype(o_ref.dtype)
        s_ref[hh] = eg[:, s] * s_ref[hh] + _dot_tn(v, kd[:, s])


def _gla(proj, small, up_pad, bias, *, batch, seq, q_off, k_off, v_off, heads, dk, dv):
    lc = SCAN_CHUNK
    nc = seq // lc
    assert dk == lc and seq % lc == 0
    diff, mask = _gla_levels()
    nlev = mask.shape[1]

    def rowblk(s, c):
        return (s // 2) * nc + c + (s % 2) * (nc - 1 - 2 * c)

    wqk, wv = heads * dk, heads * dv
    return pl.pallas_call(
        functools.partial(_gla_kernel, heads=heads, dk=dk, dv=dv, nlev=nlev, scale=dk ** -0.5),
        out_shape=jax.ShapeDtypeStruct((2, batch * seq, wv), F32),
        grid=(batch * 2, nc),
        in_specs=[
            pl.BlockSpec((lc, wqk), lambda s, c: (rowblk(s, c), q_off // wqk)),
            pl.BlockSpec((lc, wqk), lambda s, c: (rowblk(s, c), k_off // wqk)),
            pl.BlockSpec((lc, wv), lambda s, c: (rowblk(s, c), v_off // wv)),
            pl.BlockSpec((lc, LANES), lambda s, c: (rowblk(s, c), 0)),
            pl.BlockSpec((None, LANES, wqk), lambda s, c: (s % 2, 0, 0)),
            pl.BlockSpec((None, 1, wqk), lambda s, c: (s % 2, 0, 0)),
            pl.BlockSpec((None, nlev * lc, lc), lambda s, c: (s % 2, 0, 0)),
            pl.BlockSpec((None, nlev, lc, lc), lambda s, c: (s % 2, 0, 0, 0)),
        ],
        out_specs=pl.BlockSpec((None, lc, wv), lambda s, c: (s % 2, rowblk(s, c), 0)),
        scratch_shapes=[pltpu.VMEM((heads, dv, dk), F32)],
        compiler_params=_params(("parallel", "arbitrary")),
        name="gla_scan",
    )(proj, proj, proj, small, up_pad, bias, jnp.asarray(diff, BF16), jnp.asarray(mask, F32))


def _headnorm_gate_kernel(h_ref, g_ref, hn_ref, o_ref, *, heads, dv, act):
    h = h_ref[0] + h_ref[1]
    outs = []
    for hh in range(heads):
        blk = h[:, hh * dv:(hh + 1) * dv]
        r = lax.rsqrt(jnp.mean(blk * blk, axis=-1, keepdims=True) + EPS)
        outs.append(blk * r)
    hn = jnp.concatenate(outs, axis=1) * hn_ref[...]
    gate = g_ref[...].astype(F32)
    sig = 1.0 / (1.0 + jnp.exp(-gate))
    gate = sig if act == "sigmoid" else gate * sig
    o_ref[...] = (hn * gate).astype(o_ref.dtype)


def _headnorm_gate(h2, proj, gate_off, head_norm, *, heads, dv, act, tm=512):
    _, m, width = h2.shape
    tm = min(tm, m)
    return pl.pallas_call(
        functools.partial(_headnorm_gate_kernel, heads=heads, dv=dv, act=act),
        out_shape=jax.ShapeDtypeStruct((m, width), BF16),
        grid=(m // tm,),
        in_specs=[pl.BlockSpec((2, tm, width), lambda i: (0, i, 0)),
                  pl.BlockSpec((tm, width), lambda i: (i, gate_off // width)),
                  pl.BlockSpec((1, width), lambda i: (0, 0))],
        out_specs=pl.BlockSpec((tm, width), lambda i: (i, 0)),
        compiler_params=_params(("parallel",)),
        name="headnorm_gate",
    )(h2, proj, head_norm.reshape(1, width).astype(F32))


def _cm_short_conv_kernel(x0_ref, x1_ref, x2_ref, w_ref, b_ref, o0_ref, o1_ref, o2_ref, s_ref, t_ref):
    seq, tc = x0_ref.shape
    half, n2 = o0_ref.shape[1], o0_ref.shape[2]
    for p, (x_ref, o_ref) in enumerate(((x0_ref, o0_ref), (x1_ref, o1_ref), (x2_ref, o2_ref))):
        x = x_ref[...].astype(F32)
        w = w_ref[p]
        bias = b_ref[p]
        s_ref[...] = (pltpu.roll(x, 1, axis=0) * w[0:1] + x * w[1:2]
                      + pltpu.roll(x, seq - 1, axis=0) * w[2:3] + bias)
        s_ref[0:1, :] = x[0:1] * w[1:2] + x[1:2] * w[2:3] + bias
        s_ref[seq - 1:seq, :] = x[seq - 2:seq - 1] * w[0:1] + x[seq - 1:seq] * w[1:2] + bias

        def flip(a, carry):
            slab = s_ref[pl.ds(pl.multiple_of(a * n2, n2), n2), :]
            t_ref[pl.ds(pl.multiple_of(a * tc, tc), tc), :] = slab.T
            return carry

        lax.fori_loop(0, half, flip, 0, unroll=min(4, half))

        def body(ch, carry, o_ref=o_ref):
            o_ref[ch] = t_ref[pl.ds(ch, half, stride=tc), :].astype(o_ref.dtype)
            return carry

        lax.fori_loop(0, tc, body, 0, unroll=GATHER_UNROLL)


def _cm_short_conv(proj, conv_w, conv_b, *, batch, seq, off, ch, n2):
    tc = LANES
    nct = ch // tc
    half = seq // n2
    w = conv_w.reshape(3, 3, nct, tc).transpose(1, 2, 0, 3).astype(F32)
    b = conv_b.reshape(3, nct, 1, tc).astype(F32)
    in_specs = [pl.BlockSpec((seq, tc), functools.partial(lambda b_, j, p: (b_, off // tc + p * nct + j), p=p))
                for p in range(3)]
    in_specs += [pl.BlockSpec((3, None, 3, tc), lambda b_, j: (0, j, 0, 0)),
                 pl.BlockSpec((3, None, 1, tc), lambda b_, j: (0, j, 0, 0))]
    out = jax.ShapeDtypeStruct((batch, ch, half, n2), BF16)
    return pl.pallas_call(
        _cm_short_conv_kernel,
        out_shape=[out, out, out],
        grid=(batch, nct),
        in_specs=in_specs,
        out_specs=[pl.BlockSpec((None, tc, half, n2), lambda b_, j: (b_, j, 0, 0))] * 3,
        scratch_shapes=[pltpu.VMEM((seq, tc), F32), pltpu.VMEM((half * tc, n2), F32)],
        compiler_params=_params(("parallel", "parallel")),
        name="hyena_short_conv",
    )(proj, proj, proj, w, b)


def _cm_taps_kernel(z_ref, w1_ref, b1_ref, w2_ref, b2_ref, w3_ref, f_ref, dl_ref, o_ref, t_ref, h_ref, *, emb, n2):
    @pl.when(pl.program_id(1) == 0)
    def _():
        f = f_ref[...]
        h = jnp.sin(f * (_dot3(w1_ref[...], z_ref[...]) + b1_ref[...]))
        h_ref[...] = jnp.sin(f * (_dot3(w2_ref[...], h) + b2_ref[...]))

    win = jnp.exp(-dl_ref[...] * z_ref[0:1, :]) * z_ref[emb:emb + 1, :]
    out = _dot3(w3_ref[...], h_ref[...]) * win
    gc = out.shape[0]
    tile = out.shape[1] // n2
    for s in range(tile):
        t_ref[s * gc:(s + 1) * gc, :] = out[:, s * n2:(s + 1) * n2]

    def body(ch, carry):
        o_ref[ch] = t_ref[pl.ds(ch, tile, stride=gc), :].astype(o_ref.dtype)
        return carry

    lax.fori_loop(0, gc, body, 0, unroll=GATHER_UNROLL)


def _cm_taps(seq, w1, b1, w2, b2, w3, freq, ch, n2, gc=256):
    emb, hid = w1.shape
    n1 = 2 * seq // n2
    half = n1 // 2
    tile = min(BF16_TILE_ROWS, half)
    order = w3.shape[1] // (2 * ch)
    lag = jnp.arange(2 * seq, dtype=jnp.int32)
    pos = jnp.where(lag < seq, lag, 2 * seq - lag)
    n = pos.astype(F32)
    t = n / (seq - 1)
    bands = jnp.linspace(1e-4, HYENA_BANDS - 1, HYENA_BANDS, dtype=F32)
    ang = (2.0 * math.pi * n / seq)[:, None] * bands[None, :]
    valid = (lag != seq).astype(F32)
    z = jnp.concatenate([t[:, None], jnp.cos(ang), -jnp.sin(ang), valid[:, None]], axis=-1)
    zt = jnp.pad(z, ((0, 0), (0, LANES - emb - 1))).T
    deltas = jnp.abs(jnp.linspace(HYENA_MIN_DECAY, HYENA_MAX_DECAY, ch, dtype=F32))
    dl = jnp.tile(deltas, order).reshape(order * ch, 1)
    pad_h = LANES - hid
    w1t = jnp.pad(w1.astype(F32), ((0, LANES - emb), (0, pad_h))).T
    w2t = jnp.pad(w2.astype(F32), ((0, pad_h), (0, pad_h))).T
    w3t = jnp.pad(w3.astype(F32), ((0, pad_h), (0, 0))).T.reshape(order, 2, ch, LANES)
    w3t = w3t.transpose(1, 0, 2, 3).reshape(2, order * ch, LANES)
    colv = lambda v: jnp.pad(v.astype(F32), (0, pad_h)).reshape(LANES, 1)
    gc = min(gc, order * ch)
    nat = n1 // tile
    full = lambda shape: pl.BlockSpec(shape, lambda a, j: (0, 0))
    return pl.pallas_call(
        functools.partial(_cm_taps_kernel, emb=emb, n2=n2),
        out_shape=jax.ShapeDtypeStruct((order * ch, n1, n2), BF16),
        grid=(nat, order * ch // gc),
        in_specs=[pl.BlockSpec((LANES, tile * n2), lambda a, j: (0, a)),
                  full((LANES, LANES)), full((LANES, 1)), full((LANES, LANES)), full((LANES, 1)),
                  pl.BlockSpec((None, gc, LANES), lambda a, j: (a // (nat // 2), j, 0)),
                  full((LANES, 1)),
                  pl.BlockSpec((gc, 1), lambda a, j: (j, 0))],
        out_specs=pl.BlockSpec((gc, tile, n2), lambda a, j: (j, a, 0)),
        scratch_shapes=[pltpu.VMEM((tile * gc, n2), F32), pltpu.VMEM((LANES, tile * n2), F32)],
        compiler_params=_params(("parallel", "arbitrary")),
        name="hyena_filters",
    )(zt, w1t, colv(b1), w2t, colv(b2), w3t, colv(freq), dl)


def _cm_tables(n1, n2):
    a = jnp.arange(n1, dtype=jnp.int32)
    ang1 = ((a[:, None] * a[None, :]) % n1).astype(F32) * (2.0 * math.pi / n1)
    f1r, f1i = jnp.cos(ang1), -jnp.sin(ang1)
    b = jnp.arange(n2, dtype=jnp.int32)
    angt = (a[:, None] * b[None, :]).astype(F32) * (2.0 * math.pi / (n1 * n2))
    tw = jnp.stack([jnp.cos(angt), -jnp.sin(angt)])
    ang2 = ((b[:, None] * b[None, :]) % n2).astype(F32) * (2.0 * math.pi / n2)
    wr, wi = jnp.cos(ang2), -jnp.sin(ang2)
    wb = jnp.concatenate([jnp.concatenate([wr, wi], axis=1),
                          jnp.concatenate([-wi, wr], axis=1)], axis=0).astype(BF16)
    return f1r, f1i, tw, wb, wb.T


def _cm_forward(x_ref, f1, twr, twi, wb, n1):
    ys = []
    for g in range(x_ref.shape[0]):
        p = _dot(f1, x_ref[g])
        pr, pi = p[:n1], p[n1:]
        ys.append(jnp.concatenate([pr * twr - pi * twi, pr * twi + pi * twr], axis=1))
    y = jnp.concatenate(ys, axis=0).astype(BF16)
    return _dot(y, wb)


def _cm_spec_kernel(t_ref, f1_ref, tw_ref, wb_ref, o_ref, *, n1, inv_n):
    z = _cm_forward(t_ref, f1_ref[...], tw_ref[0], tw_ref[1], wb_ref[...], n1)
    o_ref[...] = (z * inv_n).reshape(o_ref.shape)


def _cm_spectrum(taps, f1, tw, wb, *, group=16):
    nch, n1, n2 = taps.shape
    group = min(group, nch)
    full2 = lambda shape: pl.BlockSpec(shape, lambda j: (0,) * len(shape))
    return pl.pallas_call(
        functools.partial(_cm_spec_kernel, n1=n1, inv_n=1.0 / (n1 * n2)),
        out_shape=jax.ShapeDtypeStruct((nch, n1, 2 * n2), F32),
        grid=(nch // group,),
        in_specs=[pl.BlockSpec((group, n1, n2), lambda j: (j, 0, 0)),
                  full2((2 * n1, n1)), full2((2, n1, n2)), full2((2 * n2, 2 * n2))],
        out_specs=pl.BlockSpec((group, n1, 2 * n2), lambda j: (j, 0, 0)),
        compiler_params=_params(("parallel",)),
        name="hyena_filter_spectrum",
    )(taps, f1, tw, wb)


def _cm_conv_kernel(x_ref, gate_ref, kf_ref, skip_ref, f1_ref, f1h_ref, tw_ref, wb_ref, wbt_ref, o_ref, *, n1, n2):
    group = x_ref.shape[0]
    twr, twi = tw_ref[0], tw_ref[1]
    z = _cm_forward(x_ref, f1_ref[...], twr, twi, wb_ref[...], n1)
    kf = kf_ref[...].reshape(group * n1, 2 * n2)
    zr, zi, kr, ki = z[:, :n2], z[:, n2:], kf[:, :n2], kf[:, n2:]
    s = jnp.concatenate([zr * kr - zi * ki, zr * ki + zi * kr], axis=1).astype(BF16)
    v = _dot(s, wbt_ref[...])
    f1h = f1h_ref[...]
    for g in range(group):
        vr, vi = v[g * n1:(g + 1) * n1, :n2], v[g * n1:(g + 1) * n1, n2:]
        u = jnp.concatenate([vr * twr + vi * twi, vi * twr - vr * twi], axis=0).astype(BF16)
        y = _dot(f1h, u)
        y = y + skip_ref[g] * x_ref[g].astype(F32)
        o_ref[g] = (gate_ref[g].astype(F32) * y).astype(o_ref.dtype)


def _cm_conv(x, gate, kf, skip, f1d, f1h, tw, wb, wbt, *, kf_off, out_dtype, group=16):
    batch, ch, half, n2 = x.shape
    n1 = 2 * half
    group = min(group, ch)
    sig = pl.BlockSpec((None, group, half, n2), lambda j, b: (b, j, 0, 0))
    full2 = lambda shape: pl.BlockSpec(shape, lambda j, b: (0,) * len(shape))
    skip_b = jnp.broadcast_to(skip.astype(F32)[:, None, None], (ch, 1, n2))
    return pl.pallas_call(
        functools.partial(_cm_conv_kernel, n1=n1, n2=n2),
        out_shape=jax.ShapeDtypeStruct((batch, ch, half, n2), out_dtype),
        grid=(ch // group, batch),
        in_specs=[sig, sig,
                  pl.BlockSpec((group, n1, 2 * n2), lambda j, b: (kf_off // group + j, 0, 0)),
                  pl.BlockSpec((group, 1, n2), lambda j, b: (j, 0, 0)),
                  full2((2 * n1, half)), full2((half, 2 * n1)), full2((2, n1, n2)),
                  full2((2 * n2, 2 * n2)), full2((2 * n2, 2 * n2))],
        out_specs=sig,
        compiler_params=_params(("parallel", "parallel")),
        name="hyena_fft_conv",
    )(x, gate, kf, skip_b, f1d, f1h, tw, wb, wbt)


def _cm_to_time_major_kernel(y_ref, o_ref, *, half):
    tc = o_ref.shape[1]
    n2 = y_ref.shape[1]

    def body(a, carry):
        slab = y_ref[pl.ds(a, tc, stride=half), :]
        o_ref[pl.ds(pl.multiple_of(a * n2, n2), n2), :] = slab.T.astype(o_ref.dtype)
        return carry

    lax.fori_loop(0, half, body, 0, unroll=min(4, half))


def _cm_to_time_major(y, *, batch, seq, ch):
    half, n2 = y.shape[2], y.shape[3]
    tc = LANES
    return pl.pallas_call(
        functools.partial(_cm_to_time_major_kernel, half=half),
        out_shape=jax.ShapeDtypeStruct((batch * seq, ch), BF16),
        grid=(batch, ch // tc),
        in_specs=[pl.BlockSpec((None, tc * half, n2), lambda b, j: (b, j, 0))],
        out_specs=pl.BlockSpec((seq, tc), lambda b, j: (b, j)),
        compiler_params=_params(("parallel", "parallel")),
        name="hyena_to_time_major",
    )(y.reshape(batch, ch * half, n2))


def _hyena_cm(proj, hy_off, conv_w, conv_b, w1, b1, w2, b2, w3, freq, skip, *, batch, seq, ch):
    n2 = FFT_N2
    n1 = 2 * seq // n2
    half = n1 // 2
    v, x1, x2 = _cm_short_conv(proj, conv_w, conv_b, batch=batch, seq=seq, off=hy_off, ch=ch, n2=n2)
    taps = _cm_taps(seq, w1, b1, w2, b2, w3, freq, ch, n2)
    f1r, f1i, tw, wb, wbt = _cm_tables(n1, n2)
    fstack = jnp.concatenate([f1r, f1i], axis=0)
    kf = _cm_spectrum(taps, fstack.astype(BF16), tw, wb)
    f1d = fstack[:, :half].astype(BF16)
    f1h = jnp.concatenate([f1r[:half], f1i[:half]], axis=1).astype(BF16)
    zcur = v
    for o, gate in enumerate((x1, x2)):
        last = o == HYENA_ORDER - 1
        zcur = _cm_conv(zcur, gate, kf, skip[o], f1d, f1h, tw, wb, wbt, kf_off=o * ch,
                        out_dtype=F32 if last else BF16)
    return _cm_to_time_major(zcur, batch=batch, seq=seq, ch=ch)


def _merge_kernel(ya_ref, yb_ref, yc_ref, ga_ref, gb_ref, gc_ref, w_ref, o_ref):
    acc = None
    for n, (y_ref, g_ref) in enumerate(((ya_ref, ga_ref), (yb_ref, gb_ref), (yc_ref, gc_ref))):
        gate = 1.0 / (1.0 + jnp.exp(-g_ref[...].astype(F32)))
        term = gate * _dot(y_ref[...], w_ref[n])
        acc = term if acc is None else acc + term
    o_ref[...] = acc.astype(o_ref.dtype)


def _merge(ys, proj, gate_off, w_branch, layer, *, d_model, tm=1024, tn=512):
    m, width = ys[0].shape
    tm, tn = min(tm, m), min(tn, d_model)
    y_spec = pl.BlockSpec((tm, width), lambda i, j: (i, 0))
    g_specs = [pl.BlockSpec((tm, tn), functools.partial(
        lambda i, j, n: (i, (gate_off + n * d_model) // tn + j), n=n)) for n in range(N_BRANCH)]
    return pl.pallas_call(
        _merge_kernel,
        out_shape=jax.ShapeDtypeStruct((m, d_model), BF16),
        grid=(m // tm, d_model // tn),
        in_specs=[y_spec, y_spec, y_spec] + g_specs +
                 [pl.BlockSpec((None, N_BRANCH, width, tn), lambda i, j: (layer, 0, 0, j))],
        out_specs=pl.BlockSpec((tm, tn), lambda i, j: (i, j)),
        compiler_params=_params(("parallel", "parallel")),
        name="branch_merge",
    )(*ys, proj, proj, proj, w_branch)


def _xattn_kernel(q_ref, k_ref, v_ref, o_ref, *, scale):
    s = _dot_nt(q_ref[...], k_ref[...]) * scale
    s = s - jnp.max(s, axis=-1, keepdims=True)
    p = jnp.exp(s)
    l = jnp.sum(p, axis=-1, keepdims=True)
    o_ref[...] = (_dot(p.astype(BF16), v_ref[...]) / l).astype(o_ref.dtype)


def _xattn(q, kv, *, batch, seq, mem, heads, hd, tq=512):
    tq = min(tq, seq)
    nq = seq // tq
    return pl.pallas_call(
        functools.partial(_xattn_kernel, scale=hd ** -0.5),
        out_shape=jax.ShapeDtypeStruct(q.shape, BF16),
        grid=(batch, heads, nq),
        in_specs=[pl.BlockSpec((tq, hd), lambda b, h, i: (b * nq + i, h)),
                  pl.BlockSpec((mem, hd), lambda b, h, i: (b, h)),
                  pl.BlockSpec((mem, hd), lambda b, h, i: (b, heads + h))],
        out_specs=pl.BlockSpec((tq, hd), lambda b, h, i: (b * nq + i, h)),
        compiler_params=_params(("parallel", "parallel", "parallel")),
        name="xattn_core",
    )(q, kv, kv)


def _in_proj_layout(d):
    mix = d // 4
    m_dk = mix // MLSTM_HEADS // 2
    g_dk = mix // GLA_HEADS // 2
    sizes = (MLSTM_HEADS * m_dk, MLSTM_HEADS * m_dk, mix, mix, 4 * MLSTM_HEADS,
             GLA_HEADS * g_dk, GLA_HEADS * g_dk, mix, mix, 2 * GLA_RANK, 3 * mix, N_BRANCH * d)
    offs = np.concatenate([[0], np.cumsum(sizes)])
    wide = (0, 1, 2, 3, 5, 6, 7, 8, 10, 11)
    narrow = (4, 9)
    new_off, pos = {}, 0
    for i in wide:
        new_off[i] = pos
        pos += sizes[i]
    return sizes, offs, wide, narrow, new_off


def kernel(x, mem, norm_gains, final_norm, w_in, b_in, mlstm_head_norm, gla_decay_up, gla_decay_bias, gla_head_norm, hyena_conv_w, hyena_conv_b, hyena_ffn_w1, hyena_ffn_b1, hyena_ffn_w2, hyena_ffn_b2, hyena_ffn_w3, hyena_freq, hyena_skip, w_branch, w_out, xattn_wq, xattn_wkv, xattn_wo, mlp_w1, mlp_w2):
    batch, seq, d = x.shape
    mem_tokens = mem.shape[1]
    depth = norm_gains.shape[0]
    mix = d // 4
    mh, gh = MLSTM_HEADS, GLA_HEADS
    m_dv, g_dv = mix // mh, mix // gh
    m_dk, g_dk = m_dv // 2, g_dv // 2
    nc = seq // SCAN_CHUNK

    sizes, offs, wide, narrow, new_off = _in_proj_layout(d)
    seg = lambda a, i: a[..., offs[i]:offs[i + 1]]
    w_main = jnp.concatenate([seg(w_in, i) for i in wide], axis=2).astype(BF16)
    b_main = jnp.concatenate([seg(b_in, i) for i in wide], axis=1)[:, None, :].astype(F32)
    n_small = sum(sizes[i] for i in narrow)
    w_small = jnp.pad(jnp.concatenate([seg(w_in, i) for i in narrow], axis=2),
                      ((0, 0), (0, 0), (0, LANES - n_small))).astype(BF16)
    b_small = jnp.pad(jnp.concatenate([seg(b_in, i) for i in narrow], axis=1),
                      ((0, 0), (0, LANES - n_small)))[:, None, :].astype(F32)
    wb16, wo16 = w_branch.astype(BF16), w_out.astype(BF16)
    wq16, wkv16, wxo16 = xattn_wq.astype(BF16), xattn_wkv.astype(BF16), xattn_wo.astype(BF16)
    w1_16, w2_16 = mlp_w1.astype(BF16), mlp_w2.astype(BF16)

    h = x.reshape(batch * seq, d)
    memf = mem.reshape(batch * mem_tokens, d)
    for l in range(depth):
        g = norm_gains[l]
        xn = _rmsnorm(h, g[0], BF16)
        proj = _mm(xn, w_main, l, bias=b_main, out_dtype=BF16, tn=1024, name="in_proj")
        small = _mm(xn, w_small, l, bias=b_small, out_dtype=F32, name="in_proj_gates")

        mg = small[:, :4 * mh].reshape(batch, nc, SCAN_CHUNK, 2, 2, mh)
        mgates = mg.transpose(0, 3, 1, 5, 4, 2).reshape(batch * 2, nc, 2 * mh, SCAN_CHUNK)
        h_m = _mlstm(proj, mgates, batch=batch, seq=seq, q_off=new_off[0], k_off=new_off[1],
                     v_off=new_off[2], heads=mh, dk=m_dk, dv=m_dv)
        y_a = _headnorm_gate(h_m, proj, new_off[3], mlstm_head_norm[l], heads=mh, dv=m_dv, act="sigmoid")

        up_pad = jnp.zeros((2, LANES, gh * g_dk), F32)
        for dr in range(2):
            r0 = 4 * mh + dr * GLA_RANK
            up_pad = up_pad.at[dr, r0:r0 + GLA_RANK, :].set(gla_decay_up[l, dr].astype(F32))
        gbias = gla_decay_bias[l].reshape(2, 1, gh * g_dk).astype(F32)
        h_g = _gla(proj, small, up_pad, gbias, batch=batch, seq=seq, q_off=new_off[5], k_off=new_off[6],
                   v_off=new_off[7], heads=gh, dk=g_dk, dv=g_dv)
        y_b = _headnorm_gate(h_g, proj, new_off[8], gla_head_norm[l], heads=gh, dv=g_dv, act="silu")

        y_c = _hyena_cm(proj, new_off[10], hyena_conv_w[l], hyena_conv_b[l], hyena_ffn_w1[l], hyena_ffn_b1[l],
                        hyena_ffn_w2[l], hyena_ffn_b2[l], hyena_ffn_w3[l], hyena_freq[l], hyena_skip[l],
                        batch=batch, seq=seq, ch=mix)

        merged = _merge((y_a, y_b, y_c), proj, new_off[11], wb16, l, d_model=d)
        h = _mm(merged, wo16, l, resid=h, out_dtype=F32, name="mixer_out")

        xn = _rmsnorm(h, g[1], BF16)
        memn = _rmsnorm(memf, g[2], BF16)
        q = _mm(xn, wq16, l, out_dtype=BF16, tn=1024, name="xattn_q")
        kv = _mm(memn, wkv16, l, out_dtype=BF16, tn=1024, name="xattn_kv")
        o = _xattn(q, kv, batch=batch, seq=seq, mem=mem_tokens, heads=XATTN_HEADS, hd=d // XATTN_HEADS)
        h = _mm(o, wxo16, l, resid=h, out_dtype=F32, name="xattn_out")

        xn = _rmsnorm(h, g[3], BF16)
        hid = _mm(xn, w1_16, l, act="relu2", out_dtype=BF16, tn=1024, name="mlp_up")
        h = _mm(hid, w2_16, l, resid=h, out_dtype=F32, tm=512, tn=256, name="mlp_down")
    return _rmsnorm(h, final_norm, x.dtype).reshape(batch, seq, d)
```

```python
import functools
import math

import numpy as np
import jax
import jax.numpy as jnp
from jax import lax
from jax.experimental import pallas as pl
from jax.experimental.pallas import tpu as pltpu

F32 = jnp.float32
BF16 = jnp.bfloat16

N_BRANCH = 3
MLSTM_HEADS = 4
GLA_HEADS = 4
GLA_RANK = 16
GLA_TAU = 16.0
HYENA_ORDER = 2
HYENA_BANDS = 16
HYENA_MIN_DECAY = math.log(1e-2) / 1.5
HYENA_MAX_DECAY = math.log(1e-2) / 0.3
XATTN_HEADS = 4
EPS = 1e-6

LANES = 128
BF16_TILE_ROWS = 16
VMEM_LIMIT_BYTES = 56 * 1024 * 1024

SCAN_CHUNK = 128
FFT_N2 = 128
NEG_BIG = -1e30
GATHER_UNROLL = 8
EXACT_PIECES = 2


def _params(sem):
    return pltpu.CompilerParams(dimension_semantics=sem, vmem_limit_bytes=VMEM_LIMIT_BYTES)


def _log_sigmoid(x):
    return -(jnp.maximum(-x, 0.0) + jnp.log(1.0 + jnp.exp(-jnp.abs(x))))


def _split_bf16(a, parts):
    out = []
    r = a
    for _ in range(parts):
        p = r.astype(BF16)
        out.append(p)
        r = r - p.astype(F32)
    return out


def _dot(a, b):
    return jnp.dot(a, b, preferred_element_type=F32)


def _dot_nt(a, b):
    return lax.dot_general(a, b, (((1,), (1,)), ((), ())), preferred_element_type=F32)


def _dot_tn(a, b):
    return lax.dot_general(a, b, (((0,), (0,)), ((), ())), preferred_element_type=F32)


def _dot3(a, b):
    a_hi, a_lo = _split_bf16(a, 2)
    b_hi, b_lo = _split_bf16(b, 2)
    return _dot(a_hi, b_hi) + _dot(a_hi, b_lo) + _dot(a_lo, b_hi)


def _rmsnorm_kernel(x_ref, g_ref, o_ref):
    x = x_ref[...].astype(F32)
    r = lax.rsqrt(jnp.mean(x * x, axis=-1, keepdims=True) + EPS)
    o_ref[...] = ((x * r) * g_ref[...]).astype(o_ref.dtype)


def _rmsnorm(x, g, out_dtype, tm=256):
    m, d = x.shape
    tm = min(tm, m)
    return pl.pallas_call(
        _rmsnorm_kernel,
        out_shape=jax.ShapeDtypeStruct((m, d), out_dtype),
        grid=(m // tm,),
        in_specs=[pl.BlockSpec((tm, d), lambda i: (i, 0)),
                  pl.BlockSpec((1, d), lambda i: (0, 0))],
        out_specs=pl.BlockSpec((tm, d), lambda i: (i, 0)),
        compiler_params=_params(("parallel",)),
        name="rmsnorm",
    )(x, g.reshape(1, d).astype(F32))


def _mm_kernel(*refs, nk, act, has_bias, has_resid):
    x_ref, w_ref = refs[0], refs[1]
    idx = 2
    b_ref = r_ref = None
    if has_bias:
        b_ref = refs[idx]
        idx += 1
    if has_resid:
        r_ref = refs[idx]
        idx += 1
    o_ref = refs[idx]

    def epilogue(acc):
        if has_bias:
            acc = acc + b_ref[...]
        if act == "relu2":
            acc = jnp.square(jnp.maximum(acc, 0.0))
        if has_resid:
            acc = acc + r_ref[...]
        o_ref[...] = acc.astype(o_ref.dtype)

    if nk == 1:
        epilogue(_dot(x_ref[...], w_ref[...]))
    else:
        acc_ref = refs[idx + 1]
        k = pl.program_id(2)

        @pl.when(k == 0)
        def _():
            acc_ref[...] = jnp.zeros_like(acc_ref)

        acc_ref[...] += _dot(x_ref[...], w_ref[...])

        @pl.when(k == nk - 1)
        def _():
            epilogue(acc_ref[...])


def _mm(x, w, layer, *, bias=None, resid=None, act=None, out_dtype=BF16, tm=1024, tn=512, tk=None, name="mm"):
    m, kdim = x.shape
    n = w.shape[2]
    tm, tn = min(tm, m), min(tn, n)
    tk = kdim if tk is None else min(tk, kdim)
    nk = kdim // tk
    assert m % tm == 0 and n % tn == 0 and kdim % tk == 0
    in_specs = [pl.BlockSpec((tm, tk), lambda i, j, k: (i, k)),
                pl.BlockSpec((None, tk, tn), lambda i, j, k: (layer, k, j))]
    args = [x, w]
    if bias is not None:
        in_specs.append(pl.BlockSpec((None, 1, tn), lambda i, j, k: (layer, 0, j)))
        args.append(bias)
    if resid is not None:
        in_specs.append(pl.BlockSpec((tm, tn), lambda i, j, k: (i, j)))
        args.append(resid)
    scratch = [pltpu.VMEM((tm, tn), F32)] if nk > 1 else []
    return pl.pallas_call(
        functools.partial(_mm_kernel, nk=nk, act=act, has_bias=bias is not None,
                          has_resid=resid is not None),
        out_shape=jax.ShapeDtypeStruct((m, n), out_dtype),
        grid=(m // tm, n // tn, nk),
        in_specs=in_specs,
        out_specs=pl.BlockSpec((tm, tn), lambda i, j, k: (i, j)),
        scratch_shapes=scratch,
        compiler_params=_params(("parallel", "parallel", "arbitrary")),
        name=name,
    )(*args)


def _rows_to_cols(sel, rows):
    return sum(_dot_nt(sel, piece) for piece in _split_bf16(rows, EXACT_PIECES))


def _mlstm_kernel(q_ref, k_ref, v_ref, g_ref, o_ref, c_ref, m_ref, *, nc, heads, dk, dv, scale):
    lc = SCAN_CHUNK
    rev = pl.program_id(0) % 2
    c = pl.program_id(1)
    cc = c + rev * (nc - 1 - 2 * c)

    @pl.when(c == 0)
    def _():
        c_ref[...] = jnp.zeros_like(c_ref)
        m_ref[...] = jnp.zeros_like(m_ref)

    row = lax.broadcasted_iota(jnp.int32, (lc, lc), 0)
    col = lax.broadcasted_iota(jnp.int32, (lc, lc), 1)
    tri = ((col - row) * (1 - 2 * rev)) <= 0
    tri_b = jnp.where(tri, 1.0, 0.0).astype(BF16)
    eye_b = jnp.where(row == col, 1.0, 0.0).astype(BF16)
    ones_blk = (lax.broadcasted_iota(jnp.int32, (lc, LANES), 1) == 0).astype(BF16)

    gl = g_ref[cc]
    lf_all = _log_sigmoid(gl)
    lf_pieces = _split_bf16(lf_all, EXACT_PIECES)
    b_rows = sum(_dot_nt(piece, tri_b) for piece in lf_pieces)
    b_cols = sum(_dot_nt(tri_b, piece) for piece in lf_pieces)
    g_all = jnp.sum(lf_all, axis=1, keepdims=True)
    wk_rows, m_news, decays = [], [], []
    for hh in range(heads):
        ig = gl[2 * hh:2 * hh + 1, :]
        g = g_all[2 * hh + 1:2 * hh + 2, :]
        m_prev = m_ref[hh]
        a_row = g - b_rows[2 * hh + 1:2 * hh + 2, :] + ig
        m_new = jnp.maximum(g + m_prev, jnp.max(a_row, axis=1, keepdims=True))
        wk_rows.append(jnp.exp(a_row - m_new))
        m_news.append(m_new)
        decays.append(jnp.exp(g + m_prev - m_new))
    wk_cols = _rows_to_cols(eye_b, jnp.concatenate(wk_rows + [jnp.zeros((heads, lc), F32)], axis=0))

    rng = range(heads)
    qs = [q_ref[:, hh * dk:(hh + 1) * dk] for hh in rng]
    ks = [k_ref[:, hh * dk:(hh + 1) * dk] for hh in rng]
    v_augs = [jnp.concatenate([v_ref[:, hh * dv:(hh + 1) * dv], ones_blk], axis=1) for hh in rng]
    qk = [_dot_nt(qs[hh], ks[hh]) for hh in rng]
    qc = [_dot(qs[hh], c_ref[hh].astype(BF16)) for hh in rng]
    kv = [_dot_tn((ks[hh].astype(F32) * wk_cols[:, hh:hh + 1]).astype(BF16), v_augs[hh]) for hh in rng]
    for hh in rng:
        b_col = b_cols[:, 2 * hh + 1:2 * hh + 2]
        log_d = jnp.where(tri, b_col - b_rows[2 * hh + 1:2 * hh + 2, :] + gl[2 * hh:2 * hh + 1, :], NEG_BIG)
        m_inter = b_col + m_ref[hh]
        m_j = jnp.maximum(m_inter, jnp.max(log_d, axis=1, keepdims=True))
        smat = (qk[hh] * jnp.exp(log_d - m_j)).astype(BF16)
        num_aug = (_dot(smat, v_augs[hh]) + jnp.exp(m_inter - m_j) * qc[hh]) * scale
        num = num_aug[:, :dv]
        den = num_aug[:, dv:dv + 1]
        o_ref[:, hh * dv:(hh + 1) * dv] = (num / jnp.maximum(jnp.abs(den), jnp.exp(-m_j))).astype(o_ref.dtype)
        c_ref[hh] = decays[hh] * c_ref[hh] + kv[hh]
        m_ref[hh] = m_news[hh]


def _mlstm(proj, gates, *, batch, seq, q_off, k_off, v_off, heads, dk, dv):
    lc = SCAN_CHUNK
    nc = seq // lc
    assert dk == lc and seq % lc == 0 and dv % LANES == 0

    def rowblk(s, c):
        return (s // 2) * nc + c + (s % 2) * (nc - 1 - 2 * c)

    wqk, wv = heads * dk, heads * dv
    return pl.pallas_call(
        functools.partial(_mlstm_kernel, nc=nc, heads=heads, dk=dk, dv=dv, scale=dk ** -0.5),
        out_shape=jax.ShapeDtypeStruct((2, batch * seq, wv), F32),
        grid=(batch * 2, nc),
        in_specs=[
            pl.BlockSpec((lc, wqk), lambda s, c: (rowblk(s, c), q_off // wqk)),
            pl.BlockSpec((lc, wqk), lambda s, c: (rowblk(s, c), k_off // wqk)),
            pl.BlockSpec((lc, wv), lambda s, c: (rowblk(s, c), v_off // wv)),
            pl.BlockSpec((None, nc, 2 * heads, lc), lambda s, c: (s, 0, 0, 0)),
        ],
        out_specs=pl.BlockSpec((None, lc, wv), lambda s, c: (s % 2, rowblk(s, c), 0)),
        scratch_shapes=[pltpu.VMEM((heads, dk, dv + LANES), F32), pltpu.VMEM((heads, 1, 1), F32)],
        compiler_params=_params(("parallel", "arbitrary")),
        name="mlstm_scan",
    )(proj, proj, proj, gates)


def _gla_levels():
    lc = SCAN_CHUNK
    nlev = int(math.log2(lc)) + 1
    t = np.arange(lc)
    diff = np.zeros((2, nlev * lc, lc), np.float32)
    mask = np.zeros((2, nlev, lc, lc), np.float32)
    u = t[None, :]
    for d in range(2):
        cum = (u <= t[:, None]) if d == 0 else (u >= t[:, None])
        diff[d, :lc] = cum
        mask[d, 0] = np.eye(lc)
        for l in range(1, nlev):
            w = 1 << (l - 1)
            pair = t // (2 * w)
            second = (t % (2 * w)) >= w
            if d == 0:
                bd = pair * 2 * w + w - 1
                cum_bd = u <= bd[:, None]
                is_q, is_k = second, ~second
            else:
                bd = pair * 2 * w + w
                cum_bd = u >= bd[:, None]
                is_q, is_k = ~second, second
            diff[d, l * lc:(l + 1) * lc] = cum.astype(np.float32) - cum_bd.astype(np.float32)
            mask[d, l] = ((pair[:, None] == pair[None, :]) & is_q[:, None] & is_k[None, :])
    return diff, mask


def _gla_kernel(q_ref, k_ref, v_ref, sm_ref, up_ref, bias_ref, d_ref, mask_ref, o_ref, s_ref, *,
                heads, dk, dv, nlev, scale):
    lc = SCAN_CHUNK
    c = pl.program_id(1)

    @pl.when(c == 0)
    def _():
        s_ref[...] = jnp.zeros_like(s_ref)

    z = _dot3(sm_ref[...], up_ref[...]) + bias_ref[...]
    la = _log_sigmoid(z) * (1.0 / GLA_TAU)
    dmat = d_ref[...]
    x = sum(_dot(dmat, piece) for piece in _split_bf16(la, EXACT_PIECES))
    b = x[:lc]
    g_row = jnp.sum(la, axis=0, keepdims=True)

    qb = q_ref[...]
    kb = k_ref[...]
    q = qb.astype(F32)
    k = kb.astype(F32)
    hs = [slice(hh * dk, (hh + 1) * dk) for hh in range(heads)]
    atts = [mask_ref[0] * _dot_nt(qb[:, s], kb[:, s]) for s in hs]
    for l in range(1, nlev):
        e = jnp.exp(-jnp.abs(x[l * lc:(l + 1) * lc]))
        qe = (q * e).astype(BF16)
        ke = (k * e).astype(BF16)
        m = mask_ref[l]
        atts = [att + m * _dot_nt(qe[:, s], ke[:, s]) for att, s in zip(atts, hs)]

    qs = (q * jnp.exp(b)).astype(BF16)
    kd = (k * jnp.exp(g_row - b)).astype(BF16)
    eg = jnp.exp(g_row)
    for hh, s in enumerate(hs):
        v = v_ref[:, hh * dv:(hh + 1) * dv]
        o = _dot(atts[hh].astype(BF16), v) + _dot_nt(qs[:, s], s_ref[hh].astype(BF16))
        o_ref[:, hh * dv:(hh + 1) * dv] = (o * scale).astype(o_ref.dtype)
        s_ref[hh] = eg[:, s] * s_ref[hh] + _dot_tn(v, kd[:, s])


def _gla(proj, small, up_pad, bias, *, batch, seq, q_off, k_off, v_off, heads, dk, dv):
    lc = SCAN_CHUNK
    nc = seq // lc
    assert dk == lc and seq % lc == 0
    diff, mask = _gla_levels()
    nlev = mask.shape[1]

    def rowblk(s, c):
        return (s // 2) * nc + c + (s % 2) * (nc - 1 - 2 * c)

    wqk, wv = heads * dk, heads * dv
    return pl.pallas_call(
        functools.partial(_gla_kernel, heads=heads, dk=dk, dv=dv, nlev=nlev, scale=dk ** -0.5),
        out_shape=jax.ShapeDtypeStruct((2, batch * seq, wv), F32),
        grid=(batch * 2, nc),
        in_specs=[
            pl.BlockSpec((lc, wqk), lambda s, c: (rowblk(s, c), q_off // wqk)),
            pl.BlockSpec((lc, wqk), lambda s, c: (rowblk(s, c), k_off // wqk)),
            pl.BlockSpec((lc, wv), lambda s, c: (rowblk(s, c), v_off // wv)),
            pl.BlockSpec((lc, LANES), lambda s, c: (rowblk(s, c), 0)),
            pl.BlockSpec((None, LANES, wqk), lambda s, c: (s % 2, 0, 0)),
            pl.BlockSpec((None, 1, wqk), lambda s, c: (s % 2, 0, 0)),
            pl.BlockSpec((None, nlev * lc, lc), lambda s, c: (s % 2, 0, 0)),
            pl.BlockSpec((None, nlev, lc, lc), lambda s, c: (s % 2, 0, 0, 0)),
        ],
        out_specs=pl.BlockSpec((None, lc, wv), lambda s, c: (s % 2, rowblk(s, c), 0)),
        scratch_shapes=[pltpu.VMEM((heads, dv, dk), F32)],
        compiler_params=_params(("parallel", "arbitrary")),
        name="gla_scan",
    )(proj, proj, proj, small, up_pad, bias, jnp.asarray(diff, BF16), jnp.asarray(mask, F32))


def _headnorm_gate_kernel(h_ref, g_ref, hn_ref, o_ref, *, heads, dv, act):
    h = h_ref[0] + h_ref[1]
    outs = []
    for hh in range(heads):
        blk = h[:, hh * dv:(hh + 1) * dv]
        r = lax.rsqrt(jnp.mean(blk * blk, axis=-1, keepdims=True) + EPS)
        outs.append(blk * r)
    hn = jnp.concatenate(outs, axis=1) * hn_ref[...]
    gate = g_ref[...].astype(F32)
    sig = 1.0 / (1.0 + jnp.exp(-gate))
    gate = sig if act == "sigmoid" else gate * sig
    o_ref[...] = (hn * gate).astype(o_ref.dtype)


def _headnorm_gate(h2, proj, gate_off, head_norm, *, heads, dv, act, tm=512):
    _, m, width = h2.shape
    tm = min(tm, m)
    return pl.pallas_call(
        functools.partial(_headnorm_gate_kernel, heads=heads, dv=dv, act=act),
        out_shape=jax.ShapeDtypeStruct((m, width), BF16),
        grid=(m // tm,),
        in_specs=[pl.BlockSpec((2, tm, width), lambda i: (0, i, 0)),
                  pl.BlockSpec((tm, width), lambda i: (i, gate_off // width)),
                  pl.BlockSpec((1, width), lambda i: (0, 0))],
        out_specs=pl.BlockSpec((tm, width), lambda i: (i, 0)),
        compiler_params=_params(("parallel",)),
        name="headnorm_gate",
    )(h2, proj, head_norm.reshape(1, width).astype(F32))


def _cm_short_conv_kernel(x0_ref, x1_ref, x2_ref, w_ref, b_ref, o0_ref, o1_ref, o2_ref, s_ref, t_ref):
    seq, tc = x0_ref.shape
    half, n2 = o0_ref.shape[1], o0_ref.shape[2]
    for p, (x_ref, o_ref) in enumerate(((x0_ref, o0_ref), (x1_ref, o1_ref), (x2_ref, o2_ref))):
        x = x_ref[...].astype(F32)
        w = w_ref[p]
        bias = b_ref[p]
        s_ref[...] = (pltpu.roll(x, 1, axis=0) * w[0:1] + x * w[1:2]
                      + pltpu.roll(x, seq - 1, axis=0) * w[2:3] + bias)
        s_ref[0:1, :] = x[0:1] * w[1:2] + x[1:2] * w[2:3] + bias
        s_ref[seq - 1:seq, :] = x[seq - 2:seq - 1] * w[0:1] + x[seq - 1:seq] * w[1:2] + bias

        def flip(a, carry):
            slab = s_ref[pl.ds(pl.multiple_of(a * n2, n2), n2), :]
            t_ref[pl.ds(pl.multiple_of(a * tc, tc), tc), :] = slab.T
            return carry

        lax.fori_loop(0, half, flip, 0, unroll=min(4, half))

        def body(ch, carry, o_ref=o_ref):
            o_ref[ch] = t_ref[pl.ds(ch, half, stride=tc), :].astype(o_ref.dtype)
            return carry

        lax.fori_loop(0, tc, body, 0, unroll=GATHER_UNROLL)


def _cm_short_conv(proj, conv_w, conv_b, *, batch, seq, off, ch, n2):
    tc = LANES
    nct = ch // tc
    half = seq // n2
    w = conv_w.reshape(3, 3, nct, tc).transpose(1, 2, 0, 3).astype(F32)
    b = conv_b.reshape(3, nct, 1, tc).astype(F32)
    in_specs = [pl.BlockSpec((seq, tc), functools.partial(lambda b_, j, p: (b_, off // tc + p * nct + j), p=p))
                for p in range(3)]
    in_specs += [pl.BlockSpec((3, None, 3, tc), lambda b_, j: (0, j, 0, 0)),
                 pl.BlockSpec((3, None, 1, tc), lambda b_, j: (0, j, 0, 0))]
    out = jax.ShapeDtypeStruct((batch, ch, half, n2), BF16)
    return pl.pallas_call(
        _cm_short_conv_kernel,
        out_shape=[out, out, out],
        grid=(batch, nct),
        in_specs=in_specs,
        out_specs=[pl.BlockSpec((None, tc, half, n2), lambda b_, j: (b_, j, 0, 0))] * 3,
        scratch_shapes=[pltpu.VMEM((seq, tc), F32), pltpu.VMEM((half * tc, n2), F32)],
        compiler_params=_params(("parallel", "parallel")),
        name="hyena_short_conv",
    )(proj, proj, proj, w, b)


def _cm_taps_kernel(z_ref, w1_ref, b1_ref, w2_ref, b2_ref, w3_ref, f_ref, dl_ref, o_ref, t_ref, h_ref, *, emb, n2):
    @pl.when(pl.program_id(1) == 0)
    def _():
        f = f_ref[...]
        h = jnp.sin(f * (_dot3(w1_ref[...], z_ref[...]) + b1_ref[...]))
        h_ref[...] = jnp.sin(f * (_dot3(w2_ref[...], h) + b2_ref[...]))

    win = jnp.exp(-dl_ref[...] * z_ref[0:1, :]) * z_ref[emb:emb + 1, :]
    out = _dot3(w3_ref[...], h_ref[...]) * win
    gc = out.shape[0]
    tile = out.shape[1] // n2
    for s in range(tile):
        t_ref[s * gc:(s + 1) * gc, :] = out[:, s * n2:(s + 1) * n2]

    def body(ch, carry):
        o_ref[ch] = t_ref[pl.ds(ch, tile, stride=gc), :].astype(o_ref.dtype)
        return carry

    lax.fori_loop(0, gc, body, 0, unroll=GATHER_UNROLL)


def _cm_taps(seq, w1, b1, w2, b2, w3, freq, ch, n2, gc=256):
    emb, hid = w1.shape
    n1 = 2 * seq // n2
    half = n1 // 2
    tile = min(BF16_TILE_ROWS, half)
    order = w3.shape[1] // (2 * ch)
    lag = jnp.arange(2 * seq, dtype=jnp.int32)
    pos = jnp.where(lag < seq, lag, 2 * seq - lag)
    n = pos.astype(F32)
    t = n / (seq - 1)
    bands = jnp.linspace(1e-4, HYENA_BANDS - 1, HYENA_BANDS, dtype=F32)
    ang = (2.0 * math.pi * n / seq)[:, None] * bands[None, :]
    valid = (lag != seq).astype(F32)
    z = jnp.concatenate([t[:, None], jnp.cos(ang), -jnp.sin(ang), valid[:, None]], axis=-1)
    zt = jnp.pad(z, ((0, 0), (0, LANES - emb - 1))).T
    deltas = jnp.abs(jnp.linspace(HYENA_MIN_DECAY, HYENA_MAX_DECAY, ch, dtype=F32))
    dl = jnp.tile(deltas, order).reshape(order * ch, 1)
    pad_h = LANES - hid
    w1t = jnp.pad(w1.astype(F32), ((0, LANES - emb), (0, pad_h))).T
    w2t = jnp.pad(w2.astype(F32), ((0, pad_h), (0, pad_h))).T
    w3t = jnp.pad(w3.astype(F32), ((0, pad_h), (0, 0))).T.reshape(order, 2, ch, LANES)
    w3t = w3t.transpose(1, 0, 2, 3).reshape(2, order * ch, LANES)
    colv = lambda v: jnp.pad(v.astype(F32), (0, pad_h)).reshape(LANES, 1)
    gc = min(gc, order * ch)
    nat = n1 // tile
    full = lambda shape: pl.BlockSpec(shape, lambda a, j: (0, 0))
    return pl.pallas_call(
        functools.partial(_cm_taps_kernel, emb=emb, n2=n2),
        out_shape=jax.ShapeDtypeStruct((order * ch, n1, n2), BF16),
        grid=(nat, order * ch // gc),
        in_specs=[pl.BlockSpec((LANES, tile * n2), lambda a, j: (0, a)),
                  full((LANES, LANES)), full((LANES, 1)), full((LANES, LANES)), full((LANES, 1)),
                  pl.BlockSpec((None, gc, LANES), lambda a, j: (a // (nat // 2), j, 0)),
                  full((LANES, 1)),
                  pl.BlockSpec((gc, 1), lambda a, j: (j, 0))],
        out_specs=pl.BlockSpec((gc, tile, n2), lambda a, j: (j, a, 0)),
        scratch_shapes=[pltpu.VMEM((tile * gc, n2), F32), pltpu.VMEM((LANES, tile * n2), F32)],
        compiler_params=_params(("parallel", "arbitrary")),
        name="hyena_filters",
    )(zt, w1t, colv(b1), w2t, colv(b2), w3t, colv(freq), dl)


def _cm_tables(n1, n2):
    a = jnp.arange(n1, dtype=jnp.int32)
    ang1 = ((a[:, None] * a[None, :]) % n1).astype(F32) * (2.0 * math.pi / n1)
    f1r, f1i = jnp.cos(ang1), -jnp.sin(ang1)
    b = jnp.arange(n2, dtype=jnp.int32)
    angt = (a[:, None] * b[None, :]).astype(F32) * (2.0 * math.pi / (n1 * n2))
    tw = jnp.stack([jnp.cos(angt), -jnp.sin(angt)])
    ang2 = ((b[:, None] * b[None, :]) % n2).astype(F32) * (2.0 * math.pi / n2)
    wr, wi = jnp.cos(ang2), -jnp.sin(ang2)
    wb = jnp.concatenate([jnp.concatenate([wr, wi], axis=1),
                          jnp.concatenate([-wi, wr], axis=1)], axis=0).astype(BF16)
    return f1r, f1i, tw, wb, wb.T


def _cm_forward(x_ref, f1, twr, twi, wb, n1):
    ys = []
    for g in range(x_ref.shape[0]):
        p = _dot(f1, x_ref[g])
        pr, pi = p[:n1], p[n1:]
        ys.append(jnp.concatenate([pr * twr - pi * twi, pr * twi + pi * twr], axis=1))
    y = jnp.concatenate(ys, axis=0).astype(BF16)
    return _dot(y, wb)


def _cm_spec_kernel(t_ref, f1_ref, tw_ref, wb_ref, o_ref, *, n1, inv_n):
    z = _cm_forward(t_ref, f1_ref[...], tw_ref[0], tw_ref[1], wb_ref[...], n1)
    o_ref[...] = (z * inv_n).reshape(o_ref.shape)


def _cm_spectrum(taps, f1, tw, wb, *, group=16):
    nch, n1, n2 = taps.shape
    group = min(group, nch)
    full2 = lambda shape: pl.BlockSpec(shape, lambda j: (0,) * len(shape))
    return pl.pallas_call(
        functools.partial(_cm_spec_kernel, n1=n1, inv_n=1.0 / (n1 * n2)),
        out_shape=jax.ShapeDtypeStruct((nch, n1, 2 * n2), F32),
        grid=(nch // group,),
        in_specs=[pl.BlockSpec((group, n1, n2), lambda j: (j, 0, 0)),
                  full2((2 * n1, n1)), full2((2, n1, n2)), full2((2 * n2, 2 * n2))],
        out_specs=pl.BlockSpec((group, n1, 2 * n2), lambda j: (j, 0, 0)),
        compiler_params=_params(("parallel",)),
        name="hyena_filter_spectrum",
    )(taps, f1, tw, wb)


def _cm_conv_kernel(x_ref, gate_ref, kf_ref, skip_ref, f1_ref, f1h_ref, tw_ref, wb_ref, wbt_ref, o_ref, *, n1, n2):
    group = x_ref.shape[0]
    twr, twi = tw_ref[0], tw_ref[1]
    z = _cm_forward(x_ref, f1_ref[...], twr, twi, wb_ref[...], n1)
    kf = kf_ref[...].reshape(group * n1, 2 * n2)
    zr, zi, kr, ki = z[:, :n2], z[:, n2:], kf[:, :n2], kf[:, n2:]
    s = jnp.concatenate([zr * kr - zi * ki, zr * ki + zi * kr], axis=1).astype(BF16)
    v = _dot(s, wbt_ref[...])
    f1h = f1h_ref[...]
    for g in range(group):
        vr, vi = v[g * n1:(g + 1) * n1, :n2], v[g * n1:(g + 1) * n1, n2:]
        u = jnp.concatenate([vr * twr + vi * twi, vi * twr - vr * twi], axis=0).astype(BF16)
        y = _dot(f1h, u)
        y = y + skip_ref[g] * x_ref[g].astype(F32)
        o_ref[g] = (gate_ref[g].astype(F32) * y).astype(o_ref.dtype)


def _cm_conv(x, gate, kf, skip, f1d, f1h, tw, wb, wbt, *, kf_off, out_dtype, group=16):
    batch, ch, half, n2 = x.shape
    n1 = 2 * half
    group = min(group, ch)
    sig = pl.BlockSpec((None, group, half, n2), lambda j, b: (b, j, 0, 0))
    full2 = lambda shape: pl.BlockSpec(shape, lambda j, b: (0,) * len(shape))
    skip_b = jnp.broadcast_to(skip.astype(F32)[:, None, None], (ch, 1, n2))
    return pl.pallas_call(
        functools.partial(_cm_conv_kernel, n1=n1, n2=n2),
        out_shape=jax.ShapeDtypeStruct((batch, ch, half, n2), out_dtype),
        grid=(ch // group, batch),
        in_specs=[sig, sig,
                  pl.BlockSpec((group, n1, 2 * n2), lambda j, b: (kf_off // group + j, 0, 0)),
                  pl.BlockSpec((group, 1, n2), lambda j, b: (j, 0, 0)),
                  full2((2 * n1, half)), full2((half, 2 * n1)), full2((2, n1, n2)),
                  full2((2 * n2, 2 * n2)), full2((2 * n2, 2 * n2))],
        out_specs=sig,
        compiler_params=_params(("parallel", "parallel")),
        name="hyena_fft_conv",
    )(x, gate, kf, skip_b, f1d, f1h, tw, wb, wbt)


def _cm_to_time_major_kernel(y_ref, o_ref, *, half):
    tc = o_ref.shape[1]
    n2 = y_ref.shape[1]

    def body(a, carry):
        slab = y_ref[pl.ds(a, tc, stride=half), :]
        o_ref[pl.ds(pl.multiple_of(a * n2, n2), n2), :] = slab.T.astype(o_ref.dtype)
        return carry

    lax.fori_loop(0, half, body, 0, unroll=min(4, half))


def _cm_to_time_major(y, *, batch, seq, ch):
    half, n2 = y.shape[2], y.shape[3]
    tc = LANES
    return pl.pallas_call(
        functools.partial(_cm_to_time_major_kernel, half=half),
        out_shape=jax.ShapeDtypeStruct((batch * seq, ch), BF16),
        grid=(batch, ch // tc),
        in_specs=[pl.BlockSpec((None, tc * half, n2), lambda b, j: (b, j, 0))],
        out_specs=pl.BlockSpec((seq, tc), lambda b, j: (b, j)),
        compiler_params=_params(("parallel", "parallel")),
        name="hyena_to_time_major",
    )(y.reshape(batch, ch * half, n2))


def _hyena_cm(proj, hy_off, conv_w, conv_b, w1, b1, w2, b2, w3, freq, skip, *, batch, seq, ch):
    n2 = FFT_N2
    n1 = 2 * seq // n2
    half = n1 // 2
    v, x1, x2 = _cm_short_conv(proj, conv_w, conv_b, batch=batch, seq=seq, off=hy_off, ch=ch, n2=n2)
    taps = _cm_taps(seq, w1, b1, w2, b2, w3, freq, ch, n2)
    f1r, f1i, tw, wb, wbt = _cm_tables(n1, n2)
    fstack = jnp.concatenate([f1r, f1i], axis=0)
    kf = _cm_spectrum(taps, fstack.astype(BF16), tw, wb)
    f1d = fstack[:, :half].astype(BF16)
    f1h = jnp.concatenate([f1r[:half], f1i[:half]], axis=1).astype(BF16)
    zcur = v
    for o, gate in enumerate((x1, x2)):
        last = o == HYENA_ORDER - 1
        zcur = _cm_conv(zcur, gate, kf, skip[o], f1d, f1h, tw, wb, wbt, kf_off=o * ch,
                        out_dtype=F32 if last else BF16)
    return _cm_to_time_major(zcur, batch=batch, seq=seq, ch=ch)


def _merge_kernel(ya_ref, yb_ref, yc_ref, ga_ref, gb_ref, gc_ref, w_ref, o_ref):
    acc = None
    for n, (y_ref, g_ref) in enumerate(((ya_ref, ga_ref), (yb_ref, gb_ref), (yc_ref, gc_ref))):
        gate = 1.0 / (1.0 + jnp.exp(-g_ref[...].astype(F32)))
        term = gate * _dot(y_ref[...], w_ref[n])
        acc = term if acc is None else acc + term
    o_ref[...] = acc.astype(o_ref.dtype)


def _merge(ys, proj, gate_off, w_branch, layer, *, d_model, tm=1024, tn=512):
    m, width = ys[0].shape
    tm, tn = min(tm, m), min(tn, d_model)
    y_spec = pl.BlockSpec((tm, width), lambda i, j: (i, 0))
    g_specs = [pl.BlockSpec((tm, tn), functools.partial(
        lambda i, j, n: (i, (gate_off + n * d_model) // tn + j), n=n)) for n in range(N_BRANCH)]
    return pl.pallas_call(
        _merge_kernel,
        out_shape=jax.ShapeDtypeStruct((m, d_model), BF16),
        grid=(m // tm, d_model // tn),
        in_specs=[y_spec, y_spec, y_spec] + g_specs +
                 [pl.BlockSpec((None, N_BRANCH, width, tn), lambda i, j: (layer, 0, 0, j))],
        out_specs=pl.BlockSpec((tm, tn), lambda i, j: (i, j)),
        compiler_params=_params(("parallel", "parallel")),
        name="branch_merge",
    )(*ys, proj, proj, proj, w_branch)


def _xattn_kernel(q_ref, k_ref, v_ref, o_ref, *, scale):
    s = _dot_nt(q_ref[...], k_ref[...]) * scale
    s = s - jnp.max(s, axis=-1, keepdims=True)
    p = jnp.exp(s)
    l = jnp.sum(p, axis=-1, keepdims=True)
    o_ref[...] = (_dot(p.astype(BF16), v_ref[...]) / l).astype(o_ref.dtype)


def _xattn(q, kv, *, batch, seq, mem, heads, hd, tq=1024):
    tq = min(tq, seq)
    nq = seq // tq
    return pl.pallas_call(
        functools.partial(_xattn_kernel, scale=hd ** -0.5),
        out_shape=jax.ShapeDtypeStruct(q.shape, BF16),
        grid=(batch, heads, nq),
        in_specs=[pl.BlockSpec((tq, hd), lambda b, h, i: (b * nq + i, h)),
                  pl.BlockSpec((mem, hd), lambda b, h, i: (b, h)),
                  pl.BlockSpec((mem, hd), lambda b, h, i: (b, heads + h))],
        out_specs=pl.BlockSpec((tq, hd), lambda b, h, i: (b * nq + i, h)),
        compiler_params=_params(("parallel", "parallel", "parallel")),
        name="xattn_core",
    )(q, kv, kv)


def _in_proj_layout(d):
    mix = d // 4
    m_dk = mix // MLSTM_HEADS // 2
    g_dk = mix // GLA_HEADS // 2
    sizes = (MLSTM_HEADS * m_dk, MLSTM_HEADS * m_dk, mix, mix, 4 * MLSTM_HEADS,
             GLA_HEADS * g_dk, GLA_HEADS * g_dk, mix, mix, 2 * GLA_RANK, 3 * mix, N_BRANCH * d)
    offs = np.concatenate([[0], np.cumsum(sizes)])
    wide = (0, 1, 2, 3, 5, 6, 7, 8, 10, 11)
    narrow = (4, 9)
    new_off, pos = {}, 0
    for i in wide:
        new_off[i] = pos
        pos += sizes[i]
    return sizes, offs, wide, narrow, new_off


def _relayout_kernel(main_ref, tail_ref, o_ref, *, groups):
    tn = o_ref.shape[1]
    j = pl.program_id(2)
    both = jnp.concatenate([main_ref[...], tail_ref[...]], axis=1)
    for shift, lo, hi in groups:
        @pl.when((j >= lo) & (j < hi))
        def _(shift=shift):
            o_ref[...] = both[:, shift:shift + tn].astype(o_ref.dtype)


def _relayout_w_in(w_in, d, tk=1024, tn=512):
    sizes, offs, wide, _, new_off = _in_proj_layout(d)
    depth, kdim, _ = w_in.shape
    n_new = sum(sizes[i] for i in wide)
    groups = []
    for i in wide:
        shift, lo, hi = int(offs[i]) - new_off[i], new_off[i], new_off[i] + sizes[i]
        assert 0 <= shift <= LANES and lo % tn == 0 and hi % tn == 0
        if groups and groups[-1][0] == shift and groups[-1][2] == lo // tn:
            groups[-1] = (shift, groups[-1][1], hi // tn)
        else:
            groups.append((shift, lo // tn, hi // tn))
    tk = min(tk, kdim)
    return pl.pallas_call(
        functools.partial(_relayout_kernel, groups=tuple(groups)),
        out_shape=jax.ShapeDtypeStruct((depth, kdim, n_new), BF16),
        grid=(depth, kdim // tk, n_new // tn),
        in_specs=[pl.BlockSpec((None, tk, tn), lambda l, k, j: (l, k, j)),
                  pl.BlockSpec((None, tk, LANES), lambda l, k, j: (l, k, (j + 1) * (tn // LANES)))],
        out_specs=pl.BlockSpec((None, tk, tn), lambda l, k, j: (l, k, j)),
        compiler_params=_params(("parallel", "parallel", "parallel")),
        name="w_in_relayout",
    )(w_in, w_in)


def kernel(x, mem, norm_gains, final_norm, w_in, b_in, mlstm_head_norm, gla_decay_up, gla_decay_bias, gla_head_norm, hyena_conv_w, hyena_conv_b, hyena_ffn_w1, hyena_ffn_b1, hyena_ffn_w2, hyena_ffn_b2, hyena_ffn_w3, hyena_freq, hyena_skip, w_branch, w_out, xattn_wq, xattn_wkv, xattn_wo, mlp_w1, mlp_w2):
    batch, seq, d = x.shape
    mem_tokens = mem.shape[1]
    depth = norm_gains.shape[0]
    mix = d // 4
    mh, gh = MLSTM_HEADS, GLA_HEADS
    m_dv, g_dv = mix // mh, mix // gh
    m_dk, g_dk = m_dv // 2, g_dv // 2
    nc = seq // SCAN_CHUNK

    sizes, offs, wide, narrow, new_off = _in_proj_layout(d)
    seg = lambda a, i: a[..., offs[i]:offs[i + 1]]
    w_main = _relayout_w_in(w_in, d)
    b_main = jnp.concatenate([seg(b_in, i) for i in wide], axis=1)[:, None, :].astype(F32)
    n_small = sum(sizes[i] for i in narrow)
    w_small = jnp.pad(jnp.concatenate([seg(w_in, i) for i in narrow], axis=2),
                      ((0, 0), (0, 0), (0, LANES - n_small))).astype(BF16)
    b_small = jnp.pad(jnp.concatenate([seg(b_in, i) for i in narrow], axis=1),
                      ((0, 0), (0, LANES - n_small)))[:, None, :].astype(F32)
    wb16, wo16 = w_branch.astype(BF16), w_out.astype(BF16)
    wq16, wkv16, wxo16 = xattn_wq.astype(BF16), xattn_wkv.astype(BF16), xattn_wo.astype(BF16)
    w1_16, w2_16 = mlp_w1.astype(BF16), mlp_w2.astype(BF16)

    h = x.reshape(batch * seq, d)
    memf = mem.reshape(batch * mem_tokens, d)
    for l in range(depth):
        g = norm_gains[l]
        xn = _rmsnorm(h, g[0], BF16)
        proj = _mm(xn, w_main, l, bias=b_main, out_dtype=BF16, tn=1024, name="in_proj")
        small = _mm(xn, w_small, l, bias=b_small, out_dtype=F32, name="in_proj_gates")

        mg = small[:, :4 * mh].reshape(batch, nc, SCAN_CHUNK, 2, 2, mh)
        mgates = mg.transpose(0, 3, 1, 5, 4, 2).reshape(batch * 2, nc, 2 * mh, SCAN_CHUNK)
        h_m = _mlstm(proj, mgates, batch=batch, seq=seq, q_off=new_off[0], k_off=new_off[1],
                     v_off=new_off[2], heads=mh, dk=m_dk, dv=m_dv)
        y_a = _headnorm_gate(h_m, proj, new_off[3], mlstm_head_norm[l], heads=mh, dv=m_dv, act="sigmoid")

        up_pad = jnp.zeros((2, LANES, gh * g_dk), F32)
        for dr in range(2):
            r0 = 4 * mh + dr * GLA_RANK
            up_pad = up_pad.at[dr, r0:r0 + GLA_RANK, :].set(gla_decay_up[l, dr].astype(F32))
        gbias = gla_decay_bias[l].reshape(2, 1, gh * g_dk).astype(F32)
        h_g = _gla(proj, small, up_pad, gbias, batch=batch, seq=seq, q_off=new_off[5], k_off=new_off[6],
                   v_off=new_off[7], heads=gh, dk=g_dk, dv=g_dv)
        y_b = _headnorm_gate(h_g, proj, new_off[8], gla_head_norm[l], heads=gh, dv=g_dv, act="silu")

        y_c = _hyena_cm(proj, new_off[10], hyena_conv_w[l], hyena_conv_b[l], hyena_ffn_w1[l], hyena_ffn_b1[l],
                        hyena_ffn_w2[l], hyena_ffn_b2[l], hyena_ffn_w3[l], hyena_freq[l], hyena_skip[l],
                        batch=batch, seq=seq, ch=mix)

        merged = _merge((y_a, y_b, y_c), proj, new_off[11], wb16, l, d_model=d)
        h = _mm(merged, wo16, l, resid=h, out_dtype=F32, name="mixer_out")

        xn = _rmsnorm(h, g[1], BF16)
        memn = _rmsnorm(memf, g[2], BF16)
        q = _mm(xn, wq16, l, out_dtype=BF16, tn=1024, name="xattn_q")
        kv = _mm(memn, wkv16, l, out_dtype=BF16, tn=1024, name="xattn_kv")
        o = _xattn(q, kv, batch=batch, seq=seq, mem=mem_tokens, heads=XATTN_HEADS, hd=d // XATTN_HEADS)
        h = _mm(o, wxo16, l, resid=h, out_dtype=F32, name="xattn_out")

        xn = _rmsnorm(h, g[3], BF16)
        hid = _mm(xn, w1_16, l, act="relu2", out_dtype=BF16, tn=1024, name="mlp_up")
        h = _mm(hid, w2_16, l, resid=h, out_dtype=F32, tm=512, tn=256, name="mlp_down")
    return _rmsnorm(h, final_norm, x.dtype).reshape(batch, seq, d)
```

```python
import functools
import math

import numpy as np
import jax
import jax.numpy as jnp
from jax import lax
from jax.experimental import pallas as pl
from jax.experimental.pallas import tpu as pltpu

F32 = jnp.float32
BF16 = jnp.bfloat16

N_BRANCH = 3
MLSTM_HEADS = 4
GLA_HEADS = 4
GLA_RANK = 16
GLA_TAU = 16.0
HYENA_ORDER = 2
HYENA_BANDS = 16
HYENA_MIN_DECAY = math.log(1e-2) / 1.5
HYENA_MAX_DECAY = math.log(1e-2) / 0.3
XATTN_HEADS = 4
EPS = 1e-6

LANES = 128
BF16_TILE_ROWS = 16
VMEM_LIMIT_BYTES = 56 * 1024 * 1024

SCAN_CHUNK = 128
FFT_N2 = 128
NEG_BIG = -1e30
GATHER_UNROLL = 8
EXACT_PIECES = 2


def _params(sem):
    return pltpu.CompilerParams(dimension_semantics=sem, vmem_limit_bytes=VMEM_LIMIT_BYTES)


def _log_sigmoid(x):
    return -(jnp.maximum(-x, 0.0) + jnp.log(1.0 + jnp.exp(-jnp.abs(x))))


def _split_bf16(a, parts):
    out = []
    r = a
    for _ in range(parts):
        p = r.astype(BF16)
        out.append(p)
        r = r - p.astype(F32)
    return out


def _dot(a, b):
    return jnp.dot(a, b, preferred_element_type=F32)


def _dot_nt(a, b):
    return lax.dot_general(a, b, (((1,), (1,)), ((), ())), preferred_element_type=F32)


def _dot_tn(a, b):
    return lax.dot_general(a, b, (((0,), (0,)), ((), ())), preferred_element_type=F32)


def _dot3(a, b):
    a_hi, a_lo = _split_bf16(a, 2)
    b_hi, b_lo = _split_bf16(b, 2)
    return _dot(a_hi, b_hi) + _dot(a_hi, b_lo) + _dot(a_lo, b_hi)


def _rmsnorm_kernel(x_ref, g_ref, o_ref):
    x = x_ref[...].astype(F32)
    r = lax.rsqrt(jnp.mean(x * x, axis=-1, keepdims=True) + EPS)
    o_ref[...] = ((x * r) * g_ref[...]).astype(o_ref.dtype)


def _rmsnorm(x, g, out_dtype, tm=256):
    m, d = x.shape
    tm = min(tm, m)
    return pl.pallas_call(
        _rmsnorm_kernel,
        out_shape=jax.ShapeDtypeStruct((m, d), out_dtype),
        grid=(m // tm,),
        in_specs=[pl.BlockSpec((tm, d), lambda i: (i, 0)),
                  pl.BlockSpec((1, d), lambda i: (0, 0))],
        out_specs=pl.BlockSpec((tm, d), lambda i: (i, 0)),
        compiler_params=_params(("parallel",)),
        name="rmsnorm",
    )(x, g.reshape(1, d).astype(F32))


def _mm_kernel(*refs, nk, act, has_bias, has_resid, w_rows):
    x_ref, w_ref = refs[0], refs[1]
    product = _dot_nt if w_rows else _dot
    idx = 2
    b_ref = r_ref = None
    if has_bias:
        b_ref = refs[idx]
        idx += 1
    if has_resid:
        r_ref = refs[idx]
        idx += 1
    o_ref = refs[idx]

    def epilogue(acc):
        if has_bias:
            acc = acc + b_ref[...]
        if act == "relu2":
            acc = jnp.square(jnp.maximum(acc, 0.0))
        if has_resid:
            acc = acc + r_ref[...]
        o_ref[...] = acc.astype(o_ref.dtype)

    if nk == 1:
        epilogue(product(x_ref[...], w_ref[...]))
    else:
        acc_ref = refs[idx + 1]
        k = pl.program_id(2)

        @pl.when(k == 0)
        def _():
            acc_ref[...] = jnp.zeros_like(acc_ref)

        acc_ref[...] += product(x_ref[...], w_ref[...])

        @pl.when(k == nk - 1)
        def _():
            epilogue(acc_ref[...])


def _mm(x, w, layer, *, bias=None, resid=None, act=None, out_dtype=BF16, tm=1024, tn=512, tk=None,
        w_rows=False, name="mm"):
    m, kdim = x.shape
    n = w.shape[1] if w_rows else w.shape[2]
    tm, tn = min(tm, m), min(tn, n)
    tk = kdim if tk is None else min(tk, kdim)
    nk = kdim // tk
    assert m % tm == 0 and n % tn == 0 and kdim % tk == 0
    w_spec = (pl.BlockSpec((None, tn, tk), lambda i, j, k: (layer, j, k)) if w_rows
              else pl.BlockSpec((None, tk, tn), lambda i, j, k: (layer, k, j)))
    in_specs = [pl.BlockSpec((tm, tk), lambda i, j, k: (i, k)), w_spec]
    args = [x, w]
    if bias is not None:
        in_specs.append(pl.BlockSpec((None, 1, tn), lambda i, j, k: (layer, 0, j)))
        args.append(bias)
    if resid is not None:
        in_specs.append(pl.BlockSpec((tm, tn), lambda i, j, k: (i, j)))
        args.append(resid)
    scratch = [pltpu.VMEM((tm, tn), F32)] if nk > 1 else []
    return pl.pallas_call(
        functools.partial(_mm_kernel, nk=nk, act=act, has_bias=bias is not None,
                          has_resid=resid is not None, w_rows=w_rows),
        out_shape=jax.ShapeDtypeStruct((m, n), out_dtype),
        grid=(m // tm, n // tn, nk),
        in_specs=in_specs,
        out_specs=pl.BlockSpec((tm, tn), lambda i, j, k: (i, j)),
        scratch_shapes=scratch,
        compiler_params=_params(("parallel", "parallel", "arbitrary")),
        name=name,
    )(*args)


def _rows_to_cols(sel, rows):
    return sum(_dot_nt(sel, piece) for piece in _split_bf16(rows, EXACT_PIECES))


def _mlstm_kernel(q_ref, k_ref, v_ref, g_ref, o_ref, c_ref, m_ref, *, nc, heads, dk, dv, scale):
    lc = SCAN_CHUNK
    rev = pl.program_id(0) % 2
    c = pl.program_id(1)
    cc = c + rev * (nc - 1 - 2 * c)

    @pl.when(c == 0)
    def _():
        c_ref[...] = jnp.zeros_like(c_ref)
        m_ref[...] = jnp.zeros_like(m_ref)

    row = lax.broadcasted_iota(jnp.int32, (lc, lc), 0)
    col = lax.broadcasted_iota(jnp.int32, (lc, lc), 1)
    tri = ((col - row) * (1 - 2 * rev)) <= 0
    tri_b = jnp.where(tri, 1.0, 0.0).astype(BF16)
    eye_b = jnp.where(row == col, 1.0, 0.0).astype(BF16)
    ones_blk = (lax.broadcasted_iota(jnp.int32, (lc, LANES), 1) == 0).astype(BF16)

    gl = g_ref[cc]
    lf_all = _log_sigmoid(gl)
    lf_pieces = _split_bf16(lf_all, EXACT_PIECES)
    b_rows = sum(_dot_nt(piece, tri_b) for piece in lf_pieces)
    b_cols = sum(_dot_nt(tri_b, piece) for piece in lf_pieces)
    g_all = jnp.sum(lf_all, axis=1, keepdims=True)
    wk_rows, m_news, decays = [], [], []
    for hh in range(heads):
        ig = gl[2 * hh:2 * hh + 1, :]
        g = g_all[2 * hh + 1:2 * hh + 2, :]
        m_prev = m_ref[hh]
        a_row = g - b_rows[2 * hh + 1:2 * hh + 2, :] + ig
        m_new = jnp.maximum(g + m_prev, jnp.max(a_row, axis=1, keepdims=True))
        wk_rows.append(jnp.exp(a_row - m_new))
        m_news.append(m_new)
        decays.append(jnp.exp(g + m_prev - m_new))
    wk_cols = _rows_to_cols(eye_b, jnp.concatenate(wk_rows + [jnp.zeros((heads, lc), F32)], axis=0))

    rng = range(heads)
    qs = [q_ref[:, hh * dk:(hh + 1) * dk] for hh in rng]
    ks = [k_ref[:, hh * dk:(hh + 1) * dk] for hh in rng]
    v_augs = [jnp.concatenate([v_ref[:, hh * dv:(hh + 1) * dv], ones_blk], axis=1) for hh in rng]
    qk = [_dot_nt(qs[hh], ks[hh]) for hh in rng]
    qc = [_dot(qs[hh], c_ref[hh].astype(BF16)) for hh in rng]
    kv = [_dot_tn((ks[hh].astype(F32) * wk_cols[:, hh:hh + 1]).astype(BF16), v_augs[hh]) for hh in rng]
    for hh in rng:
        b_col = b_cols[:, 2 * hh + 1:2 * hh + 2]
        log_d = jnp.where(tri, b_col - b_rows[2 * hh + 1:2 * hh + 2, :] + gl[2 * hh:2 * hh + 1, :], NEG_BIG)
        m_inter = b_col + m_ref[hh]
        m_j = jnp.maximum(m_inter, jnp.max(log_d, axis=1, keepdims=True))
        smat = (qk[hh] * jnp.exp(log_d - m_j)).astype(BF16)
        num_aug = (_dot(smat, v_augs[hh]) + jnp.exp(m_inter - m_j) * qc[hh]) * scale
        num = num_aug[:, :dv]
        den = num_aug[:, dv:dv + 1]
        o_ref[:, hh * dv:(hh + 1) * dv] = (num / jnp.maximum(jnp.abs(den), jnp.exp(-m_j))).astype(o_ref.dtype)
        c_ref[hh] = decays[hh] * c_ref[hh] + kv[hh]
        m_ref[hh] = m_news[hh]


def _mlstm(proj, gates, *, batch, seq, q_off, k_off, v_off, heads, dk, dv):
    lc = SCAN_CHUNK
    nc = seq // lc
    assert dk == lc and seq % lc == 0 and dv % LANES == 0

    def rowblk(s, c):
        return (s // 2) * nc + c + (s % 2) * (nc - 1 - 2 * c)

    wqk, wv = heads * dk, heads * dv
    return pl.pallas_call(
        functools.partial(_mlstm_kernel, nc=nc, heads=heads, dk=dk, dv=dv, scale=dk ** -0.5),
        out_shape=jax.ShapeDtypeStruct((2, batch * seq, wv), F32),
        grid=(batch * 2, nc),
        in_specs=[
            pl.BlockSpec((lc, wqk), lambda s, c: (rowblk(s, c), q_off // wqk)),
            pl.BlockSpec((lc, wqk), lambda s, c: (rowblk(s, c), k_off // wqk)),
            pl.BlockSpec((lc, wv), lambda s, c: (rowblk(s, c), v_off // wv)),
            pl.BlockSpec((None, nc, 2 * heads, lc), lambda s, c: (s, 0, 0, 0)),
        ],
        out_specs=pl.BlockSpec((None, lc, wv), lambda s, c: (s % 2, rowblk(s, c), 0)),
        scratch_shapes=[pltpu.VMEM((heads, dk, dv + LANES), F32), pltpu.VMEM((heads, 1, 1), F32)],
        compiler_params=_params(("parallel", "arbitrary")),
        name="mlstm_scan",
    )(proj, proj, proj, gates)


def _gla_levels():
    lc = SCAN_CHUNK
    nlev = int(math.log2(lc)) + 1
    t = np.arange(lc)
    diff = np.zeros((2, nlev * lc, lc), np.float32)
    mask = np.zeros((2, nlev, lc, lc), np.float32)
    u = t[None, :]
    for d in range(2):
        cum = (u <= t[:, None]) if d == 0 else (u >= t[:, None])
        diff[d, :lc] = cum
        mask[d, 0] = np.eye(lc)
        for l in range(1, nlev):
            w = 1 << (l - 1)
            pair = t // (2 * w)
            second = (t % (2 * w)) >= w
            if d == 0:
                bd = pair * 2 * w + w - 1
                cum_bd = u <= bd[:, None]
                is_q, is_k = second, ~second
            else:
                bd = pair * 2 * w + w
                cum_bd = u >= bd[:, None]
                is_q, is_k = ~second, second
            diff[d, l * lc:(l + 1) * lc] = cum.astype(np.float32) - cum_bd.astype(np.float32)
            mask[d, l] = ((pair[:, None] == pair[None, :]) & is_q[:, None] & is_k[None, :])
    return diff, mask


def _gla_kernel(q_ref, k_ref, v_ref, sm_ref, up_ref, bias_ref, d_ref, mask_ref, o_ref, s_ref, *,
                heads, dk, dv, nlev, scale):
    lc = SCAN_CHUNK
    c = pl.program_id(1)

    @pl.when(c == 0)
    def _():
        s_ref[...] = jnp.zeros_like(s_ref)

    z = _dot3(sm_ref[...], up_ref[...]) + bias_ref[...]
    la = _log_sigmoid(z) * (1.0 / GLA_TAU)
    dmat = d_ref[...]
    x = sum(_dot(dmat, piece) for piece in _split_bf16(la, EXACT_PIECES))
    b = x[:lc]
    g_row = jnp.sum(la, axis=0, keepdims=True)

    qb = q_ref[...]
    kb = k_ref[...]
    q = qb.astype(F32)
    k = kb.astype(F32)
    hs = [slice(hh * dk, (hh + 1) * dk) for hh in range(heads)]
    atts = [mask_ref[0] * _dot_nt(qb[:, s], kb[:, s]) for s in hs]
    for l in range(1, nlev):
        e = jnp.exp(-jnp.abs(x[l * lc:(l + 1) * lc]))
        qe = (q * e).astype(BF16)
        ke = (k * e).astype(BF16)
        m = mask_ref[l]
        atts = [att + m * _dot_nt(qe[:, s], ke[:, s]) for att, s in zip(atts, hs)]

    qs = (q * jnp.exp(b)).astype(BF16)
    kd = (k * jnp.exp(g_row - b)).astype(BF16)
    eg = jnp.exp(g_row)
    for hh, s in enumerate(hs):
        v = v_ref[:, hh * dv:(hh + 1) * dv]
        o = _dot(atts[hh].astype(BF16), v) + _dot_nt(qs[:, s], s_ref[hh].astype(BF16))
        o_ref[:, hh * dv:(hh + 1) * dv] = (o * scale).astype(o_ref.dtype)
        s_ref[hh] = eg[:, s] * s_ref[hh] + _dot_tn(v, kd[:, s])


def _gla(proj, small, up_pad, bias, *, batch, seq, q_off, k_off, v_off, heads, dk, dv):
    lc = SCAN_CHUNK
    nc = seq // lc
    assert dk == lc and seq % lc == 0
    diff, mask = _gla_levels()
    nlev = mask.shape[1]

    def rowblk(s, c):
        return (s // 2) * nc + c + (s % 2) * (nc - 1 - 2 * c)

    wqk, wv = heads * dk, heads * dv
    return pl.pallas_call(
        functools.partial(_gla_kernel, heads=heads, dk=dk, dv=dv, nlev=nlev, scale=dk ** -0.5),
        out_shape=jax.ShapeDtypeStruct((2, batch * seq, wv), F32),
        grid=(batch * 2, nc),
        in_specs=[
            pl.BlockSpec((lc, wqk), lambda s, c: (rowblk(s, c), q_off // wqk)),
            pl.BlockSpec((lc, wqk), lambda s, c: (rowblk(s, c), k_off // wqk)),
            pl.BlockSpec((lc, wv), lambda s, c: (rowblk(s, c), v_off // wv)),
            pl.BlockSpec((lc, LANES), lambda s, c: (rowblk(s, c), 0)),
            pl.BlockSpec((None, LANES, wqk), lambda s, c: (s % 2, 0, 0)),
            pl.BlockSpec((None, 1, wqk), lambda s, c: (s % 2, 0, 0)),
            pl.BlockSpec((None, nlev * lc, lc), lambda s, c: (s % 2, 0, 0)),
            pl.BlockSpec((None, nlev, lc, lc), lambda s, c: (s % 2, 0, 0, 0)),
        ],
        out_specs=pl.BlockSpec((None, lc, wv), lambda s, c: (s % 2, rowblk(s, c), 0)),
        scratch_shapes=[pltpu.VMEM((heads, dv, dk), F32)],
        compiler_params=_params(("parallel", "arbitrary")),
        name="gla_scan",
    )(proj, proj, proj, small, up_pad, bias, jnp.asarray(diff, BF16), jnp.asarray(mask, F32))


def _headnorm_gate_kernel(h_ref, g_ref, hn_ref, o_ref, *, heads, dv, act):
    h = h_ref[0] + h_ref[1]
    outs = []
    for hh in range(heads):
        blk = h[:, hh * dv:(hh + 1) * dv]
        r = lax.rsqrt(jnp.mean(blk * blk, axis=-1, keepdims=True) + EPS)
        outs.append(blk * r)
    hn = jnp.concatenate(outs, axis=1) * hn_ref[...]
    gate = g_ref[...].astype(F32)
    sig = 1.0 / (1.0 + jnp.exp(-gate))
    gate = sig if act == "sigmoid" else gate * sig
    o_ref[...] = (hn * gate).astype(o_ref.dtype)


def _headnorm_gate(h2, proj, gate_off, head_norm, *, heads, dv, act, tm=512):
    _, m, width = h2.shape
    tm = min(tm, m)
    return pl.pallas_call(
        functools.partial(_headnorm_gate_kernel, heads=heads, dv=dv, act=act),
        out_shape=jax.ShapeDtypeStruct((m, width), BF16),
        grid=(m // tm,),
        in_specs=[pl.BlockSpec((2, tm, width), lambda i: (0, i, 0)),
                  pl.BlockSpec((tm, width), lambda i: (i, gate_off // width)),
                  pl.BlockSpec((1, width), lambda i: (0, 0))],
        out_specs=pl.BlockSpec((tm, width), lambda i: (i, 0)),
        compiler_params=_params(("parallel",)),
        name="headnorm_gate",
    )(h2, proj, head_norm.reshape(1, width).astype(F32))


def _cm_short_conv_kernel(x0_ref, x1_ref, x2_ref, w_ref, b_ref, o0_ref, o1_ref, o2_ref, s_ref, t_ref):
    seq, tc = x0_ref.shape
    half, n2 = o0_ref.shape[1], o0_ref.shape[2]
    for p, (x_ref, o_ref) in enumerate(((x0_ref, o0_ref), (x1_ref, o1_ref), (x2_ref, o2_ref))):
        x = x_ref[...].astype(F32)
        w = w_ref[p]
        bias = b_ref[p]
        s_ref[...] = (pltpu.roll(x, 1, axis=0) * w[0:1] + x * w[1:2]
                      + pltpu.roll(x, seq - 1, axis=0) * w[2:3] + bias)
        s_ref[0:1, :] = x[0:1] * w[1:2] + x[1:2] * w[2:3] + bias
        s_ref[seq - 1:seq, :] = x[seq - 2:seq - 1] * w[0:1] + x[seq - 1:seq] * w[1:2] + bias

        def flip(a, carry):
            slab = s_ref[pl.ds(pl.multiple_of(a * n2, n2), n2), :]
            t_ref[pl.ds(pl.multiple_of(a * tc, tc), tc), :] = slab.T
            return carry

        lax.fori_loop(0, half, flip, 0, unroll=min(4, half))

        def body(ch, carry, o_ref=o_ref):
            o_ref[ch] = t_ref[pl.ds(ch, half, stride=tc), :].astype(o_ref.dtype)
            return carry

        lax.fori_loop(0, tc, body, 0, unroll=GATHER_UNROLL)


def _cm_short_conv(proj, conv_w, conv_b, *, batch, seq, off, ch, n2):
    tc = LANES
    nct = ch // tc
    half = seq // n2
    w = conv_w.reshape(3, 3, nct, tc).transpose(1, 2, 0, 3).astype(F32)
    b = conv_b.reshape(3, nct, 1, tc).astype(F32)
    in_specs = [pl.BlockSpec((seq, tc), functools.partial(lambda b_, j, p: (b_, off // tc + p * nct + j), p=p))
                for p in range(3)]
    in_specs += [pl.BlockSpec((3, None, 3, tc), lambda b_, j: (0, j, 0, 0)),
                 pl.BlockSpec((3, None, 1, tc), lambda b_, j: (0, j, 0, 0))]
    out = jax.ShapeDtypeStruct((batch, ch, half, n2), BF16)
    return pl.pallas_call(
        _cm_short_conv_kernel,
        out_shape=[out, out, out],
        grid=(batch, nct),
        in_specs=in_specs,
        out_specs=[pl.BlockSpec((None, tc, half, n2), lambda b_, j: (b_, j, 0, 0))] * 3,
        scratch_shapes=[pltpu.VMEM((seq, tc), F32), pltpu.VMEM((half * tc, n2), F32)],
        compiler_params=_params(("parallel", "parallel")),
        name="hyena_short_conv",
    )(proj, proj, proj, w, b)


def _cm_taps_kernel(z_ref, w1_ref, b1_ref, w2_ref, b2_ref, w3_ref, f_ref, dl_ref, o_ref, t_ref, h_ref, *, emb, n2):
    @pl.when(pl.program_id(1) == 0)
    def _():
        f = f_ref[...]
        h = jnp.sin(f * (_dot3(w1_ref[...], z_ref[...]) + b1_ref[...]))
        h_ref[...] = jnp.sin(f * (_dot3(w2_ref[...], h) + b2_ref[...]))

    win = jnp.exp(-dl_ref[...] * z_ref[0:1, :]) * z_ref[emb:emb + 1, :]
    out = _dot3(w3_ref[...], h_ref[...]) * win
    gc = out.shape[0]
    tile = out.shape[1] // n2
    for s in range(tile):
        t_ref[s * gc:(s + 1) * gc, :] = out[:, s * n2:(s + 1) * n2]

    def body(ch, carry):
        o_ref[ch] = t_ref[pl.ds(ch, tile, stride=gc), :].astype(o_ref.dtype)
        return carry

    lax.fori_loop(0, gc, body, 0, unroll=GATHER_UNROLL)


def _cm_taps(seq, w1, b1, w2, b2, w3, freq, ch, n2, gc=256):
    emb, hid = w1.shape
    n1 = 2 * seq // n2
    half = n1 // 2
    tile = min(BF16_TILE_ROWS, half)
    order = w3.shape[1] // (2 * ch)
    lag = jnp.arange(2 * seq, dtype=jnp.int32)
    pos = jnp.where(lag < seq, lag, 2 * seq - lag)
    n = pos.astype(F32)
    t = n / (seq - 1)
    bands = jnp.linspace(1e-4, HYENA_BANDS - 1, HYENA_BANDS, dtype=F32)
    ang = (2.0 * math.pi * n / seq)[:, None] * bands[None, :]
    valid = (lag != seq).astype(F32)
    z = jnp.concatenate([t[:, None], jnp.cos(ang), -jnp.sin(ang), valid[:, None]], axis=-1)
    zt = jnp.pad(z, ((0, 0), (0, LANES - emb - 1))).T
    deltas = jnp.abs(jnp.linspace(HYENA_MIN_DECAY, HYENA_MAX_DECAY, ch, dtype=F32))
    dl = jnp.tile(deltas, order).reshape(order * ch, 1)
    pad_h = LANES - hid
    w1t = jnp.pad(w1.astype(F32), ((0, LANES - emb), (0, pad_h))).T
    w2t = jnp.pad(w2.astype(F32), ((0, pad_h), (0, pad_h))).T
    w3t = jnp.pad(w3.astype(F32), ((0, pad_h), (0, 0))).T.reshape(order, 2, ch, LANES)
    w3t = w3t.transpose(1, 0, 2, 3).reshape(2, order * ch, LANES)
    colv = lambda v: jnp.pad(v.astype(F32), (0, pad_h)).reshape(LANES, 1)
    gc = min(gc, order * ch)
    nat = n1 // tile
    full = lambda shape: pl.BlockSpec(shape, lambda a, j: (0, 0))
    return pl.pallas_call(
        functools.partial(_cm_taps_kernel, emb=emb, n2=n2),
        out_shape=jax.ShapeDtypeStruct((order * ch, n1, n2), BF16),
        grid=(nat, order * ch // gc),
        in_specs=[pl.BlockSpec((LANES, tile * n2), lambda a, j: (0, a)),
                  full((LANES, LANES)), full((LANES, 1)), full((LANES, LANES)), full((LANES, 1)),
                  pl.BlockSpec((None, gc, LANES), lambda a, j: (a // (nat // 2), j, 0)),
                  full((LANES, 1)),
                  pl.BlockSpec((gc, 1), lambda a, j: (j, 0))],
        out_specs=pl.BlockSpec((gc, tile, n2), lambda a, j: (j, a, 0)),
        scratch_shapes=[pltpu.VMEM((tile * gc, n2), F32), pltpu.VMEM((LANES, tile * n2), F32)],
        compiler_params=_params(("parallel", "arbitrary")),
        name="hyena_filters",
    )(zt, w1t, colv(b1), w2t, colv(b2), w3t, colv(freq), dl)


def _cm_tables(n1, n2):
    a = jnp.arange(n1, dtype=jnp.int32)
    ang1 = ((a[:, None] * a[None, :]) % n1).astype(F32) * (2.0 * math.pi / n1)
    f1r, f1i = jnp.cos(ang1), -jnp.sin(ang1)
    b = jnp.arange(n2, dtype=jnp.int32)
    angt = (a[:, None] * b[None, :]).astype(F32) * (2.0 * math.pi / (n1 * n2))
    tw = jnp.stack([jnp.cos(angt), -jnp.sin(angt)])
    ang2 = ((b[:, None] * b[None, :]) % n2).astype(F32) * (2.0 * math.pi / n2)
    wr, wi = jnp.cos(ang2), -jnp.sin(ang2)
    wb = jnp.concatenate([jnp.concatenate([wr, wi], axis=1),
                          jnp.concatenate([-wi, wr], axis=1)], axis=0).astype(BF16)
    return f1r, f1i, tw, wb, wb.T


def _cm_forward(x_ref, f1, twr, twi, wb, n1):
    ys = []
    for g in range(x_ref.shape[0]):
        p = _dot(f1, x_ref[g])
        pr, pi = p[:n1], p[n1:]
        ys.append(jnp.concatenate([pr * twr - pi * twi, pr * twi + pi * twr], axis=1))
    y = jnp.concatenate(ys, axis=0).astype(BF16)
    return _dot(y, wb)


def _cm_spec_kernel(t_ref, f1_ref, tw_ref, wb_ref, o_ref, *, n1, inv_n):
    z = _cm_forward(t_ref, f1_ref[...], tw_ref[0], tw_ref[1], wb_ref[...], n1)
    o_ref[...] = (z * inv_n).reshape(o_ref.shape)


def _cm_spectrum(taps, f1, tw, wb, *, group=16):
    nch, n1, n2 = taps.shape
    group = min(group, nch)
    full2 = lambda shape: pl.BlockSpec(shape, lambda j: (0,) * len(shape))
    return pl.pallas_call(
        functools.partial(_cm_spec_kernel, n1=n1, inv_n=1.0 / (n1 * n2)),
        out_shape=jax.ShapeDtypeStruct((nch, n1, 2 * n2), F32),
        grid=(nch // group,),
        in_specs=[pl.BlockSpec((group, n1, n2), lambda j: (j, 0, 0)),
                  full2((2 * n1, n1)), full2((2, n1, n2)), full2((2 * n2, 2 * n2))],
        out_specs=pl.BlockSpec((group, n1, 2 * n2), lambda j: (j, 0, 0)),
        compiler_params=_params(("parallel",)),
        name="hyena_filter_spectrum",
    )(taps, f1, tw, wb)


def _cm_conv_kernel(x_ref, gate_ref, kf_ref, skip_ref, f1_ref, f1h_ref, tw_ref, wb_ref, wbt_ref, o_ref, *, n1, n2):
    group = x_ref.shape[0]
    twr, twi = tw_ref[0], tw_ref[1]
    z = _cm_forward(x_ref, f1_ref[...], twr, twi, wb_ref[...], n1)
    kf = kf_ref[...].reshape(group * n1, 2 * n2)
    zr, zi, kr, ki = z[:, :n2], z[:, n2:], kf[:, :n2], kf[:, n2:]
    s = jnp.concatenate([zr * kr - zi * ki, zr * ki + zi * kr], axis=1).astype(BF16)
    v = _dot(s, wbt_ref[...])
    f1h = f1h_ref[...]
    for g in range(group):
        vr, vi = v[g * n1:(g + 1) * n1, :n2], v[g * n1:(g + 1) * n1, n2:]
        u = jnp.concatenate([vr * twr + vi * twi, vi * twr - vr * twi], axis=0).astype(BF16)
        y = _dot(f1h, u)
        y = y + skip_ref[g] * x_ref[g].astype(F32)
        o_ref[g] = (gate_ref[g].astype(F32) * y).astype(o_ref.dtype)


def _cm_conv(x, gate, kf, skip, f1d, f1h, tw, wb, wbt, *, kf_off, out_dtype, group=16):
    batch, ch, half, n2 = x.shape
    n1 = 2 * half
    group = min(group, ch)
    sig = pl.BlockSpec((None, group, half, n2), lambda j, b: (b, j, 0, 0))
    full2 = lambda shape: pl.BlockSpec(shape, lambda j, b: (0,) * len(shape))
    skip_b = jnp.broadcast_to(skip.astype(F32)[:, None, None], (ch, 1, n2))
    return pl.pallas_call(
        functools.partial(_cm_conv_kernel, n1=n1, n2=n2),
        out_shape=jax.ShapeDtypeStruct((batch, ch, half, n2), out_dtype),
        grid=(ch // group, batch),
        in_specs=[sig, sig,
                  pl.BlockSpec((group, n1, 2 * n2), lambda j, b: (kf_off // group + j, 0, 0)),
                  pl.BlockSpec((group, 1, n2), lambda j, b: (j, 0, 0)),
                  full2((2 * n1, half)), full2((half, 2 * n1)), full2((2, n1, n2)),
                  full2((2 * n2, 2 * n2)), full2((2 * n2, 2 * n2))],
        out_specs=sig,
        compiler_params=_params(("parallel", "parallel")),
        name="hyena_fft_conv",
    )(x, gate, kf, skip_b, f1d, f1h, tw, wb, wbt)


def _cm_to_time_major_kernel(y_ref, o_ref, *, half):
    tc = o_ref.shape[1]
    n2 = y_ref.shape[1]

    def body(a, carry):
        slab = y_ref[pl.ds(a, tc, stride=half), :]
        o_ref[pl.ds(pl.multiple_of(a * n2, n2), n2), :] = slab.T.astype(o_ref.dtype)
        return carry

    lax.fori_loop(0, half, body, 0, unroll=min(4, half))


def _cm_to_time_major(y, *, batch, seq, ch):
    half, n2 = y.shape[2], y.shape[3]
    tc = LANES
    return pl.pallas_call(
        functools.partial(_cm_to_time_major_kernel, half=half),
        out_shape=jax.ShapeDtypeStruct((batch * seq, ch), BF16),
        grid=(batch, ch // tc),
        in_specs=[pl.BlockSpec((None, tc * half, n2), lambda b, j: (b, j, 0))],
        out_specs=pl.BlockSpec((seq, tc), lambda b, j: (b, j)),
        compiler_params=_params(("parallel", "parallel")),
        name="hyena_to_time_major",
    )(y.reshape(batch, ch * half, n2))


def _hyena_cm(proj, hy_off, conv_w, conv_b, w1, b1, w2, b2, w3, freq, skip, *, batch, seq, ch):
    n2 = FFT_N2
    n1 = 2 * seq // n2
    half = n1 // 2
    v, x1, x2 = _cm_short_conv(proj, conv_w, conv_b, batch=batch, seq=seq, off=hy_off, ch=ch, n2=n2)
    taps = _cm_taps(seq, w1, b1, w2, b2, w3, freq, ch, n2)
    f1r, f1i, tw, wb, wbt = _cm_tables(n1, n2)
    fstack = jnp.concatenate([f1r, f1i], axis=0)
    kf = _cm_spectrum(taps, fstack.astype(BF16), tw, wb)
    f1d = fstack[:, :half].astype(BF16)
    f1h = jnp.concatenate([f1r[:half], f1i[:half]], axis=1).astype(BF16)
    zcur = v
    for o, gate in enumerate((x1, x2)):
        last = o == HYENA_ORDER - 1
        zcur = _cm_conv(zcur, gate, kf, skip[o], f1d, f1h, tw, wb, wbt, kf_off=o * ch,
                        out_dtype=F32 if last else BF16)
    return _cm_to_time_major(zcur, batch=batch, seq=seq, ch=ch)


def _merge_kernel(ya_ref, yb_ref, yc_ref, ga_ref, gb_ref, gc_ref, w_ref, o_ref):
    acc = None
    for n, (y_ref, g_ref) in enumerate(((ya_ref, ga_ref), (yb_ref, gb_ref), (yc_ref, gc_ref))):
        gate = 1.0 / (1.0 + jnp.exp(-g_ref[...].astype(F32)))
        term = gate * _dot(y_ref[...], w_ref[n])
        acc = term if acc is None else acc + term
    o_ref[...] = acc.astype(o_ref.dtype)


def _merge(ys, proj, gate_off, w_branch, layer, *, d_model, tm=1024, tn=512):
    m, width = ys[0].shape
    tm, tn = min(tm, m), min(tn, d_model)
    y_spec = pl.BlockSpec((tm, width), lambda i, j: (i, 0))
    g_specs = [pl.BlockSpec((tm, tn), functools.partial(
        lambda i, j, n: (i, (gate_off + n * d_model) // tn + j), n=n)) for n in range(N_BRANCH)]
    return pl.pallas_call(
        _merge_kernel,
        out_shape=jax.ShapeDtypeStruct((m, d_model), BF16),
        grid=(m // tm, d_model // tn),
        in_specs=[y_spec, y_spec, y_spec] + g_specs +
                 [pl.BlockSpec((None, N_BRANCH, width, tn), lambda i, j: (layer, 0, 0, j))],
        out_specs=pl.BlockSpec((tm, tn), lambda i, j: (i, j)),
        compiler_params=_params(("parallel", "parallel")),
        name="branch_merge",
    )(*ys, proj, proj, proj, w_branch)


def _xattn_kernel(q_ref, k_ref, v_ref, o_ref, *, scale):
    s = _dot_nt(q_ref[...], k_ref[...]) * scale
    s = s - jnp.max(s, axis=-1, keepdims=True)
    p = jnp.exp(s)
    l = jnp.sum(p, axis=-1, keepdims=True)
    o_ref[...] = (_dot(p.astype(BF16), v_ref[...]) / l).astype(o_ref.dtype)


def _xattn(q, kv, *, batch, seq, mem, heads, hd, tq=1024):
    tq = min(tq, seq)
    nq = seq // tq
    return pl.pallas_call(
        functools.partial(_xattn_kernel, scale=hd ** -0.5),
        out_shape=jax.ShapeDtypeStruct(q.shape, BF16),
        grid=(batch, heads, nq),
        in_specs=[pl.BlockSpec((tq, hd), lambda b, h, i: (b * nq + i, h)),
                  pl.BlockSpec((mem, hd), lambda b, h, i: (b, h)),
                  pl.BlockSpec((mem, hd), lambda b, h, i: (b, heads + h))],
        out_specs=pl.BlockSpec((tq, hd), lambda b, h, i: (b * nq + i, h)),
        compiler_params=_params(("parallel", "parallel", "parallel")),
        name="xattn_core",
    )(q, kv, kv)


def _in_proj_layout(d):
    mix = d // 4
    m_dk = mix // MLSTM_HEADS // 2
    g_dk = mix // GLA_HEADS // 2
    sizes = (MLSTM_HEADS * m_dk, MLSTM_HEADS * m_dk, mix, mix, 4 * MLSTM_HEADS,
             GLA_HEADS * g_dk, GLA_HEADS * g_dk, mix, mix, 2 * GLA_RANK, 3 * mix, N_BRANCH * d)
    offs = np.concatenate([[0], np.cumsum(sizes)])
    wide = (0, 1, 2, 3, 5, 6, 7, 8, 10, 11)
    narrow = (4, 9)
    new_off, pos = {}, 0
    for i in wide:
        new_off[i] = pos
        pos += sizes[i]
    return sizes, offs, wide, narrow, new_off


def kernel(x, mem, norm_gains, final_norm, w_in, b_in, mlstm_head_norm, gla_decay_up, gla_decay_bias, gla_head_norm, hyena_conv_w, hyena_conv_b, hyena_ffn_w1, hyena_ffn_b1, hyena_ffn_w2, hyena_ffn_b2, hyena_ffn_w3, hyena_freq, hyena_skip, w_branch, w_out, xattn_wq, xattn_wkv, xattn_wo, mlp_w1, mlp_w2):
    batch, seq, d = x.shape
    mem_tokens = mem.shape[1]
    depth = norm_gains.shape[0]
    mix = d // 4
    mh, gh = MLSTM_HEADS, GLA_HEADS
    m_dv, g_dv = mix // mh, mix // gh
    m_dk, g_dk = m_dv // 2, g_dv // 2
    nc = seq // SCAN_CHUNK

    sizes, offs, wide, narrow, new_off = _in_proj_layout(d)
    seg = lambda a, i: a[..., offs[i]:offs[i + 1]]
    w_in_rows = jnp.swapaxes(w_in, 1, 2)
    rows = lambda i: w_in_rows[:, offs[i]:offs[i + 1], :]
    w_main = jnp.concatenate([rows(i) for i in wide], axis=1).astype(BF16)
    b_main = jnp.concatenate([seg(b_in, i) for i in wide], axis=1)[:, None, :].astype(F32)
    n_small = sum(sizes[i] for i in narrow)
    w_small = jnp.pad(jnp.concatenate([rows(i) for i in narrow], axis=1),
                      ((0, 0), (0, LANES - n_small), (0, 0))).astype(BF16)
    b_small = jnp.pad(jnp.concatenate([seg(b_in, i) for i in narrow], axis=1),
                      ((0, 0), (0, LANES - n_small)))[:, None, :].astype(F32)
    wb16, wo16 = w_branch.astype(BF16), w_out.astype(BF16)
    wq16, wkv16, wxo16 = xattn_wq.astype(BF16), xattn_wkv.astype(BF16), xattn_wo.astype(BF16)
    w1_16, w2_16 = mlp_w1.astype(BF16), mlp_w2.astype(BF16)

    h = x.reshape(batch * seq, d)
    memf = mem.reshape(batch * mem_tokens, d)
    for l in range(depth):
        g = norm_gains[l]
        xn = _rmsnorm(h, g[0], BF16)
        proj = _mm(xn, w_main, l, bias=b_main, out_dtype=BF16, tn=1024, w_rows=True, name="in_proj")
        small = _mm(xn, w_small, l, bias=b_small, out_dtype=F32, w_rows=True, name="in_proj_gates")

        mg = small[:, :4 * mh].reshape(batch, nc, SCAN_CHUNK, 2, 2, mh)
        mgates = mg.transpose(0, 3, 1, 5, 4, 2).reshape(batch * 2, nc, 2 * mh, SCAN_CHUNK)
        h_m = _mlstm(proj, mgates, batch=batch, seq=seq, q_off=new_off[0], k_off=new_off[1],
                     v_off=new_off[2], heads=mh, dk=m_dk, dv=m_dv)
        y_a = _headnorm_gate(h_m, proj, new_off[3], mlstm_head_norm[l], heads=mh, dv=m_dv, act="sigmoid")

        up_pad = jnp.zeros((2, LANES, gh * g_dk), F32)
        for dr in range(2):
            r0 = 4 * mh + dr * GLA_RANK
            up_pad = up_pad.at[dr, r0:r0 + GLA_RANK, :].set(gla_decay_up[l, dr].astype(F32))
        gbias = gla_decay_bias[l].reshape(2, 1, gh * g_dk).astype(F32)
        h_g = _gla(proj, small, up_pad, gbias, batch=batch, seq=seq, q_off=new_off[5], k_off=new_off[6],
                   v_off=new_off[7], heads=gh, dk=g_dk, dv=g_dv)
        y_b = _headnorm_gate(h_g, proj, new_off[8], gla_head_norm[l], heads=gh, dv=g_dv, act="silu")

        y_c = _hyena_cm(proj, new_off[10], hyena_conv_w[l], hyena_conv_b[l], hyena_ffn_w1[l], hyena_ffn_b1[l],
                        hyena_ffn_w2[l], hyena_ffn_b2[l], hyena_ffn_w3[l], hyena_freq[l], hyena_skip[l],
                        batch=batch, seq=seq, ch=mix)

        merged = _merge((y_a, y_b, y_c), proj, new_off[11], wb16, l, d_model=d)
        h = _mm(merged, wo16, l, resid=h, out_dtype=F32, name="mixer_out")

        xn = _rmsnorm(h, g[1], BF16)
        memn = _rmsnorm(memf, g[2], BF16)
        q = _mm(xn, wq16, l, out_dtype=BF16, tn=1024, name="xattn_q")
        kv = _mm(memn, wkv16, l, out_dtype=BF16, tn=1024, name="xattn_kv")
        o = _xattn(q, kv, batch=batch, seq=seq, mem=mem_tokens, heads=XATTN_HEADS, hd=d // XATTN_HEADS)
        h = _mm(o, wxo16, l, resid=h, out_dtype=F32, name="xattn_out")

        xn = _rmsnorm(h, g[3], BF16)
        hid = _mm(xn, w1_16, l, act="relu2", out_dtype=BF16, tn=1024, name="mlp_up")
        h = _mm(hid, w2_16, l, resid=h, out_dtype=F32, tm=512, tn=256, name="mlp_down")
    return _rmsnorm(h, final_norm, x.dtype).reshape(batch, seq, d)
```

```python
import functools
import math

import numpy as np
import jax
import jax.numpy as jnp
from jax import lax
from jax.experimental import pallas as pl
from jax.experimental.pallas import tpu as pltpu

F32 = jnp.float32
BF16 = jnp.bfloat16

N_BRANCH = 3
MLSTM_HEADS = 4
GLA_HEADS = 4
GLA_RANK = 16
GLA_TAU = 16.0
HYENA_ORDER = 2
HYENA_BANDS = 16
HYENA_MIN_DECAY = math.log(1e-2) / 1.5
HYENA_MAX_DECAY = math.log(1e-2) / 0.3
XATTN_HEADS = 4
EPS = 1e-6

LANES = 128
BF16_TILE_ROWS = 16
VMEM_LIMIT_BYTES = 56 * 1024 * 1024

SCAN_CHUNK = 128
FFT_N2 = 128
NEG_BIG = -1e30
GATHER_UNROLL = 8
EXACT_PIECES = 2


def _params(sem):
    return pltpu.CompilerParams(dimension_semantics=sem, vmem_limit_bytes=VMEM_LIMIT_BYTES)


def _log_sigmoid(x):
    return -(jnp.maximum(-x, 0.0) + jnp.log(1.0 + jnp.exp(-jnp.abs(x))))


def _split_bf16(a, parts):
    out = []
    r = a
    for _ in range(parts):
        p = r.astype(BF16)
        out.append(p)
        r = r - p.astype(F32)
    return out


def _dot(a, b):
    return jnp.dot(a, b, preferred_element_type=F32)


def _dot_nt(a, b):
    return lax.dot_general(a, b, (((1,), (1,)), ((), ())), preferred_element_type=F32)


def _dot_tn(a, b):
    return lax.dot_general(a, b, (((0,), (0,)), ((), ())), preferred_element_type=F32)


def _dot3(a, b):
    a_hi, a_lo = _split_bf16(a, 2)
    b_hi, b_lo = _split_bf16(b, 2)
    return _dot(a_hi, b_hi) + _dot(a_hi, b_lo) + _dot(a_lo, b_hi)


def _rmsnorm_kernel(x_ref, g_ref, o_ref):
    x = x_ref[...].astype(F32)
    r = lax.rsqrt(jnp.mean(x * x, axis=-1, keepdims=True) + EPS)
    o_ref[...] = ((x * r) * g_ref[...]).astype(o_ref.dtype)


def _rmsnorm(x, g, out_dtype, tm=256):
    m, d = x.shape
    tm = min(tm, m)
    return pl.pallas_call(
        _rmsnorm_kernel,
        out_shape=jax.ShapeDtypeStruct((m, d), out_dtype),
        grid=(m // tm,),
        in_specs=[pl.BlockSpec((tm, d), lambda i: (i, 0)),
                  pl.BlockSpec((1, d), lambda i: (0, 0))],
        out_specs=pl.BlockSpec((tm, d), lambda i: (i, 0)),
        compiler_params=_params(("parallel",)),
        name="rmsnorm",
    )(x, g.reshape(1, d).astype(F32))


def _mm_kernel(*refs, nk, act, has_bias, has_resid, w_rows):
    x_ref, w_ref = refs[0], refs[1]
    product = (lambda a, b: _dot_nt(a, b[0])) if w_rows else _dot
    idx = 2
    b_ref = r_ref = None
    if has_bias:
        b_ref = refs[idx]
        idx += 1
    if has_resid:
        r_ref = refs[idx]
        idx += 1
    o_ref = refs[idx]

    def epilogue(acc):
        if has_bias:
            acc = acc + b_ref[...]
        if act == "relu2":
            acc = jnp.square(jnp.maximum(acc, 0.0))
        if has_resid:
            acc = acc + r_ref[...]
        o_ref[...] = acc.astype(o_ref.dtype)

    if nk == 1:
        epilogue(product(x_ref[...], w_ref[...]))
    else:
        acc_ref = refs[idx + 1]
        k = pl.program_id(2)

        @pl.when(k == 0)
        def _():
            acc_ref[...] = jnp.zeros_like(acc_ref)

        acc_ref[...] += product(x_ref[...], w_ref[...])

        @pl.when(k == nk - 1)
        def _():
            epilogue(acc_ref[...])


def _mm(x, w, layer, *, bias=None, resid=None, act=None, out_dtype=BF16, tm=1024, tn=512, tk=None,
        w_rows=None, name="mm"):
    m, kdim = x.shape
    n = w_rows[1] if w_rows else w.shape[2]
    tm, tn = min(tm, m), min(tn, n)
    tk = kdim if tk is None else min(tk, kdim)
    nk = kdim // tk
    assert m % tm == 0 and n % tn == 0 and kdim % tk == 0
    if w_rows:
        row0 = w_rows[0]
        assert row0 % BF16_TILE_ROWS == 0
        w_spec = pl.BlockSpec((pl.Element(1), pl.Element(tn), pl.Element(tk)),
                              lambda i, j, k: (layer, pl.multiple_of(row0 + j * tn, BF16_TILE_ROWS), k * tk))
    else:
        w_spec = pl.BlockSpec((None, tk, tn), lambda i, j, k: (layer, k, j))
    in_specs = [pl.BlockSpec((tm, tk), lambda i, j, k: (i, k)), w_spec]
    args = [x, w]
    if bias is not None:
        in_specs.append(pl.BlockSpec((None, 1, tn), lambda i, j, k: (layer, 0, j)))
        args.append(bias)
    if resid is not None:
        in_specs.append(pl.BlockSpec((tm, tn), lambda i, j, k: (i, j)))
        args.append(resid)
    scratch = [pltpu.VMEM((tm, tn), F32)] if nk > 1 else []
    return pl.pallas_call(
        functools.partial(_mm_kernel, nk=nk, act=act, has_bias=bias is not None,
                          has_resid=resid is not None, w_rows=w_rows is not None),
        out_shape=jax.ShapeDtypeStruct((m, n), out_dtype),
        grid=(m // tm, n // tn, nk),
        in_specs=in_specs,
        out_specs=pl.BlockSpec((tm, tn), lambda i, j, k: (i, j)),
        scratch_shapes=scratch,
        compiler_params=_params(("parallel", "parallel", "arbitrary")),
        name=name,
    )(*args)


def _rows_to_cols(sel, rows):
    return sum(_dot_nt(sel, piece) for piece in _split_bf16(rows, EXACT_PIECES))


def _mlstm_kernel(q_ref, k_ref, v_ref, g_ref, o_ref, c_ref, m_ref, *, nc, heads, dk, dv, scale):
    lc = SCAN_CHUNK
    rev = pl.program_id(0) % 2
    c = pl.program_id(1)
    cc = c + rev * (nc - 1 - 2 * c)

    @pl.when(c == 0)
    def _():
        c_ref[...] = jnp.zeros_like(c_ref)
        m_ref[...] = jnp.zeros_like(m_ref)

    row = lax.broadcasted_iota(jnp.int32, (lc, lc), 0)
    col = lax.broadcasted_iota(jnp.int32, (lc, lc), 1)
    tri = ((col - row) * (1 - 2 * rev)) <= 0
    tri_b = jnp.where(tri, 1.0, 0.0).astype(BF16)
    eye_b = jnp.where(row == col, 1.0, 0.0).astype(BF16)
    ones_blk = (lax.broadcasted_iota(jnp.int32, (lc, LANES), 1) == 0).astype(BF16)

    gl = g_ref[cc]
    lf_all = _log_sigmoid(gl)
    lf_pieces = _split_bf16(lf_all, EXACT_PIECES)
    b_rows = sum(_dot_nt(piece, tri_b) for piece in lf_pieces)
    b_cols = sum(_dot_nt(tri_b, piece) for piece in lf_pieces)
    g_all = jnp.sum(lf_all, axis=1, keepdims=True)
    wk_rows, m_news, decays = [], [], []
    for hh in range(heads):
        ig = gl[2 * hh:2 * hh + 1, :]
        g = g_all[2 * hh + 1:2 * hh + 2, :]
        m_prev = m_ref[hh]
        a_row = g - b_rows[2 * hh + 1:2 * hh + 2, :] + ig
        m_new = jnp.maximum(g + m_prev, jnp.max(a_row, axis=1, keepdims=True))
        wk_rows.append(jnp.exp(a_row - m_new))
        m_news.append(m_new)
        decays.append(jnp.exp(g + m_prev - m_new))
    wk_cols = _rows_to_cols(eye_b, jnp.concatenate(wk_rows + [jnp.zeros((heads, lc), F32)], axis=0))

    rng = range(heads)
    qs = [q_ref[:, hh * dk:(hh + 1) * dk] for hh in rng]
    ks = [k_ref[:, hh * dk:(hh + 1) * dk] for hh in rng]
    v_augs = [jnp.concatenate([v_ref[:, hh * dv:(hh + 1) * dv], ones_blk], axis=1) for hh in rng]
    qk = [_dot_nt(qs[hh], ks[hh]) for hh in rng]
    qc = [_dot(qs[hh], c_ref[hh].astype(BF16)) for hh in rng]
    kv = [_dot_tn((ks[hh].astype(F32) * wk_cols[:, hh:hh + 1]).astype(BF16), v_augs[hh]) for hh in rng]
    for hh in rng:
        b_col = b_cols[:, 2 * hh + 1:2 * hh + 2]
        log_d = jnp.where(tri, b_col - b_rows[2 * hh + 1:2 * hh + 2, :] + gl[2 * hh:2 * hh + 1, :], NEG_BIG)
        m_inter = b_col + m_ref[hh]
        m_j = jnp.maximum(m_inter, jnp.max(log_d, axis=1, keepdims=True))
        smat = (qk[hh] * jnp.exp(log_d - m_j)).astype(BF16)
        num_aug = (_dot(smat, v_augs[hh]) + jnp.exp(m_inter - m_j) * qc[hh]) * scale
        num = num_aug[:, :dv]
        den = num_aug[:, dv:dv + 1]
        o_ref[:, hh * dv:(hh + 1) * dv] = (num / jnp.maximum(jnp.abs(den), jnp.exp(-m_j))).astype(o_ref.dtype)
        c_ref[hh] = decays[hh] * c_ref[hh] + kv[hh]
        m_ref[hh] = m_news[hh]


def _mlstm(proj, gates, *, batch, seq, q_off, k_off, v_off, heads, dk, dv):
    lc = SCAN_CHUNK
    nc = seq // lc
    assert dk == lc and seq % lc == 0 and dv % LANES == 0

    def rowblk(s, c):
        return (s // 2) * nc + c + (s % 2) * (nc - 1 - 2 * c)

    wqk, wv = heads * dk, heads * dv
    return pl.pallas_call(
        functools.partial(_mlstm_kernel, nc=nc, heads=heads, dk=dk, dv=dv, scale=dk ** -0.5),
        out_shape=jax.ShapeDtypeStruct((2, batch * seq, wv), F32),
        grid=(batch * 2, nc),
        in_specs=[
            pl.BlockSpec((lc, wqk), lambda s, c: (rowblk(s, c), q_off // wqk)),
            pl.BlockSpec((lc, wqk), lambda s, c: (rowblk(s, c), k_off // wqk)),
            pl.BlockSpec((lc, wv), lambda s, c: (rowblk(s, c), v_off // wv)),
            pl.BlockSpec((None, nc, 2 * heads, lc), lambda s, c: (s, 0, 0, 0)),
        ],
        out_specs=pl.BlockSpec((None, lc, wv), lambda s, c: (s % 2, rowblk(s, c), 0)),
        scratch_shapes=[pltpu.VMEM((heads, dk, dv + LANES), F32), pltpu.VMEM((heads, 1, 1), F32)],
        compiler_params=_params(("parallel", "arbitrary")),
        name="mlstm_scan",
    )(proj, proj, proj, gates)


def _gla_levels():
    lc = SCAN_CHUNK
    nlev = int(math.log2(lc)) + 1
    t = np.arange(lc)
    diff = np.zeros((2, nlev * lc, lc), np.float32)
    mask = np.zeros((2, nlev, lc, lc), np.float32)
    u = t[None, :]
    for d in range(2):
        cum = (u <= t[:, None]) if d == 0 else (u >= t[:, None])
        diff[d, :lc] = cum
        mask[d, 0] = np.eye(lc)
        for l in range(1, nlev):
            w = 1 << (l - 1)
            pair = t // (2 * w)
            second = (t % (2 * w)) >= w
            if d == 0:
                bd = pair * 2 * w + w - 1
                cum_bd = u <= bd[:, None]
                is_q, is_k = second, ~second
            else:
                bd = pair * 2 * w + w
                cum_bd = u >= bd[:, None]
                is_q, is_k = ~second, second
            diff[d, l * lc:(l + 1) * lc] = cum.astype(np.float32) - cum_bd.astype(np.float32)
            mask[d, l] = ((pair[:, None] == pair[None, :]) & is_q[:, None] & is_k[None, :])
    return diff, mask


def _gla_kernel(q_ref, k_ref, v_ref, sm_ref, up_ref, bias_ref, d_ref, mask_ref, o_ref, s_ref, *,
                heads, dk, dv, nlev, scale):
    lc = SCAN_CHUNK
    c = pl.program_id(1)

    @pl.when(c == 0)
    def _():
        s_ref[...] = jnp.zeros_like(s_ref)

    z = _dot3(sm_ref[...], up_ref[...]) + bias_ref[...]
    la = _log_sigmoid(z) * (1.0 / GLA_TAU)
    dmat = d_ref[...]
    x = sum(_dot(dmat, piece) for piece in _split_bf16(la, EXACT_PIECES))
    b = x[:lc]
    g_row = jnp.sum(la, axis=0, keepdims=True)

    qb = q_ref[...]
    kb = k_ref[...]
    q = qb.astype(F32)
    k = kb.astype(F32)
    hs = [slice(hh * dk, (hh + 1) * dk) for hh in range(heads)]
    atts = [mask_ref[0] * _dot_nt(qb[:, s], kb[:, s]) for s in hs]
    for l in range(1, nlev):
        e = jnp.exp(-jnp.abs(x[l * lc:(l + 1) * lc]))
        qe = (q * e).astype(BF16)
        ke = (k * e).astype(BF16)
        m = mask_ref[l]
        atts = [att + m * _dot_nt(qe[:, s], ke[:, s]) for att, s in zip(atts, hs)]

    qs = (q * jnp.exp(b)).astype(BF16)
    kd = (k * jnp.exp(g_row - b)).astype(BF16)
    eg = jnp.exp(g_row)
    for hh, s in enumerate(hs):
        v = v_ref[:, hh * dv:(hh + 1) * dv]
        o = _dot(atts[hh].astype(BF16), v) + _dot_nt(qs[:, s], s_ref[hh].astype(BF16))
        o_ref[:, hh * dv:(hh + 1) * dv] = (o * scale).astype(o_ref.dtype)
        s_ref[hh] = eg[:, s] * s_ref[hh] + _dot_tn(v, kd[:, s])


def _gla(proj, small, up_pad, bias, *, batch, seq, q_off, k_off, v_off, heads, dk, dv):
    lc = SCAN_CHUNK
    nc = seq // lc
    assert dk == lc and seq % lc == 0
    diff, mask = _gla_levels()
    nlev = mask.shape[1]

    def rowblk(s, c):
        return (s // 2) * nc + c + (s % 2) * (nc - 1 - 2 * c)

    wqk, wv = heads * dk, heads * dv
    return pl.pallas_call(
        functools.partial(_gla_kernel, heads=heads, dk=dk, dv=dv, nlev=nlev, scale=dk ** -0.5),
        out_shape=jax.ShapeDtypeStruct((2, batch * seq, wv), F32),
        grid=(batch * 2, nc),
        in_specs=[
            pl.BlockSpec((lc, wqk), lambda s, c: (rowblk(s, c), q_off // wqk)),
            pl.BlockSpec((lc, wqk), lambda s, c: (rowblk(s, c), k_off // wqk)),
            pl.BlockSpec((lc, wv), lambda s, c: (rowblk(s, c), v_off // wv)),
            pl.BlockSpec((lc, LANES), lambda s, c: (rowblk(s, c), 0)),
            pl.BlockSpec((None, LANES, wqk), lambda s, c: (s % 2, 0, 0)),
            pl.BlockSpec((None, 1, wqk), lambda s, c: (s % 2, 0, 0)),
            pl.BlockSpec((None, nlev * lc, lc), lambda s, c: (s % 2, 0, 0)),
            pl.BlockSpec((None, nlev, lc, lc), lambda s, c: (s % 2, 0, 0, 0)),
        ],
        out_specs=pl.BlockSpec((None, lc, wv), lambda s, c: (s % 2, rowblk(s, c), 0)),
        scratch_shapes=[pltpu.VMEM((heads, dv, dk), F32)],
        compiler_params=_params(("parallel", "arbitrary")),
        name="gla_scan",
    )(proj, proj, proj, small, up_pad, bias, jnp.asarray(diff, BF16), jnp.asarray(mask, F32))


def _headnorm_gate_kernel(h_ref, g_ref, hn_ref, o_ref, *, heads, dv, act):
    h = h_ref[0] + h_ref[1]
    outs = []
    for hh in range(heads):
        blk = h[:, hh * dv:(hh + 1) * dv]
        r = lax.rsqrt(jnp.mean(blk * blk, axis=-1, keepdims=True) + EPS)
        outs.append(blk * r)
    hn = jnp.concatenate(outs, axis=1) * hn_ref[...]
    gate = g_ref[...].astype(F32)
    sig = 1.0 / (1.0 + jnp.exp(-gate))
    gate = sig if act == "sigmoid" else gate * sig
    o_ref[...] = (hn * gate).astype(o_ref.dtype)


def _headnorm_gate(h2, proj, gate_off, head_norm, *, heads, dv, act, tm=512):
    _, m, width = h2.shape
    tm = min(tm, m)
    return pl.pallas_call(
        functools.partial(_headnorm_gate_kernel, heads=heads, dv=dv, act=act),
        out_shape=jax.ShapeDtypeStruct((m, width), BF16),
        grid=(m // tm,),
        in_specs=[pl.BlockSpec((2, tm, width), lambda i: (0, i, 0)),
                  pl.BlockSpec((tm, width), lambda i: (i, gate_off // width)),
                  pl.BlockSpec((1, width), lambda i: (0, 0))],
        out_specs=pl.BlockSpec((tm, width), lambda i: (i, 0)),
        compiler_params=_params(("parallel",)),
        name="headnorm_gate",
    )(h2, proj, head_norm.reshape(1, width).astype(F32))


def _cm_short_conv_kernel(x0_ref, x1_ref, x2_ref, w_ref, b_ref, o0_ref, o1_ref, o2_ref, s_ref, t_ref):
    seq, tc = x0_ref.shape
    half, n2 = o0_ref.shape[1], o0_ref.shape[2]
    for p, (x_ref, o_ref) in enumerate(((x0_ref, o0_ref), (x1_ref, o1_ref), (x2_ref, o2_ref))):
        x = x_ref[...].astype(F32)
        w = w_ref[p]
        bias = b_ref[p]
        s_ref[...] = (pltpu.roll(x, 1, axis=0) * w[0:1] + x * w[1:2]
                      + pltpu.roll(x, seq - 1, axis=0) * w[2:3] + bias)
        s_ref[0:1, :] = x[0:1] * w[1:2] + x[1:2] * w[2:3] + bias
        s_ref[seq - 1:seq, :] = x[seq - 2:seq - 1] * w[0:1] + x[seq - 1:seq] * w[1:2] + bias

        def flip(a, carry):
            slab = s_ref[pl.ds(pl.multiple_of(a * n2, n2), n2), :]
            t_ref[pl.ds(pl.multiple_of(a * tc, tc), tc), :] = slab.T
            return carry

        lax.fori_loop(0, half, flip, 0, unroll=min(4, half))

        def body(ch, carry, o_ref=o_ref):
            o_ref[ch] = t_ref[pl.ds(ch, half, stride=tc), :].astype(o_ref.dtype)
            return carry

        lax.fori_loop(0, tc, body, 0, unroll=GATHER_UNROLL)


def _cm_short_conv(proj, conv_w, conv_b, *, batch, seq, off, ch, n2):
    tc = LANES
    nct = ch // tc
    half = seq // n2
    w = conv_w.reshape(3, 3, nct, tc).transpose(1, 2, 0, 3).astype(F32)
    b = conv_b.reshape(3, nct, 1, tc).astype(F32)
    in_specs = [pl.BlockSpec((seq, tc), functools.partial(lambda b_, j, p: (b_, off // tc + p * nct + j), p=p))
                for p in range(3)]
    in_specs += [pl.BlockSpec((3, None, 3, tc), lambda b_, j: (0, j, 0, 0)),
                 pl.BlockSpec((3, None, 1, tc), lambda b_, j: (0, j, 0, 0))]
    out = jax.ShapeDtypeStruct((batch, ch, half, n2), BF16)
    return pl.pallas_call(
        _cm_short_conv_kernel,
        out_shape=[out, out, out],
        grid=(batch, nct),
        in_specs=in_specs,
        out_specs=[pl.BlockSpec((None, tc, half, n2), lambda b_, j: (b_, j, 0, 0))] * 3,
        scratch_shapes=[pltpu.VMEM((seq, tc), F32), pltpu.VMEM((half * tc, n2), F32)],
        compiler_params=_params(("parallel", "parallel")),
        name="hyena_short_conv",
    )(proj, proj, proj, w, b)


def _cm_taps_kernel(z_ref, w1_ref, b1_ref, w2_ref, b2_ref, w3_ref, f_ref, dl_ref, o_ref, t_ref, h_ref, *, emb, n2):
    @pl.when(pl.program_id(1) == 0)
    def _():
        f = f_ref[...]
        h = jnp.sin(f * (_dot3(w1_ref[...], z_ref[...]) + b1_ref[...]))
        h_ref[...] = jnp.sin(f * (_dot3(w2_ref[...], h) + b2_ref[...]))

    win = jnp.exp(-dl_ref[...] * z_ref[0:1, :]) * z_ref[emb:emb + 1, :]
    out = _dot3(w3_ref[...], h_ref[...]) * win
    gc = out.shape[0]
    tile = out.shape[1] // n2
    for s in range(tile):
        t_ref[s * gc:(s + 1) * gc, :] = out[:, s * n2:(s + 1) * n2]

    def body(ch, carry):
        o_ref[ch] = t_ref[pl.ds(ch, tile, stride=gc), :].astype(o_ref.dtype)
        return carry

    lax.fori_loop(0, gc, body, 0, unroll=GATHER_UNROLL)


def _cm_taps(seq, w1, b1, w2, b2, w3, freq, ch, n2, gc=256):
    emb, hid = w1.shape
    n1 = 2 * seq // n2
    half = n1 // 2
    tile = min(BF16_TILE_ROWS, half)
    order = w3.shape[1] // (2 * ch)
    lag = jnp.arange(2 * seq, dtype=jnp.int32)
    pos = jnp.where(lag < seq, lag, 2 * seq - lag)
    n = pos.astype(F32)
    t = n / (seq - 1)
    bands = jnp.linspace(1e-4, HYENA_BANDS - 1, HYENA_BANDS, dtype=F32)
    ang = (2.0 * math.pi * n / seq)[:, None] * bands[None, :]
    valid = (lag != seq).astype(F32)
    z = jnp.concatenate([t[:, None], jnp.cos(ang), -jnp.sin(ang), valid[:, None]], axis=-1)
    zt = jnp.pad(z, ((0, 0), (0, LANES - emb - 1))).T
    deltas = jnp.abs(jnp.linspace(HYENA_MIN_DECAY, HYENA_MAX_DECAY, ch, dtype=F32))
    dl = jnp.tile(deltas, order).reshape(order * ch, 1)
    pad_h = LANES - hid
    w1t = jnp.pad(w1.astype(F32), ((0, LANES - emb), (0, pad_h))).T
    w2t = jnp.pad(w2.astype(F32), ((0, pad_h), (0, pad_h))).T
    w3t = jnp.pad(w3.astype(F32), ((0, pad_h), (0, 0))).T.reshape(order, 2, ch, LANES)
    w3t = w3t.transpose(1, 0, 2, 3).reshape(2, order * ch, LANES)
    colv = lambda v: jnp.pad(v.astype(F32), (0, pad_h)).reshape(LANES, 1)
    gc = min(gc, order * ch)
    nat = n1 // tile
    full = lambda shape: pl.BlockSpec(shape, lambda a, j: (0, 0))
    return pl.pallas_call(
        functools.partial(_cm_taps_kernel, emb=emb, n2=n2),
        out_shape=jax.ShapeDtypeStruct((order * ch, n1, n2), BF16),
        grid=(nat, order * ch // gc),
        in_specs=[pl.BlockSpec((LANES, tile * n2), lambda a, j: (0, a)),
                  full((LANES, LANES)), full((LANES, 1)), full((LANES, LANES)), full((LANES, 1)),
                  pl.BlockSpec((None, gc, LANES), lambda a, j: (a // (nat // 2), j, 0)),
                  full((LANES, 1)),
                  pl.BlockSpec((gc, 1), lambda a, j: (j, 0))],
        out_specs=pl.BlockSpec((gc, tile, n2), lambda a, j: (j, a, 0)),
        scratch_shapes=[pltpu.VMEM((tile * gc, n2), F32), pltpu.VMEM((LANES, tile * n2), F32)],
        compiler_params=_params(("parallel", "arbitrary")),
        name="hyena_filters",
    )(zt, w1t, colv(b1), w2t, colv(b2), w3t, colv(freq), dl)


def _cm_tables(n1, n2):
    a = jnp.arange(n1, dtype=jnp.int32)
    ang1 = ((a[:, None] * a[None, :]) % n1).astype(F32) * (2.0 * math.pi / n1)
    f1r, f1i = jnp.cos(ang1), -jnp.sin(ang1)
    b = jnp.arange(n2, dtype=jnp.int32)
    angt = (a[:, None] * b[None, :]).astype(F32) * (2.0 * math.pi / (n1 * n2))
    tw = jnp.stack([jnp.cos(angt), -jnp.sin(angt)])
    ang2 = ((b[:, None] * b[None, :]) % n2).astype(F32) * (2.0 * math.pi / n2)
    wr, wi = jnp.cos(ang2), -jnp.sin(ang2)
    wb = jnp.concatenate([jnp.concatenate([wr, wi], axis=1),
                          jnp.concatenate([-wi, wr], axis=1)], axis=0).astype(BF16)
    return f1r, f1i, tw, wb, wb.T


def _cm_forward(x_ref, f1, twr, twi, wb, n1):
    ys = []
    for g in range(x_ref.shape[0]):
        p = _dot(f1, x_ref[g])
        pr, pi = p[:n1], p[n1:]
        ys.append(jnp.concatenate([pr * twr - pi * twi, pr * twi + pi * twr], axis=1))
    y = jnp.concatenate(ys, axis=0).astype(BF16)
    return _dot(y, wb)


def _cm_spec_kernel(t_ref, f1_ref, tw_ref, wb_ref, o_ref, *, n1, inv_n):
    z = _cm_forward(t_ref, f1_ref[...], tw_ref[0], tw_ref[1], wb_ref[...], n1)
    o_ref[...] = (z * inv_n).reshape(o_ref.shape)


def _cm_spectrum(taps, f1, tw, wb, *, group=16):
    nch, n1, n2 = taps.shape
    group = min(group, nch)
    full2 = lambda shape: pl.BlockSpec(shape, lambda j: (0,) * len(shape))
    return pl.pallas_call(
        functools.partial(_cm_spec_kernel, n1=n1, inv_n=1.0 / (n1 * n2)),
        out_shape=jax.ShapeDtypeStruct((nch, n1, 2 * n2), F32),
        grid=(nch // group,),
        in_specs=[pl.BlockSpec((group, n1, n2), lambda j: (j, 0, 0)),
                  full2((2 * n1, n1)), full2((2, n1, n2)), full2((2 * n2, 2 * n2))],
        out_specs=pl.BlockSpec((group, n1, 2 * n2), lambda j: (j, 0, 0)),
        compiler_params=_params(("parallel",)),
        name="hyena_filter_spectrum",
    )(taps, f1, tw, wb)


def _cm_conv_kernel(x_ref, gate_ref, kf_ref, skip_ref, f1_ref, f1h_ref, tw_ref, wb_ref, wbt_ref, o_ref, *, n1, n2):
    group = x_ref.shape[0]
    twr, twi = tw_ref[0], tw_ref[1]
    z = _cm_forward(x_ref, f1_ref[...], twr, twi, wb_ref[...], n1)
    kf = kf_ref[...].reshape(group * n1, 2 * n2)
    zr, zi, kr, ki = z[:, :n2], z[:, n2:], kf[:, :n2], kf[:, n2:]
    s = jnp.concatenate([zr * kr - zi * ki, zr * ki + zi * kr], axis=1).astype(BF16)
    v = _dot(s, wbt_ref[...])
    f1h = f1h_ref[...]
    for g in range(group):
        vr, vi = v[g * n1:(g + 1) * n1, :n2], v[g * n1:(g + 1) * n1, n2:]
        u = jnp.concatenate([vr * twr + vi * twi, vi * twr - vr * twi], axis=0).astype(BF16)
        y = _dot(f1h, u)
        y = y + skip_ref[g] * x_ref[g].astype(F32)
        o_ref[g] = (gate_ref[g].astype(F32) * y).astype(o_ref.dtype)


def _cm_conv(x, gate, kf, skip, f1d, f1h, tw, wb, wbt, *, kf_off, out_dtype, group=16):
    batch, ch, half, n2 = x.shape
    n1 = 2 * half
    group = min(group, ch)
    sig = pl.BlockSpec((None, group, half, n2), lambda j, b: (b, j, 0, 0))
    full2 = lambda shape: pl.BlockSpec(shape, lambda j, b: (0,) * len(shape))
    skip_b = jnp.broadcast_to(skip.astype(F32)[:, None, None], (ch, 1, n2))
    return pl.pallas_call(
        functools.partial(_cm_conv_kernel, n1=n1, n2=n2),
        out_shape=jax.ShapeDtypeStruct((batch, ch, half, n2), out_dtype),
        grid=(ch // group, batch),
        in_specs=[sig, sig,
                  pl.BlockSpec((group, n1, 2 * n2), lambda j, b: (kf_off // group + j, 0, 0)),
                  pl.BlockSpec((group, 1, n2), lambda j, b: (j, 0, 0)),
                  full2((2 * n1, half)), full2((half, 2 * n1)), full2((2, n1, n2)),
                  full2((2 * n2, 2 * n2)), full2((2 * n2, 2 * n2))],
        out_specs=sig,
        compiler_params=_params(("parallel", "parallel")),
        name="hyena_fft_conv",
    )(x, gate, kf, skip_b, f1d, f1h, tw, wb, wbt)


def _cm_to_time_major_kernel(y_ref, o_ref, *, half):
    tc = o_ref.shape[1]
    n2 = y_ref.shape[1]

    def body(a, carry):
        slab = y_ref[pl.ds(a, tc, stride=half), :]
        o_ref[pl.ds(pl.multiple_of(a * n2, n2), n2), :] = slab.T.astype(o_ref.dtype)
        return carry

    lax.fori_loop(0, half, body, 0, unroll=min(4, half))


def _cm_to_time_major(y, *, batch, seq, ch):
    half, n2 = y.shape[2], y.shape[3]
    tc = LANES
    return pl.pallas_call(
        functools.partial(_cm_to_time_major_kernel, half=half),
        out_shape=jax.ShapeDtypeStruct((batch * seq, ch), BF16),
        grid=(batch, ch // tc),
        in_specs=[pl.BlockSpec((None, tc * half, n2), lambda b, j: (b, j, 0))],
        out_specs=pl.BlockSpec((seq, tc), lambda b, j: (b, j)),
        compiler_params=_params(("parallel", "parallel")),
        name="hyena_to_time_major",
    )(y.reshape(batch, ch * half, n2))


def _hyena_cm(proj, hy_off, conv_w, conv_b, w1, b1, w2, b2, w3, freq, skip, *, batch, seq, ch):
    n2 = FFT_N2
    n1 = 2 * seq // n2
    half = n1 // 2
    v, x1, x2 = _cm_short_conv(proj, conv_w, conv_b, batch=batch, seq=seq, off=hy_off, ch=ch, n2=n2)
    taps = _cm_taps(seq, w1, b1, w2, b2, w3, freq, ch, n2)
    f1r, f1i, tw, wb, wbt = _cm_tables(n1, n2)
    fstack = jnp.concatenate([f1r, f1i], axis=0)
    kf = _cm_spectrum(taps, fstack.astype(BF16), tw, wb)
    f1d = fstack[:, :half].astype(BF16)
    f1h = jnp.concatenate([f1r[:half], f1i[:half]], axis=1).astype(BF16)
    zcur = v
    for o, gate in enumerate((x1, x2)):
        last = o == HYENA_ORDER - 1
        zcur = _cm_conv(zcur, gate, kf, skip[o], f1d, f1h, tw, wb, wbt, kf_off=o * ch,
                        out_dtype=F32 if last else BF16)
    return _cm_to_time_major(zcur, batch=batch, seq=seq, ch=ch)


def _merge_kernel(ya_ref, yb_ref, yc_ref, ga_ref, gb_ref, gc_ref, w_ref, o_ref):
    acc = None
    for n, (y_ref, g_ref) in enumerate(((ya_ref, ga_ref), (yb_ref, gb_ref), (yc_ref, gc_ref))):
        gate = 1.0 / (1.0 + jnp.exp(-g_ref[...].astype(F32)))
        term = gate * _dot(y_ref[...], w_ref[n])
        acc = term if acc is None else acc + term
    o_ref[...] = acc.astype(o_ref.dtype)


def _merge(ys, proj, gate_off, w_branch, layer, *, d_model, tm=1024, tn=512):
    m, width = ys[0].shape
    tm, tn = min(tm, m), min(tn, d_model)
    y_spec = pl.BlockSpec((tm, width), lambda i, j: (i, 0))
    g_specs = [pl.BlockSpec((tm, tn), functools.partial(
        lambda i, j, n: (i, (gate_off + n * d_model) // tn + j), n=n)) for n in range(N_BRANCH)]
    return pl.pallas_call(
        _merge_kernel,
        out_shape=jax.ShapeDtypeStruct((m, d_model), BF16),
        grid=(m // tm, d_model // tn),
        in_specs=[y_spec, y_spec, y_spec] + g_specs +
                 [pl.BlockSpec((None, N_BRANCH, width, tn), lambda i, j: (layer, 0, 0, j))],
        out_specs=pl.BlockSpec((tm, tn), lambda i, j: (i, j)),
        compiler_params=_params(("parallel", "parallel")),
        name="branch_merge",
    )(*ys, proj, proj, proj, w_branch)


def _xattn_kernel(q_ref, k_ref, v_ref, o_ref, *, scale):
    s = _dot_nt(q_ref[...], k_ref[...]) * scale
    s = s - jnp.max(s, axis=-1, keepdims=True)
    p = jnp.exp(s)
    l = jnp.sum(p, axis=-1, keepdims=True)
    o_ref[...] = (_dot(p.astype(BF16), v_ref[...]) / l).astype(o_ref.dtype)


def _xattn(q, kv, *, batch, seq, mem, heads, hd, tq=1024):
    tq = min(tq, seq)
    nq = seq // tq
    return pl.pallas_call(
        functools.partial(_xattn_kernel, scale=hd ** -0.5),
        out_shape=jax.ShapeDtypeStruct(q.shape, BF16),
        grid=(batch, heads, nq),
        in_specs=[pl.BlockSpec((tq, hd), lambda b, h, i: (b * nq + i, h)),
                  pl.BlockSpec((mem, hd), lambda b, h, i: (b, h)),
                  pl.BlockSpec((mem, hd), lambda b, h, i: (b, heads + h))],
        out_specs=pl.BlockSpec((tq, hd), lambda b, h, i: (b * nq + i, h)),
        compiler_params=_params(("parallel", "parallel", "parallel")),
        name="xattn_core",
    )(q, kv, kv)


def _in_proj_layout(d):
    mix = d // 4
    m_dk = mix // MLSTM_HEADS // 2
    g_dk = mix // GLA_HEADS // 2
    sizes = (MLSTM_HEADS * m_dk, MLSTM_HEADS * m_dk, mix, mix, 4 * MLSTM_HEADS,
             GLA_HEADS * g_dk, GLA_HEADS * g_dk, mix, mix, 2 * GLA_RANK, 3 * mix, N_BRANCH * d)
    offs = [int(o) for o in np.concatenate([[0], np.cumsum(sizes)])]
    runs = ((0, 1, 2, 3), (5, 6, 7, 8), (10, 11))
    narrow = (4, 9)
    where = {i: (r, offs[i] - offs[run[0]]) for r, run in enumerate(runs) for i in run}
    return sizes, offs, runs, narrow, where


def kernel(x, mem, norm_gains, final_norm, w_in, b_in, mlstm_head_norm, gla_decay_up, gla_decay_bias, gla_head_norm, hyena_conv_w, hyena_conv_b, hyena_ffn_w1, hyena_ffn_b1, hyena_ffn_w2, hyena_ffn_b2, hyena_ffn_w3, hyena_freq, hyena_skip, w_branch, w_out, xattn_wq, xattn_wkv, xattn_wo, mlp_w1, mlp_w2):
    batch, seq, d = x.shape
    mem_tokens = mem.shape[1]
    depth = norm_gains.shape[0]
    mix = d // 4
    mh, gh = MLSTM_HEADS, GLA_HEADS
    m_dv, g_dv = mix // mh, mix // gh
    m_dk, g_dk = m_dv // 2, g_dv // 2
    nc = seq // SCAN_CHUNK

    sizes, offs, runs, narrow, where = _in_proj_layout(d)
    seg = lambda a, i: a[..., offs[i]:offs[i + 1]]
    w_in_rows = jnp.swapaxes(w_in, 1, 2).astype(BF16)
    run_rows = [(offs[run[0]], sum(sizes[i] for i in run)) for run in runs]
    b_runs = [b_in[:, r0:r0 + n][:, None, :].astype(F32) for r0, n in run_rows]
    n_small = sum(sizes[i] for i in narrow)
    w_small = jnp.pad(jnp.concatenate([w_in_rows[:, offs[i]:offs[i + 1], :] for i in narrow], axis=1),
                      ((0, 0), (0, LANES - n_small), (0, 0)))
    b_small = jnp.pad(jnp.concatenate([seg(b_in, i) for i in narrow], axis=1),
                      ((0, 0), (0, LANES - n_small)))[:, None, :].astype(F32)
    col = lambda i: where[i][1]
    wb16, wo16 = w_branch.astype(BF16), w_out.astype(BF16)
    wq16, wkv16, wxo16 = xattn_wq.astype(BF16), xattn_wkv.astype(BF16), xattn_wo.astype(BF16)
    w1_16, w2_16 = mlp_w1.astype(BF16), mlp_w2.astype(BF16)

    h = x.reshape(batch * seq, d)
    memf = mem.reshape(batch * mem_tokens, d)
    for l in range(depth):
        g = norm_gains[l]
        xn = _rmsnorm(h, g[0], BF16)
        proj_m, proj_g, proj_h = [
            _mm(xn, w_in_rows, l, bias=b_run, out_dtype=BF16, tn=1024, w_rows=rw, name="in_proj")
            for rw, b_run in zip(run_rows, b_runs)]
        small = _mm(xn, w_small, l, bias=b_small, out_dtype=F32, w_rows=(0, LANES), name="in_proj_gates")

        mg = small[:, :4 * mh].reshape(batch, nc, SCAN_CHUNK, 2, 2, mh)
        mgates = mg.transpose(0, 3, 1, 5, 4, 2).reshape(batch * 2, nc, 2 * mh, SCAN_CHUNK)
        h_m = _mlstm(proj_m, mgates, batch=batch, seq=seq, q_off=col(0), k_off=col(1), v_off=col(2),
                     heads=mh, dk=m_dk, dv=m_dv)
        y_a = _headnorm_gate(h_m, proj_m, col(3), mlstm_head_norm[l], heads=mh, dv=m_dv, act="sigmoid")

        up_pad = jnp.zeros((2, LANES, gh * g_dk), F32)
        for dr in range(2):
            r0 = 4 * mh + dr * GLA_RANK
            up_pad = up_pad.at[dr, r0:r0 + GLA_RANK, :].set(gla_decay_up[l, dr].astype(F32))
        gbias = gla_decay_bias[l].reshape(2, 1, gh * g_dk).astype(F32)
        h_g = _gla(proj_g, small, up_pad, gbias, batch=batch, seq=seq, q_off=col(5), k_off=col(6),
                   v_off=col(7), heads=gh, dk=g_dk, dv=g_dv)
        y_b = _headnorm_gate(h_g, proj_g, col(8), gla_head_norm[l], heads=gh, dv=g_dv, act="silu")

        y_c = _hyena_cm(proj_h, col(10), hyena_conv_w[l], hyena_conv_b[l], hyena_ffn_w1[l], hyena_ffn_b1[l],
                        hyena_ffn_w2[l], hyena_ffn_b2[l], hyena_ffn_w3[l], hyena_freq[l], hyena_skip[l],
                        batch=batch, seq=seq, ch=mix)

        merged = _merge((y_a, y_b, y_c), proj_h, col(11), wb16, l, d_model=d)
        h = _mm(merged, wo16, l, resid=h, out_dtype=F32, name="mixer_out")

        xn = _rmsnorm(h, g[1], BF16)
        memn = _rmsnorm(memf, g[2], BF16)
        q = _mm(xn, wq16, l, out_dtype=BF16, tn=1024, name="xattn_q")
        kv = _mm(memn, wkv16, l, out_dtype=BF16, tn=1024, name="xattn_kv")
        o = _xattn(q, kv, batch=batch, seq=seq, mem=mem_tokens, heads=XATTN_HEADS, hd=d // XATTN_HEADS)
        h = _mm(o, wxo16, l, resid=h, out_dtype=F32, name="xattn_out")

        xn = _rmsnorm(h, g[3], BF16)
        hid = _mm(xn, w1_16, l, act="relu2", out_dtype=BF16, tn=1024, name="mlp_up")
        h = _mm(hid, w2_16, l, resid=h, out_dtype=F32, tm=512, tn=256, name="mlp_down")
    return _rmsnorm(h, final_norm, x.dtype).reshape(batch, seq, d)
```

```python
import functools
import math

import numpy as np
import jax
import jax.numpy as jnp
from jax import lax
from jax.experimental import pallas as pl
from jax.experimental.pallas import tpu as pltpu

F32 = jnp.float32
BF16 = jnp.bfloat16

N_BRANCH = 3
MLSTM_HEADS = 4
GLA_HEADS = 4
GLA_RANK = 16
GLA_TAU = 16.0
HYENA_ORDER = 2
HYENA_BANDS = 16
HYENA_MIN_DECAY = math.log(1e-2) / 1.5
HYENA_MAX_DECAY = math.log(1e-2) / 0.3
XATTN_HEADS = 4
EPS = 1e-6

LANES = 128
BF16_TILE_ROWS = 16
VMEM_LIMIT_BYTES = 56 * 1024 * 1024

SCAN_CHUNK = 128
FFT_N2 = 128
NEG_BIG = -1e30
GATHER_UNROLL = 8
EXACT_PIECES = 2


def _params(sem):
    return pltpu.CompilerParams(dimension_semantics=sem, vmem_limit_bytes=VMEM_LIMIT_BYTES)


def _log_sigmoid(x):
    return -(jnp.maximum(-x, 0.0) + jnp.log(1.0 + jnp.exp(-jnp.abs(x))))


def _split_bf16(a, parts):
    out = []
    r = a
    for _ in range(parts):
        p = r.astype(BF16)
        out.append(p)
        r = r - p.astype(F32)
    return out


def _dot(a, b):
    return jnp.dot(a, b, preferred_element_type=F32)


def _dot_nt(a, b):
    return lax.dot_general(a, b, (((1,), (1,)), ((), ())), preferred_element_type=F32)


def _dot_tn(a, b):
    return lax.dot_general(a, b, (((0,), (0,)), ((), ())), preferred_element_type=F32)


def _dot3(a, b):
    a_hi, a_lo = _split_bf16(a, 2)
    b_hi, b_lo = _split_bf16(b, 2)
    return _dot(a_hi, b_hi) + _dot(a_hi, b_lo) + _dot(a_lo, b_hi)


def _rmsnorm_kernel(x_ref, g_ref, o_ref):
    x = x_ref[...].astype(F32)
    r = lax.rsqrt(jnp.mean(x * x, axis=-1, keepdims=True) + EPS)
    o_ref[...] = ((x * r) * g_ref[...]).astype(o_ref.dtype)


def _rmsnorm(x, g, out_dtype, tm=256):
    m, d = x.shape
    tm = min(tm, m)
    return pl.pallas_call(
        _rmsnorm_kernel,
        out_shape=jax.ShapeDtypeStruct((m, d), out_dtype),
        grid=(m // tm,),
        in_specs=[pl.BlockSpec((tm, d), lambda i: (i, 0)),
                  pl.BlockSpec((1, d), lambda i: (0, 0))],
        out_specs=pl.BlockSpec((tm, d), lambda i: (i, 0)),
        compiler_params=_params(("parallel",)),
        name="rmsnorm",
    )(x, g.reshape(1, d).astype(F32))


def _mm_kernel(*refs, nk, act, has_bias, has_resid, w_rows):
    x_ref, w_ref = refs[0], refs[1]
    product = (lambda a, b: _dot_nt(a, b[0])) if w_rows else _dot
    idx = 2
    b_ref = r_ref = None
    if has_bias:
        b_ref = refs[idx]
        idx += 1
    if has_resid:
        r_ref = refs[idx]
        idx += 1
    o_ref = refs[idx]

    def epilogue(acc):
        if has_bias:
            acc = acc + b_ref[...]
        if act == "relu2":
            acc = jnp.square(jnp.maximum(acc, 0.0))
        if has_resid:
            acc = acc + r_ref[...]
        o_ref[...] = acc.astype(o_ref.dtype)

    if nk == 1:
        epilogue(product(x_ref[...], w_ref[...]))
    else:
        acc_ref = refs[idx + 1]
        k = pl.program_id(2)

        @pl.when(k == 0)
        def _():
            acc_ref[...] = jnp.zeros_like(acc_ref)

        acc_ref[...] += product(x_ref[...], w_ref[...])

        @pl.when(k == nk - 1)
        def _():
            epilogue(acc_ref[...])


def _mm(x, w, layer, *, bias=None, resid=None, act=None, out_dtype=BF16, tm=1024, tn=512, tk=None,
        w_rows=None, name="mm"):
    m, kdim = x.shape
    n = w_rows[1] if w_rows else w.shape[2]
    tm, tn = min(tm, m), min(tn, n)
    tk = kdim if tk is None else min(tk, kdim)
    nk = kdim // tk
    assert m % tm == 0 and n % tn == 0 and kdim % tk == 0
    if w_rows:
        row0 = w_rows[0]
        assert row0 % BF16_TILE_ROWS == 0
        w_spec = pl.BlockSpec((pl.Element(1), pl.Element(tn), pl.Element(tk)),
                              lambda i, j, k: (layer, pl.multiple_of(row0 + j * tn, BF16_TILE_ROWS), k * tk))
    else:
        w_spec = pl.BlockSpec((None, tk, tn), lambda i, j, k: (layer, k, j))
    in_specs = [pl.BlockSpec((tm, tk), lambda i, j, k: (i, k)), w_spec]
    args = [x, w]
    if bias is not None:
        in_specs.append(pl.BlockSpec((None, 1, tn), lambda i, j, k: (layer, 0, j)))
        args.append(bias)
    if resid is not None:
        in_specs.append(pl.BlockSpec((tm, tn), lambda i, j, k: (i, j)))
        args.append(resid)
    scratch = [pltpu.VMEM((tm, tn), F32)] if nk > 1 else []
    return pl.pallas_call(
        functools.partial(_mm_kernel, nk=nk, act=act, has_bias=bias is not None,
                          has_resid=resid is not None, w_rows=w_rows is not None),
        out_shape=jax.ShapeDtypeStruct((m, n), out_dtype),
        grid=(m // tm, n // tn, nk),
        in_specs=in_specs,
        out_specs=pl.BlockSpec((tm, tn), lambda i, j, k: (i, j)),
        scratch_shapes=scratch,
        compiler_params=_params(("parallel", "parallel", "arbitrary")),
        name=name,
    )(*args)


def _rows_to_cols(sel, rows):
    return sum(_dot_nt(sel, piece) for piece in _split_bf16(rows, EXACT_PIECES))


def _mlstm_kernel(q_ref, k_ref, v_ref, g_ref, o_ref, c_ref, m_ref, *, nc, heads, dk, dv, scale):
    lc = SCAN_CHUNK
    rev = pl.program_id(0) % 2
    c = pl.program_id(1)
    cc = c + rev * (nc - 1 - 2 * c)

    @pl.when(c == 0)
    def _():
        c_ref[...] = jnp.zeros_like(c_ref)
        m_ref[...] = jnp.zeros_like(m_ref)

    row = lax.broadcasted_iota(jnp.int32, (lc, lc), 0)
    col = lax.broadcasted_iota(jnp.int32, (lc, lc), 1)
    tri = ((col - row) * (1 - 2 * rev)) <= 0
    tri_b = jnp.where(tri, 1.0, 0.0).astype(BF16)
    eye_b = jnp.where(row == col, 1.0, 0.0).astype(BF16)
    ones_blk = (lax.broadcasted_iota(jnp.int32, (lc, LANES), 1) == 0).astype(BF16)

    gl = g_ref[cc]
    lf_all = _log_sigmoid(gl)
    lf_pieces = _split_bf16(lf_all, EXACT_PIECES)
    b_rows = sum(_dot_nt(piece, tri_b) for piece in lf_pieces)
    b_cols = sum(_dot_nt(tri_b, piece) for piece in lf_pieces)
    g_all = jnp.sum(lf_all, axis=1, keepdims=True)
    wk_rows, m_news, decays = [], [], []
    for hh in range(heads):
        ig = gl[2 * hh:2 * hh + 1, :]
        g = g_all[2 * hh + 1:2 * hh + 2, :]
        m_prev = m_ref[hh]
        a_row = g - b_rows[2 * hh + 1:2 * hh + 2, :] + ig
        m_new = jnp.maximum(g + m_prev, jnp.max(a_row, axis=1, keepdims=True))
        wk_rows.append(jnp.exp(a_row - m_new))
        m_news.append(m_new)
        decays.append(jnp.exp(g + m_prev - m_new))
    wk_cols = _rows_to_cols(eye_b, jnp.concatenate(wk_rows + [jnp.zeros((heads, lc), F32)], axis=0))

    rng = range(heads)
    qs = [q_ref[:, hh * dk:(hh + 1) * dk] for hh in rng]
    ks = [k_ref[:, hh * dk:(hh + 1) * dk] for hh in rng]
    v_augs = [jnp.concatenate([v_ref[:, hh * dv:(hh + 1) * dv], ones_blk], axis=1) for hh in rng]
    qk = [_dot_nt(qs[hh], ks[hh]) for hh in rng]
    qc = [_dot(qs[hh], c_ref[hh].astype(BF16)) for hh in rng]
    kv = [_dot_tn((ks[hh].astype(F32) * wk_cols[:, hh:hh + 1]).astype(BF16), v_augs[hh]) for hh in rng]
    for hh in rng:
        b_col = b_cols[:, 2 * hh + 1:2 * hh + 2]
        log_d = jnp.where(tri, b_col - b_rows[2 * hh + 1:2 * hh + 2, :] + gl[2 * hh:2 * hh + 1, :], NEG_BIG)
        m_inter = b_col + m_ref[hh]
        m_j = jnp.maximum(m_inter, jnp.max(log_d, axis=1, keepdims=True))
        smat = (qk[hh] * jnp.exp(log_d - m_j)).astype(BF16)
        num_aug = (_dot(smat, v_augs[hh]) + jnp.exp(m_inter - m_j) * qc[hh]) * scale
        num = num_aug[:, :dv]
        den = num_aug[:, dv:dv + 1]
        o_ref[:, hh * dv:(hh + 1) * dv] = (num / jnp.maximum(jnp.abs(den), jnp.exp(-m_j))).astype(o_ref.dtype)
        c_ref[hh] = decays[hh] * c_ref[hh] + kv[hh]
        m_ref[hh] = m_news[hh]


def _mlstm(proj, gates, *, batch, seq, q_off, k_off, v_off, heads, dk, dv):
    lc = SCAN_CHUNK
    nc = seq // lc
    assert dk == lc and seq % lc == 0 and dv % LANES == 0

    def rowblk(s, c):
        return (s // 2) * nc + c + (s % 2) * (nc - 1 - 2 * c)

    wqk, wv = heads * dk, heads * dv
    return pl.pallas_call(
        functools.partial(_mlstm_kernel, nc=nc, heads=heads, dk=dk, dv=dv, scale=dk ** -0.5),
        out_shape=jax.ShapeDtypeStruct((2, batch * seq, wv), F32),
        grid=(batch * 2, nc),
        in_specs=[
            pl.BlockSpec((lc, wqk), lambda s, c: (rowblk(s, c), q_off // wqk)),
            pl.BlockSpec((lc, wqk), lambda s, c: (rowblk(s, c), k_off // wqk)),
            pl.BlockSpec((lc, wv), lambda s, c: (rowblk(s, c), v_off // wv)),
            pl.BlockSpec((None, nc, 2 * heads, lc), lambda s, c: (s, 0, 0, 0)),
        ],
        out_specs=pl.BlockSpec((None, lc, wv), lambda s, c: (s % 2, rowblk(s, c), 0)),
        scratch_shapes=[pltpu.VMEM((heads, dk, dv + LANES), F32), pltpu.VMEM((heads, 1, 1), F32)],
        compiler_params=_params(("parallel", "arbitrary")),
        name="mlstm_scan",
    )(proj, proj, proj, gates)


def _gla_levels():
    lc = SCAN_CHUNK
    nlev = int(math.log2(lc)) + 1
    t = np.arange(lc)
    diff = np.zeros((2, nlev * lc, lc), np.float32)
    mask = np.zeros((2, nlev, lc, lc), np.float32)
    u = t[None, :]
    for d in range(2):
        cum = (u <= t[:, None]) if d == 0 else (u >= t[:, None])
        diff[d, :lc] = cum
        mask[d, 0] = np.eye(lc)
        for l in range(1, nlev):
            w = 1 << (l - 1)
            pair = t // (2 * w)
            second = (t % (2 * w)) >= w
            if d == 0:
                bd = pair * 2 * w + w - 1
                cum_bd = u <= bd[:, None]
                is_q, is_k = second, ~second
            else:
                bd = pair * 2 * w + w
                cum_bd = u >= bd[:, None]
                is_q, is_k = ~second, second
            diff[d, l * lc:(l + 1) * lc] = cum.astype(np.float32) - cum_bd.astype(np.float32)
            mask[d, l] = ((pair[:, None] == pair[None, :]) & is_q[:, None] & is_k[None, :])
    return diff, mask


def _gla_kernel(qf_ref, qb_ref, kf_ref, kb_ref, vf_ref, vb_ref, smf_ref, smb_ref, up_ref, bias_ref, d_ref,
                mask_ref, of_ref, ob_ref, s_ref, *, heads, dk, dv, nlev, scale):
    lc = SCAN_CHUNK
    c = pl.program_id(1)

    @pl.when(c == 0)
    def _():
        s_ref[...] = jnp.zeros_like(s_ref)

    dirs = (0, 1)
    q_refs, k_refs, v_refs = (qf_ref, qb_ref), (kf_ref, kb_ref), (vf_ref, vb_ref)
    sm_refs, o_refs = (smf_ref, smb_ref), (of_ref, ob_ref)
    hs = [slice(hh * dk, (hh + 1) * dk) for hh in range(heads)]

    zs = [_dot3(sm_refs[d][...], up_ref[d]) + bias_ref[d] for d in dirs]
    las = [_log_sigmoid(z) * (1.0 / GLA_TAU) for z in zs]
    xs = [sum(_dot(d_ref[d], piece) for piece in _split_bf16(las[d], EXACT_PIECES)) for d in dirs]
    g_rows = [jnp.sum(la, axis=0, keepdims=True) for la in las]

    qbs = [q_refs[d][...] for d in dirs]
    kbs = [k_refs[d][...] for d in dirs]
    qfs = [qb.astype(F32) for qb in qbs]
    kfs = [kb.astype(F32) for kb in kbs]
    atts = [[mask_ref[d, 0] * _dot_nt(qbs[d][:, s], kbs[d][:, s]) for s in hs] for d in dirs]
    for l in range(1, nlev):
        es = [jnp.exp(-jnp.abs(xs[d][l * lc:(l + 1) * lc])) for d in dirs]
        qes = [(qfs[d] * es[d]).astype(BF16) for d in dirs]
        kes = [(kfs[d] * es[d]).astype(BF16) for d in dirs]
        atts = [[att + mask_ref[d, l] * _dot_nt(qes[d][:, s], kes[d][:, s]) for att, s in zip(atts[d], hs)]
                for d in dirs]

    bs = [x[:lc] for x in xs]
    qss = [(qfs[d] * jnp.exp(bs[d])).astype(BF16) for d in dirs]
    kds = [(kfs[d] * jnp.exp(g_rows[d] - bs[d])).astype(BF16) for d in dirs]
    egs = [jnp.exp(g) for g in g_rows]
    for hh, s in enumerate(hs):
        for d in dirs:
            v = v_refs[d][:, hh * dv:(hh + 1) * dv]
            o = _dot(atts[d][hh].astype(BF16), v) + _dot_nt(qss[d][:, s], s_ref[d, hh].astype(BF16))
            o_refs[d][:, hh * dv:(hh + 1) * dv] = (o * scale).astype(o_refs[d].dtype)
            s_ref[d, hh] = egs[d][:, s] * s_ref[d, hh] + _dot_tn(v, kds[d][:, s])


def _gla(proj, small, up_pad, bias, *, batch, seq, q_off, k_off, v_off, heads, dk, dv):
    lc = SCAN_CHUNK
    nc = seq // lc
    assert dk == lc and seq % lc == 0
    diff, mask = _gla_levels()
    nlev = mask.shape[1]
    wqk, wv = heads * dk, heads * dv
    fwd = lambda b, c: b * nc + c
    bwd = lambda b, c: b * nc + nc - 1 - c

    def both(width, col):
        return [pl.BlockSpec((lc, width), lambda b, c: (fwd(b, c), col)),
                pl.BlockSpec((lc, width), lambda b, c: (bwd(b, c), col))]

    full = lambda shape: pl.BlockSpec(shape, lambda b, c: (0,) * len(shape))
    out = jax.ShapeDtypeStruct((batch * seq, wv), F32)
    return pl.pallas_call(
        functools.partial(_gla_kernel, heads=heads, dk=dk, dv=dv, nlev=nlev, scale=dk ** -0.5),
        out_shape=[out, out],
        grid=(batch, nc),
        in_specs=both(wqk, q_off // wqk) + both(wqk, k_off // wqk) + both(wv, v_off // wv) + both(LANES, 0) +
                 [full((2, LANES, wqk)), full((2, 1, wqk)), full((2, nlev * lc, lc)), full((2, nlev, lc, lc))],
        out_specs=both(wv, 0),
        scratch_shapes=[pltpu.VMEM((2, heads, dv, dk), F32)],
        compiler_params=_params(("parallel", "arbitrary")),
        name="gla_scan",
    )(proj, proj, proj, proj, proj, proj, small, small, up_pad, bias,
      jnp.asarray(diff, BF16), jnp.asarray(mask, F32))


def _headnorm_gate_kernel(hf_ref, hb_ref, g_ref, hn_ref, o_ref, *, heads, dv, act):
    h = hf_ref[...] + hb_ref[...]
    outs = []
    for hh in range(heads):
        blk = h[:, hh * dv:(hh + 1) * dv]
        r = lax.rsqrt(jnp.mean(blk * blk, axis=-1, keepdims=True) + EPS)
        outs.append(blk * r)
    hn = jnp.concatenate(outs, axis=1) * hn_ref[...]
    gate = g_ref[...].astype(F32)
    sig = 1.0 / (1.0 + jnp.exp(-gate))
    gate = sig if act == "sigmoid" else gate * sig
    o_ref[...] = (hn * gate).astype(o_ref.dtype)


def _headnorm_gate(h2, proj, gate_off, head_norm, *, heads, dv, act, tm=512):
    if isinstance(h2, (tuple, list)):
        hf, hb = h2
        m, width = hf.shape
        dir_specs = [pl.BlockSpec((tm, width), lambda i: (i, 0))] * 2
    else:
        hf = hb = h2
        _, m, width = h2.shape
        dir_specs = [pl.BlockSpec((None, tm, width), lambda i: (0, i, 0)),
                     pl.BlockSpec((None, tm, width), lambda i: (1, i, 0))]
    assert m % tm == 0
    return pl.pallas_call(
        functools.partial(_headnorm_gate_kernel, heads=heads, dv=dv, act=act),
        out_shape=jax.ShapeDtypeStruct((m, width), BF16),
        grid=(m // tm,),
        in_specs=dir_specs + [pl.BlockSpec((tm, width), lambda i: (i, gate_off // width)),
                              pl.BlockSpec((1, width), lambda i: (0, 0))],
        out_specs=pl.BlockSpec((tm, width), lambda i: (i, 0)),
        compiler_params=_params(("parallel",)),
        name="headnorm_gate",
    )(hf, hb, proj, head_norm.reshape(1, width).astype(F32))


def _cm_short_conv_kernel(x0_ref, x1_ref, x2_ref, w_ref, b_ref, o0_ref, o1_ref, o2_ref, s_ref, t_ref):
    seq, tc = x0_ref.shape
    half, n2 = o0_ref.shape[1], o0_ref.shape[2]
    for p, (x_ref, o_ref) in enumerate(((x0_ref, o0_ref), (x1_ref, o1_ref), (x2_ref, o2_ref))):
        x = x_ref[...].astype(F32)
        w = w_ref[p]
        bias = b_ref[p]
        s_ref[...] = (pltpu.roll(x, 1, axis=0) * w[0:1] + x * w[1:2]
                      + pltpu.roll(x, seq - 1, axis=0) * w[2:3] + bias)
        s_ref[0:1, :] = x[0:1] * w[1:2] + x[1:2] * w[2:3] + bias
        s_ref[seq - 1:seq, :] = x[seq - 2:seq - 1] * w[0:1] + x[seq - 1:seq] * w[1:2] + bias

        def flip(a, carry):
            slab = s_ref[pl.ds(pl.multiple_of(a * n2, n2), n2), :]
            t_ref[pl.ds(pl.multiple_of(a * tc, tc), tc), :] = slab.T
            return carry

        lax.fori_loop(0, half, flip, 0, unroll=min(4, half))

        def body(ch, carry, o_ref=o_ref):
            o_ref[ch] = t_ref[pl.ds(ch, half, stride=tc), :].astype(o_ref.dtype)
            return carry

        lax.fori_loop(0, tc, body, 0, unroll=GATHER_UNROLL)


def _cm_short_conv(proj, conv_w, conv_b, *, batch, seq, off, ch, n2):
    tc = LANES
    nct = ch // tc
    half = seq // n2
    w = conv_w.reshape(3, 3, nct, tc).transpose(1, 2, 0, 3).astype(F32)
    b = conv_b.reshape(3, nct, 1, tc).astype(F32)
    in_specs = [pl.BlockSpec((seq, tc), functools.partial(lambda b_, j, p: (b_, off // tc + p * nct + j), p=p))
                for p in range(3)]
    in_specs += [pl.BlockSpec((3, None, 3, tc), lambda b_, j: (0, j, 0, 0)),
                 pl.BlockSpec((3, None, 1, tc), lambda b_, j: (0, j, 0, 0))]
    out = jax.ShapeDtypeStruct((batch, ch, half, n2), BF16)
    return pl.pallas_call(
        _cm_short_conv_kernel,
        out_shape=[out, out, out],
        grid=(batch, nct),
        in_specs=in_specs,
        out_specs=[pl.BlockSpec((None, tc, half, n2), lambda b_, j: (b_, j, 0, 0))] * 3,
        scratch_shapes=[pltpu.VMEM((seq, tc), F32), pltpu.VMEM((half * tc, n2), F32)],
        compiler_params=_params(("parallel", "parallel")),
        name="hyena_short_conv",
    )(proj, proj, proj, w, b)


def _cm_taps_kernel(z_ref, w1_ref, b1_ref, w2_ref, b2_ref, w3_ref, f_ref, dl_ref, o_ref, t_ref, h_ref, *, emb, n2):
    @pl.when(pl.program_id(1) == 0)
    def _():
        f = f_ref[...]
        h = jnp.sin(f * (_dot3(w1_ref[...], z_ref[...]) + b1_ref[...]))
        h_ref[...] = jnp.sin(f * (_dot3(w2_ref[...], h) + b2_ref[...]))

    win = jnp.exp(-dl_ref[...] * z_ref[0:1, :]) * z_ref[emb:emb + 1, :]
    out = _dot3(w3_ref[...], h_ref[...]) * win
    gc = out.shape[0]
    tile = out.shape[1] // n2
    for s in range(tile):
        t_ref[s * gc:(s + 1) * gc, :] = out[:, s * n2:(s + 1) * n2]

    def body(ch, carry):
        o_ref[ch] = t_ref[pl.ds(ch, tile, stride=gc), :].astype(o_ref.dtype)
        return carry

    lax.fori_loop(0, gc, body, 0, unroll=GATHER_UNROLL)


def _cm_taps(seq, w1, b1, w2, b2, w3, freq, ch, n2, gc=256):
    emb, hid = w1.shape
    n1 = 2 * seq // n2
    half = n1 // 2
    tile = min(BF16_TILE_ROWS, half)
    order = w3.shape[1] // (2 * ch)
    lag = jnp.arange(2 * seq, dtype=jnp.int32)
    pos = jnp.where(lag < seq, lag, 2 * seq - lag)
    n = pos.astype(F32)
    t = n / (seq - 1)
    bands = jnp.linspace(1e-4, HYENA_BANDS - 1, HYENA_BANDS, dtype=F32)
    ang = (2.0 * math.pi * n / seq)[:, None] * bands[None, :]
    valid = (lag != seq).astype(F32)
    z = jnp.concatenate([t[:, None], jnp.cos(ang), -jnp.sin(ang), valid[:, None]], axis=-1)
    zt = jnp.pad(z, ((0, 0), (0, LANES - emb - 1))).T
    deltas = jnp.abs(jnp.linspace(HYENA_MIN_DECAY, HYENA_MAX_DECAY, ch, dtype=F32))
    dl = jnp.tile(deltas, order).reshape(order * ch, 1)
    pad_h = LANES - hid
    w1t = jnp.pad(w1.astype(F32), ((0, LANES - emb), (0, pad_h))).T
    w2t = jnp.pad(w2.astype(F32), ((0, pad_h), (0, pad_h))).T
    w3t = jnp.pad(w3.astype(F32), ((0, pad_h), (0, 0))).T.reshape(order, 2, ch, LANES)
    w3t = w3t.transpose(1, 0, 2, 3).reshape(2, order * ch, LANES)
    colv = lambda v: jnp.pad(v.astype(F32), (0, pad_h)).reshape(LANES, 1)
    gc = min(gc, order * ch)
    nat = n1 // tile
    full = lambda shape: pl.BlockSpec(shape, lambda a, j: (0, 0))
    return pl.pallas_call(
        functools.partial(_cm_taps_kernel, emb=emb, n2=n2),
        out_shape=jax.ShapeDtypeStruct((order * ch, n1, n2), BF16),
        grid=(nat, order * ch // gc),
        in_specs=[pl.BlockSpec((LANES, tile * n2), lambda a, j: (0, a)),
                  full((LANES, LANES)), full((LANES, 1)), full((LANES, LANES)), full((LANES, 1)),
                  pl.BlockSpec((None, gc, LANES), lambda a, j: (a // (nat // 2), j, 0)),
                  full((LANES, 1)),
                  pl.BlockSpec((gc, 1), lambda a, j: (j, 0))],
        out_specs=pl.BlockSpec((gc, tile, n2), lambda a, j: (j, a, 0)),
        scratch_shapes=[pltpu.VMEM((tile * gc, n2), F32), pltpu.VMEM((LANES, tile * n2), F32)],
        compiler_params=_params(("parallel", "arbitrary")),
        name="hyena_filters",
    )(zt, w1t, colv(b1), w2t, colv(b2), w3t, colv(freq), dl)


def _cm_tables(n1, n2):
    a = jnp.arange(n1, dtype=jnp.int32)
    ang1 = ((a[:, None] * a[None, :]) % n1).astype(F32) * (2.0 * math.pi / n1)
    f1r, f1i = jnp.cos(ang1), -jnp.sin(ang1)
    b = jnp.arange(n2, dtype=jnp.int32)
    angt = (a[:, None] * b[None, :]).astype(F32) * (2.0 * math.pi / (n1 * n2))
    tw = jnp.stack([jnp.cos(angt), -jnp.sin(angt)])
    ang2 = ((b[:, None] * b[None, :]) % n2).astype(F32) * (2.0 * math.pi / n2)
    wr, wi = jnp.cos(ang2), -jnp.sin(ang2)
    wb = jnp.concatenate([jnp.concatenate([wr, wi], axis=1),
                          jnp.concatenate([-wi, wr], axis=1)], axis=0).astype(BF16)
    return f1r, f1i, tw, wb, wb.T


def _cm_forward(x_ref, f1, twr, twi, wb, n1):
    ys = []
    for g in range(x_ref.shape[0]):
        p = _dot(f1, x_ref[g])
        pr, pi = p[:n1], p[n1:]
        ys.append(jnp.concatenate([pr * twr - pi * twi, pr * twi + pi * twr], axis=1))
    y = jnp.concatenate(ys, axis=0).astype(BF16)
    return _dot(y, wb)


def _cm_spec_kernel(t_ref, f1_ref, tw_ref, wb_ref, o_ref, *, n1, inv_n):
    z = _cm_forward(t_ref, f1_ref[...], tw_ref[0], tw_ref[1], wb_ref[...], n1)
    o_ref[...] = (z * inv_n).reshape(o_ref.shape)


def _cm_spectrum(taps, f1, tw, wb, *, group=32):
    nch, n1, n2 = taps.shape
    group = min(group, nch)
    full2 = lambda shape: pl.BlockSpec(shape, lambda j: (0,) * len(shape))
    return pl.pallas_call(
        functools.partial(_cm_spec_kernel, n1=n1, inv_n=1.0 / (n1 * n2)),
        out_shape=jax.ShapeDtypeStruct((nch, n1, 2 * n2), F32),
        grid=(nch // group,),
        in_specs=[pl.BlockSpec((group, n1, n2), lambda j: (j, 0, 0)),
                  full2((2 * n1, n1)), full2((2, n1, n2)), full2((2 * n2, 2 * n2))],
        out_specs=pl.BlockSpec((group, n1, 2 * n2), lambda j: (j, 0, 0)),
        compiler_params=_params(("parallel",)),
        name="hyena_filter_spectrum",
    )(taps, f1, tw, wb)


def _cm_conv_kernel(x_ref, gate_ref, kf_ref, skip_ref, f1_ref, f1h_ref, tw_ref, wb_ref, wbt_ref, o_ref, *, n1, n2):
    group = x_ref.shape[0]
    twr, twi = tw_ref[0], tw_ref[1]
    z = _cm_forward(x_ref, f1_ref[...], twr, twi, wb_ref[...], n1)
    kf = kf_ref[...].reshape(group * n1, 2 * n2)
    zr, zi, kr, ki = z[:, :n2], z[:, n2:], kf[:, :n2], kf[:, n2:]
    s = jnp.concatenate([zr * kr - zi * ki, zr * ki + zi * kr], axis=1).astype(BF16)
    v = _dot(s, wbt_ref[...])
    f1h = f1h_ref[...]
    for g in range(group):
        vr, vi = v[g * n1:(g + 1) * n1, :n2], v[g * n1:(g + 1) * n1, n2:]
        u = jnp.concatenate([vr * twr + vi * twi, vi * twr - vr * twi], axis=0).astype(BF16)
        y = _dot(f1h, u)
        y = y + skip_ref[g] * x_ref[g].astype(F32)
        o_ref[g] = (gate_ref[g].astype(F32) * y).astype(o_ref.dtype)


def _cm_conv(x, gate, kf, skip, f1d, f1h, tw, wb, wbt, *, kf_off, out_dtype, group=32):
    batch, ch, half, n2 = x.shape
    n1 = 2 * half
    group = min(group, ch)
    sig = pl.BlockSpec((None, group, half, n2), lambda j, b: (b, j, 0, 0))
    full2 = lambda shape: pl.BlockSpec(shape, lambda j, b: (0,) * len(shape))
    skip_b = jnp.broadcast_to(skip.astype(F32)[:, None, None], (ch, 1, n2))
    return pl.pallas_call(
        functools.partial(_cm_conv_kernel, n1=n1, n2=n2),
        out_shape=jax.ShapeDtypeStruct((batch, ch, half, n2), out_dtype),
        grid=(ch // group, batch),
        in_specs=[sig, sig,
                  pl.BlockSpec((group, n1, 2 * n2), lambda j, b: (kf_off // group + j, 0, 0)),
                  pl.BlockSpec((group, 1, n2), lambda j, b: (j, 0, 0)),
                  full2((2 * n1, half)), full2((half, 2 * n1)), full2((2, n1, n2)),
                  full2((2 * n2, 2 * n2)), full2((2 * n2, 2 * n2))],
        out_specs=sig,
        compiler_params=_params(("parallel", "parallel")),
        name="hyena_fft_conv",
    )(x, gate, kf, skip_b, f1d, f1h, tw, wb, wbt)


def _cm_to_time_major_kernel(y_ref, o_ref, *, half):
    tc = o_ref.shape[1]
    n2 = y_ref.shape[1]

    def body(a, carry):
        slab = y_ref[pl.ds(a, tc, stride=half), :]
        o_ref[pl.ds(pl.multiple_of(a * n2, n2), n2), :] = slab.T.astype(o_ref.dtype)
        return carry

    lax.fori_loop(0, half, body, 0, unroll=min(4, half))


def _cm_to_time_major(y, *, batch, seq, ch):
    half, n2 = y.shape[2], y.shape[3]
    tc = LANES
    return pl.pallas_call(
        functools.partial(_cm_to_time_major_kernel, half=half),
        out_shape=jax.ShapeDtypeStruct((batch * seq, ch), BF16),
        grid=(batch, ch // tc),
        in_specs=[pl.BlockSpec((None, tc * half, n2), lambda b, j: (b, j, 0))],
        out_specs=pl.BlockSpec((seq, tc), lambda b, j: (b, j)),
        compiler_params=_params(("parallel", "parallel")),
        name="hyena_to_time_major",
    )(y.reshape(batch, ch * half, n2))


def _hyena_cm(proj, hy_off, conv_w, conv_b, w1, b1, w2, b2, w3, freq, skip, *, batch, seq, ch):
    n2 = FFT_N2
    n1 = 2 * seq // n2
    half = n1 // 2
    v, x1, x2 = _cm_short_conv(proj, conv_w, conv_b, batch=batch, seq=seq, off=hy_off, ch=ch, n2=n2)
    taps = _cm_taps(seq, w1, b1, w2, b2, w3, freq, ch, n2)
    f1r, f1i, tw, wb, wbt = _cm_tables(n1, n2)
    fstack = jnp.concatenate([f1r, f1i], axis=0)
    kf = _cm_spectrum(taps, fstack.astype(BF16), tw, wb)
    f1d = fstack[:, :half].astype(BF16)
    f1h = jnp.concatenate([f1r[:half], f1i[:half]], axis=1).astype(BF16)
    zcur = v
    for o, gate in enumerate((x1, x2)):
        last = o == HYENA_ORDER - 1
        zcur = _cm_conv(zcur, gate, kf, skip[o], f1d, f1h, tw, wb, wbt, kf_off=o * ch,
                        out_dtype=F32 if last else BF16)
    return _cm_to_time_major(zcur, batch=batch, seq=seq, ch=ch)


def _merge_kernel(ya_ref, yb_ref, yc_ref, ga_ref, gb_ref, gc_ref, w_ref, o_ref):
    acc = None
    for n, (y_ref, g_ref) in enumerate(((ya_ref, ga_ref), (yb_ref, gb_ref), (yc_ref, gc_ref))):
        gate = 1.0 / (1.0 + jnp.exp(-g_ref[...].astype(F32)))
        term = gate * _dot(y_ref[...], w_ref[n])
        acc = term if acc is None else acc + term
    o_ref[...] = acc.astype(o_ref.dtype)


def _merge(ys, proj, gate_off, w_branch, layer, *, d_model, tm=1024, tn=1024):
    m, width = ys[0].shape
    tm, tn = min(tm, m), min(tn, d_model)
    y_spec = pl.BlockSpec((tm, width), lambda i, j: (i, 0))
    g_specs = [pl.BlockSpec((tm, tn), functools.partial(
        lambda i, j, n: (i, (gate_off + n * d_model) // tn + j), n=n)) for n in range(N_BRANCH)]
    return pl.pallas_call(
        _merge_kernel,
        out_shape=jax.ShapeDtypeStruct((m, d_model), BF16),
        grid=(m // tm, d_model // tn),
        in_specs=[y_spec, y_spec, y_spec] + g_specs +
                 [pl.BlockSpec((None, N_BRANCH, width, tn), lambda i, j: (layer, 0, 0, j))],
        out_specs=pl.BlockSpec((tm, tn), lambda i, j: (i, j)),
        compiler_params=_params(("parallel", "parallel")),
        name="branch_merge",
    )(*ys, proj, proj, proj, w_branch)


def _xattn_kernel(q_ref, k_ref, v_ref, o_ref, *, scale):
    s = _dot_nt(q_ref[...], k_ref[...]) * scale
    s = s - jnp.max(s, axis=-1, keepdims=True)
    p = jnp.exp(s)
    l = jnp.sum(p, axis=-1, keepdims=True)
    o_ref[...] = (_dot(p.astype(BF16), v_ref[...]) / l).astype(o_ref.dtype)


def _xattn(q, kv, *, batch, seq, mem, heads, hd, tq=1024):
    tq = min(tq, seq)
    nq = seq // tq
    return pl.pallas_call(
        functools.partial(_xattn_kernel, scale=hd ** -0.5),
        out_shape=jax.ShapeDtypeStruct(q.shape, BF16),
        grid=(batch, heads, nq),
        in_specs=[pl.BlockSpec((tq, hd), lambda b, h, i: (b * nq + i, h)),
                  pl.BlockSpec((mem, hd), lambda b, h, i: (b, h)),
                  pl.BlockSpec((mem, hd), lambda b, h, i: (b, heads + h))],
        out_specs=pl.BlockSpec((tq, hd), lambda b, h, i: (b * nq + i, h)),
        compiler_params=_params(("parallel", "parallel", "parallel")),
        name="xattn_core",
    )(q, kv, kv)


def _in_proj_layout(d):
    mix = d // 4
    m_dk = mix // MLSTM_HEADS // 2
    g_dk = mix // GLA_HEADS // 2
    sizes = (MLSTM_HEADS * m_dk, MLSTM_HEADS * m_dk, mix, mix, 4 * MLSTM_HEADS,
             GLA_HEADS * g_dk, GLA_HEADS * g_dk, mix, mix, 2 * GLA_RANK, 3 * mix, N_BRANCH * d)
    offs = [int(o) for o in np.concatenate([[0], np.cumsum(sizes)])]
    runs = ((0, 1, 2, 3), (5, 6, 7, 8), (10, 11))
    narrow = (4, 9)
    where = {i: (r, offs[i] - offs[run[0]]) for r, run in enumerate(runs) for i in run}
    return sizes, offs, runs, narrow, where


def kernel(x, mem, norm_gains, final_norm, w_in, b_in, mlstm_head_norm, gla_decay_up, gla_decay_bias, gla_head_norm, hyena_conv_w, hyena_conv_b, hyena_ffn_w1, hyena_ffn_b1, hyena_ffn_w2, hyena_ffn_b2, hyena_ffn_w3, hyena_freq, hyena_skip, w_branch, w_out, xattn_wq, xattn_wkv, xattn_wo, mlp_w1, mlp_w2):
    batch, seq, d = x.shape
    mem_tokens = mem.shape[1]
    depth = norm_gains.shape[0]
    mix = d // 4
    mh, gh = MLSTM_HEADS, GLA_HEADS
    m_dv, g_dv = mix // mh, mix // gh
    m_dk, g_dk = m_dv // 2, g_dv // 2
    nc = seq // SCAN_CHUNK

    sizes, offs, runs, narrow, where = _in_proj_layout(d)
    seg = lambda a, i: a[..., offs[i]:offs[i + 1]]
    w_in_rows = jnp.swapaxes(w_in, 1, 2).astype(BF16)
    run_rows = [(offs[run[0]], sum(sizes[i] for i in run)) for run in runs]
    b_runs = [b_in[:, r0:r0 + n][:, None, :].astype(F32) for r0, n in run_rows]
    n_small = sum(sizes[i] for i in narrow)
    w_small = jnp.pad(jnp.concatenate([w_in_rows[:, offs[i]:offs[i + 1], :] for i in narrow], axis=1),
                      ((0, 0), (0, LANES - n_small), (0, 0)))
    b_small = jnp.pad(jnp.concatenate([seg(b_in, i) for i in narrow], axis=1),
                      ((0, 0), (0, LANES - n_small)))[:, None, :].astype(F32)
    col = lambda i: where[i][1]
    wb16, wo16 = w_branch.astype(BF16), w_out.astype(BF16)
    wq16, wkv16, wxo16 = xattn_wq.astype(BF16), xattn_wkv.astype(BF16), xattn_wo.astype(BF16)
    w1_16, w2_16 = mlp_w1.astype(BF16), mlp_w2.astype(BF16)

    h = x.reshape(batch * seq, d)
    memf = mem.reshape(batch * mem_tokens, d)
    for l in range(depth):
        g = norm_gains[l]
        xn = _rmsnorm(h, g[0], BF16)
        proj_m, proj_g, proj_h = [
            _mm(xn, w_in_rows, l, bias=b_run, out_dtype=BF16, tn=1024, w_rows=rw, name="in_proj")
            for rw, b_run in zip(run_rows, b_runs)]
        small = _mm(xn, w_small, l, bias=b_small, out_dtype=F32, w_rows=(0, LANES), name="in_proj_gates")

        mg = small[:, :4 * mh].reshape(batch, nc, SCAN_CHUNK, 2, 2, mh)
        mgates = mg.transpose(0, 3, 1, 5, 4, 2).reshape(batch * 2, nc, 2 * mh, SCAN_CHUNK)
        h_m = _mlstm(proj_m, mgates, batch=batch, seq=seq, q_off=col(0), k_off=col(1), v_off=col(2),
                     heads=mh, dk=m_dk, dv=m_dv)
        y_a = _headnorm_gate(h_m, proj_m, col(3), mlstm_head_norm[l], heads=mh, dv=m_dv, act="sigmoid")

        up_pad = jnp.zeros((2, LANES, gh * g_dk), F32)
        for dr in range(2):
            r0 = 4 * mh + dr * GLA_RANK
            up_pad = up_pad.at[dr, r0:r0 + GLA_RANK, :].set(gla_decay_up[l, dr].astype(F32))
        gbias = gla_decay_bias[l].reshape(2, 1, gh * g_dk).astype(F32)
        h_g = _gla(proj_g, small, up_pad, gbias, batch=batch, seq=seq, q_off=col(5), k_off=col(6),
                   v_off=col(7), heads=gh, dk=g_dk, dv=g_dv)
        y_b = _headnorm_gate(h_g, proj_g, col(8), gla_head_norm[l], heads=gh, dv=g_dv, act="silu")

        y_c = _hyena_cm(proj_h, col(10), hyena_conv_w[l], hyena_conv_b[l], hyena_ffn_w1[l], hyena_ffn_b1[l],
                        hyena_ffn_w2[l], hyena_ffn_b2[l], hyena_ffn_w3[l], hyena_freq[l], hyena_skip[l],
                        batch=batch, seq=seq, ch=mix)

        merged = _merge((y_a, y_b, y_c), proj_h, col(11), wb16, l, d_model=d)
        h = _mm(merged, wo16, l, resid=h, out_dtype=F32, tn=1024, name="mixer_out")

        xn = _rmsnorm(h, g[1], BF16)
        memn = _rmsnorm(memf, g[2], BF16)
        q = _mm(xn, wq16, l, out_dtype=BF16, tn=1024, name="xattn_q")
        kv = _mm(memn, wkv16, l, out_dtype=BF16, tn=1024, name="xattn_kv")
        o = _xattn(q, kv, batch=batch, seq=seq, mem=mem_tokens, heads=XATTN_HEADS, hd=d // XATTN_HEADS)
        h = _mm(o, wxo16, l, resid=h, out_dtype=F32, tn=1024, name="xattn_out")

        xn = _rmsnorm(h, g[3], BF16)
        hid = _mm(xn, w1_16, l, act="relu2", out_dtype=BF16, tn=1024, name="mlp_up")
        h = _mm(hid, w2_16, l, resid=h, out_dtype=F32, tm=512, tn=256, name="mlp_down")
    return _rmsnorm(h, final_norm, x.dtype).reshape(batch, seq, d)
```

```python
import functools
import math

import numpy as np
import jax
import jax.numpy as jnp
from jax import lax
from jax.experimental import pallas as pl
from jax.experimental.pallas import tpu as pltpu

F32 = jnp.float32
BF16 = jnp.bfloat16

N_BRANCH = 3
MLSTM_HEADS = 4
GLA_HEADS = 4
GLA_RANK = 16
GLA_TAU = 16.0
HYENA_ORDER = 2
HYENA_BANDS = 16
HYENA_MIN_DECAY = math.log(1e-2) / 1.5
HYENA_MAX_DECAY = math.log(1e-2) / 0.3
XATTN_HEADS = 4
EPS = 1e-6

LANES = 128
BF16_TILE_ROWS = 16
VMEM_LIMIT_BYTES = 56 * 1024 * 1024

SCAN_CHUNK = 128
FFT_N2 = 128
NEG_BIG = -1e30
GATHER_UNROLL = 8
EXACT_PIECES = 2


def _params(sem):
    return pltpu.CompilerParams(dimension_semantics=sem, vmem_limit_bytes=VMEM_LIMIT_BYTES)


def _log_sigmoid(x):
    return -(jnp.maximum(-x, 0.0) + jnp.log(1.0 + jnp.exp(-jnp.abs(x))))


def _split_bf16(a, parts):
    out = []
    r = a
    for _ in range(parts):
        p = r.astype(BF16)
        out.append(p)
        r = r - p.astype(F32)
    return out


def _dot(a, b):
    return jnp.dot(a, b, preferred_element_type=F32)


def _dot_nt(a, b):
    return lax.dot_general(a, b, (((1,), (1,)), ((), ())), preferred_element_type=F32)


def _dot_tn(a, b):
    return lax.dot_general(a, b, (((0,), (0,)), ((), ())), preferred_element_type=F32)


def _dot3(a, b):
    a_hi, a_lo = _split_bf16(a, 2)
    b_hi, b_lo = _split_bf16(b, 2)
    return _dot(a_hi, b_hi) + _dot(a_hi, b_lo) + _dot(a_lo, b_hi)


def _rmsnorm_kernel(x_ref, g_ref, o_ref):
    x = x_ref[...].astype(F32)
    r = lax.rsqrt(jnp.mean(x * x, axis=-1, keepdims=True) + EPS)
    o_ref[...] = ((x * r) * g_ref[...]).astype(o_ref.dtype)


def _rmsnorm(x, g, out_dtype, tm=256):
    m, d = x.shape
    tm = min(tm, m)
    return pl.pallas_call(
        _rmsnorm_kernel,
        out_shape=jax.ShapeDtypeStruct((m, d), out_dtype),
        grid=(m // tm,),
        in_specs=[pl.BlockSpec((tm, d), lambda i: (i, 0)),
                  pl.BlockSpec((1, d), lambda i: (0, 0))],
        out_specs=pl.BlockSpec((tm, d), lambda i: (i, 0)),
        compiler_params=_params(("parallel",)),
        name="rmsnorm",
    )(x, g.reshape(1, d).astype(F32))


def _mm_kernel(*refs, nk, act, has_bias, has_resid, w_rows):
    x_ref, w_ref = refs[0], refs[1]
    product = (lambda a, b: _dot_nt(a, b[0])) if w_rows else _dot
    idx = 2
    b_ref = r_ref = None
    if has_bias:
        b_ref = refs[idx]
        idx += 1
    if has_resid:
        r_ref = refs[idx]
        idx += 1
    o_ref = refs[idx]

    def epilogue(acc):
        if has_bias:
            acc = acc + b_ref[...]
        if act == "relu2":
            acc = jnp.square(jnp.maximum(acc, 0.0))
        if has_resid:
            acc = acc + r_ref[...]
        o_ref[...] = acc.astype(o_ref.dtype)

    if nk == 1:
        epilogue(product(x_ref[...], w_ref[...]))
    else:
        acc_ref = refs[idx + 1]
        k = pl.program_id(2)

        @pl.when(k == 0)
        def _():
            acc_ref[...] = jnp.zeros_like(acc_ref)

        acc_ref[...] += product(x_ref[...], w_ref[...])

        @pl.when(k == nk - 1)
        def _():
            epilogue(acc_ref[...])


def _mm(x, w, layer, *, bias=None, resid=None, act=None, out_dtype=BF16, tm=1024, tn=512, tk=None,
        w_rows=None, name="mm"):
    m, kdim = x.shape
    n = w_rows[1] if w_rows else w.shape[2]
    tm, tn = min(tm, m), min(tn, n)
    tk = kdim if tk is None else min(tk, kdim)
    nk = kdim // tk
    assert m % tm == 0 and n % tn == 0 and kdim % tk == 0
    if w_rows:
        row0 = w_rows[0]
        assert row0 % BF16_TILE_ROWS == 0
        w_spec = pl.BlockSpec((pl.Element(1), pl.Element(tn), pl.Element(tk)),
                              lambda i, j, k: (layer, pl.multiple_of(row0 + j * tn, BF16_TILE_ROWS), k * tk))
    else:
        w_spec = pl.BlockSpec((None, tk, tn), lambda i, j, k: (layer, k, j))
    in_specs = [pl.BlockSpec((tm, tk), lambda i, j, k: (i, k)), w_spec]
    args = [x, w]
    if bias is not None:
        in_specs.append(pl.BlockSpec((None, 1, tn), lambda i, j, k: (layer, 0, j)))
        args.append(bias)
    if resid is not None:
        in_specs.append(pl.BlockSpec((tm, tn), lambda i, j, k: (i, j)))
        args.append(resid)
    scratch = [pltpu.VMEM((tm, tn), F32)] if nk > 1 else []
    return pl.pallas_call(
        functools.partial(_mm_kernel, nk=nk, act=act, has_bias=bias is not None,
                          has_resid=resid is not None, w_rows=w_rows is not None),
        out_shape=jax.ShapeDtypeStruct((m, n), out_dtype),
        grid=(m // tm, n // tn, nk),
        in_specs=in_specs,
        out_specs=pl.BlockSpec((tm, tn), lambda i, j, k: (i, j)),
        scratch_shapes=scratch,
        compiler_params=_params(("parallel", "parallel", "arbitrary")),
        name=name,
    )(*args)


def _rows_to_cols(sel, rows):
    return sum(_dot_nt(sel, piece) for piece in _split_bf16(rows, EXACT_PIECES))


def _mlstm_kernel(qf_ref, qb_ref, kf_ref, kb_ref, vf_ref, vb_ref, g_ref, of_ref, ob_ref, c_ref, m_ref, *,
                  nc, heads, dk, dv, scale):
    lc = SCAN_CHUNK
    c = pl.program_id(1)

    @pl.when(c == 0)
    def _():
        c_ref[...] = jnp.zeros_like(c_ref)
        m_ref[...] = jnp.zeros_like(m_ref)

    dirs = (0, 1)
    q_refs, k_refs, v_refs, o_refs = (qf_ref, qb_ref), (kf_ref, kb_ref), (vf_ref, vb_ref), (of_ref, ob_ref)
    row = lax.broadcasted_iota(jnp.int32, (lc, lc), 0)
    col = lax.broadcasted_iota(jnp.int32, (lc, lc), 1)
    tris = (col <= row, col >= row)
    tri_bs = [jnp.where(t, 1.0, 0.0).astype(BF16) for t in tris]
    eye_b = jnp.where(row == col, 1.0, 0.0).astype(BF16)
    ones_blk = (lax.broadcasted_iota(jnp.int32, (lc, LANES), 1) == 0).astype(BF16)

    gls = [g_ref[0, c], g_ref[1, nc - 1 - c]]
    lf_alls = [_log_sigmoid(gl) for gl in gls]
    lf_pieces = [_split_bf16(lf, EXACT_PIECES) for lf in lf_alls]
    b_rows = [sum(_dot_nt(piece, tri_bs[d]) for piece in lf_pieces[d]) for d in dirs]
    b_cols = [sum(_dot_nt(tri_bs[d], piece) for piece in lf_pieces[d]) for d in dirs]
    g_alls = [jnp.sum(lf, axis=1, keepdims=True) for lf in lf_alls]
    wk_rows, m_news, decays = ([], []), ([], []), ([], [])
    for d in dirs:
        for hh in range(heads):
            ig = gls[d][2 * hh:2 * hh + 1, :]
            g = g_alls[d][2 * hh + 1:2 * hh + 2, :]
            m_prev = m_ref[d, hh]
            a_row = g - b_rows[d][2 * hh + 1:2 * hh + 2, :] + ig
            m_new = jnp.maximum(g + m_prev, jnp.max(a_row, axis=1, keepdims=True))
            wk_rows[d].append(jnp.exp(a_row - m_new))
            m_news[d].append(m_new)
            decays[d].append(jnp.exp(g + m_prev - m_new))
    wk_cols = [_rows_to_cols(eye_b, jnp.concatenate(wk_rows[d] + [jnp.zeros((heads, lc), F32)], axis=0))
               for d in dirs]

    chains = [(d, hh) for hh in range(heads) for d in dirs]
    qs = {ch: q_refs[ch[0]][:, ch[1] * dk:(ch[1] + 1) * dk] for ch in chains}
    ks = {ch: k_refs[ch[0]][:, ch[1] * dk:(ch[1] + 1) * dk] for ch in chains}
    v_augs = {ch: jnp.concatenate([v_refs[ch[0]][:, ch[1] * dv:(ch[1] + 1) * dv], ones_blk], axis=1)
              for ch in chains}
    qk = {ch: _dot_nt(qs[ch], ks[ch]) for ch in chains}
    qc = {ch: _dot(qs[ch], c_ref[ch[0], ch[1]].astype(BF16)) for ch in chains}
    kv = {ch: _dot_tn((ks[ch].astype(F32) * wk_cols[ch[0]][:, ch[1]:ch[1] + 1]).astype(BF16), v_augs[ch])
          for ch in chains}
    for d, hh in chains:
        b_col = b_cols[d][:, 2 * hh + 1:2 * hh + 2]
        log_d = jnp.where(tris[d], b_col - b_rows[d][2 * hh + 1:2 * hh + 2, :] + gls[d][2 * hh:2 * hh + 1, :],
                          NEG_BIG)
        m_inter = b_col + m_ref[d, hh]
        m_j = jnp.maximum(m_inter, jnp.max(log_d, axis=1, keepdims=True))
        smat = (qk[d, hh] * jnp.exp(log_d - m_j)).astype(BF16)
        num_aug = (_dot(smat, v_augs[d, hh]) + jnp.exp(m_inter - m_j) * qc[d, hh]) * scale
        num = num_aug[:, :dv]
        den = num_aug[:, dv:dv + 1]
        o_refs[d][:, hh * dv:(hh + 1) * dv] = (num / jnp.maximum(jnp.abs(den), jnp.exp(-m_j))).astype(of_ref.dtype)
        c_ref[d, hh] = decays[d][hh] * c_ref[d, hh] + kv[d, hh]
        m_ref[d, hh] = m_news[d][hh]


def _mlstm(proj, gates, *, batch, seq, q_off, k_off, v_off, heads, dk, dv):
    lc = SCAN_CHUNK
    nc = seq // lc
    assert dk == lc and seq % lc == 0 and dv % LANES == 0
    wqk, wv = heads * dk, heads * dv
    fwd = lambda b, c: b * nc + c
    bwd = lambda b, c: b * nc + nc - 1 - c

    def both(width, col):
        return [pl.BlockSpec((lc, width), lambda b, c: (fwd(b, c), col)),
                pl.BlockSpec((lc, width), lambda b, c: (bwd(b, c), col))]

    out = jax.ShapeDtypeStruct((batch * seq, wv), F32)
    return pl.pallas_call(
        functools.partial(_mlstm_kernel, nc=nc, heads=heads, dk=dk, dv=dv, scale=dk ** -0.5),
        out_shape=[out, out],
        grid=(batch, nc),
        in_specs=both(wqk, q_off // wqk) + both(wqk, k_off // wqk) + both(wv, v_off // wv) +
                 [pl.BlockSpec((None, 2, nc, 2 * heads, lc), lambda b, c: (b, 0, 0, 0, 0))],
        out_specs=both(wv, 0),
        scratch_shapes=[pltpu.VMEM((2, heads, dk, dv + LANES), F32), pltpu.VMEM((2, heads, 1, 1), F32)],
        compiler_params=_params(("parallel", "arbitrary")),
        name="mlstm_scan",
    )(proj, proj, proj, proj, proj, proj, gates)


def _gla_levels():
    lc = SCAN_CHUNK
    nlev = int(math.log2(lc)) + 1
    t = np.arange(lc)
    diff = np.zeros((2, nlev * lc, lc), np.float32)
    mask = np.zeros((2, nlev, lc, lc), np.float32)
    u = t[None, :]
    for d in range(2):
        cum = (u <= t[:, None]) if d == 0 else (u >= t[:, None])
        diff[d, :lc] = cum
        mask[d, 0] = np.eye(lc)
        for l in range(1, nlev):
            w = 1 << (l - 1)
            pair = t // (2 * w)
            second = (t % (2 * w)) >= w
            if d == 0:
                bd = pair * 2 * w + w - 1
                cum_bd = u <= bd[:, None]
                is_q, is_k = second, ~second
            else:
                bd = pair * 2 * w + w
                cum_bd = u >= bd[:, None]
                is_q, is_k = ~second, second
            diff[d, l * lc:(l + 1) * lc] = cum.astype(np.float32) - cum_bd.astype(np.float32)
            mask[d, l] = ((pair[:, None] == pair[None, :]) & is_q[:, None] & is_k[None, :])
    return diff, mask


def _gla_kernel(qf_ref, qb_ref, kf_ref, kb_ref, vf_ref, vb_ref, smf_ref, smb_ref, up_ref, bias_ref, d_ref,
                mask_ref, of_ref, ob_ref, s_ref, *, heads, dk, dv, nlev, scale):
    lc = SCAN_CHUNK
    c = pl.program_id(1)

    @pl.when(c == 0)
    def _():
        s_ref[...] = jnp.zeros_like(s_ref)

    dirs = (0, 1)
    q_refs, k_refs, v_refs = (qf_ref, qb_ref), (kf_ref, kb_ref), (vf_ref, vb_ref)
    sm_refs, o_refs = (smf_ref, smb_ref), (of_ref, ob_ref)
    hs = [slice(hh * dk, (hh + 1) * dk) for hh in range(heads)]

    zs = [_dot3(sm_refs[d][...], up_ref[d]) + bias_ref[d] for d in dirs]
    las = [_log_sigmoid(z) * (1.0 / GLA_TAU) for z in zs]
    xs = [sum(_dot(d_ref[d], piece) for piece in _split_bf16(las[d], EXACT_PIECES)) for d in dirs]
    g_rows = [jnp.sum(la, axis=0, keepdims=True) for la in las]

    qbs = [q_refs[d][...] for d in dirs]
    kbs = [k_refs[d][...] for d in dirs]
    qfs = [qb.astype(F32) for qb in qbs]
    kfs = [kb.astype(F32) for kb in kbs]
    atts = [[mask_ref[d, 0] * _dot_nt(qbs[d][:, s], kbs[d][:, s]) for s in hs] for d in dirs]
    for l in range(1, nlev):
        es = [jnp.exp(-jnp.abs(xs[d][l * lc:(l + 1) * lc])) for d in dirs]
        qes = [(qfs[d] * es[d]).astype(BF16) for d in dirs]
        kes = [(kfs[d] * es[d]).astype(BF16) for d in dirs]
        atts = [[att + mask_ref[d, l] * _dot_nt(qes[d][:, s], kes[d][:, s]) for att, s in zip(atts[d], hs)]
                for d in dirs]

    bs = [x[:lc] for x in xs]
    qss = [(qfs[d] * jnp.exp(bs[d])).astype(BF16) for d in dirs]
    kds = [(kfs[d] * jnp.exp(g_rows[d] - bs[d])).astype(BF16) for d in dirs]
    egs = [jnp.exp(g) for g in g_rows]
    for hh, s in enumerate(hs):
        for d in dirs:
            v = v_refs[d][:, hh * dv:(hh + 1) * dv]
            o = _dot(atts[d][hh].astype(BF16), v) + _dot_nt(qss[d][:, s], s_ref[d, hh].astype(BF16))
            o_refs[d][:, hh * dv:(hh + 1) * dv] = (o * scale).astype(o_refs[d].dtype)
            s_ref[d, hh] = egs[d][:, s] * s_ref[d, hh] + _dot_tn(v, kds[d][:, s])


def _gla(proj, small, up_pad, bias, *, batch, seq, q_off, k_off, v_off, heads, dk, dv):
    lc = SCAN_CHUNK
    nc = seq // lc
    assert dk == lc and seq % lc == 0
    diff, mask = _gla_levels()
    nlev = mask.shape[1]
    wqk, wv = heads * dk, heads * dv
    fwd = lambda b, c: b * nc + c
    bwd = lambda b, c: b * nc + nc - 1 - c

    def both(width, col):
        return [pl.BlockSpec((lc, width), lambda b, c: (fwd(b, c), col)),
                pl.BlockSpec((lc, width), lambda b, c: (bwd(b, c), col))]

    full = lambda shape: pl.BlockSpec(shape, lambda b, c: (0,) * len(shape))
    out = jax.ShapeDtypeStruct((batch * seq, wv), F32)
    return pl.pallas_call(
        functools.partial(_gla_kernel, heads=heads, dk=dk, dv=dv, nlev=nlev, scale=dk ** -0.5),
        out_shape=[out, out],
        grid=(batch, nc),
        in_specs=both(wqk, q_off // wqk) + both(wqk, k_off // wqk) + both(wv, v_off // wv) + both(LANES, 0) +
                 [full((2, LANES, wqk)), full((2, 1, wqk)), full((2, nlev * lc, lc)), full((2, nlev, lc, lc))],
        out_specs=both(wv, 0),
        scratch_shapes=[pltpu.VMEM((2, heads, dv, dk), F32)],
        compiler_params=_params(("parallel", "arbitrary")),
        name="gla_scan",
    )(proj, proj, proj, proj, proj, proj, small, small, up_pad, bias,
      jnp.asarray(diff, BF16), jnp.asarray(mask, F32))


def _headnorm_gate_kernel(hf_ref, hb_ref, g_ref, hn_ref, o_ref, *, heads, dv, act):
    h = hf_ref[...] + hb_ref[...]
    outs = []
    for hh in range(heads):
        blk = h[:, hh * dv:(hh + 1) * dv]
        r = lax.rsqrt(jnp.mean(blk * blk, axis=-1, keepdims=True) + EPS)
        outs.append(blk * r)
    hn = jnp.concatenate(outs, axis=1) * hn_ref[...]
    gate = g_ref[...].astype(F32)
    sig = 1.0 / (1.0 + jnp.exp(-gate))
    gate = sig if act == "sigmoid" else gate * sig
    o_ref[...] = (hn * gate).astype(o_ref.dtype)


def _headnorm_gate(h2, proj, gate_off, head_norm, *, heads, dv, act, tm=512):
    if isinstance(h2, (tuple, list)):
        hf, hb = h2
        m, width = hf.shape
        dir_specs = [pl.BlockSpec((tm, width), lambda i: (i, 0))] * 2
    else:
        hf = hb = h2
        _, m, width = h2.shape
        dir_specs = [pl.BlockSpec((None, tm, width), lambda i: (0, i, 0)),
                     pl.BlockSpec((None, tm, width), lambda i: (1, i, 0))]
    assert m % tm == 0
    return pl.pallas_call(
        functools.partial(_headnorm_gate_kernel, heads=heads, dv=dv, act=act),
        out_shape=jax.ShapeDtypeStruct((m, width), BF16),
        grid=(m // tm,),
        in_specs=dir_specs + [pl.BlockSpec((tm, width), lambda i: (i, gate_off // width)),
                              pl.BlockSpec((1, width), lambda i: (0, 0))],
        out_specs=pl.BlockSpec((tm, width), lambda i: (i, 0)),
        compiler_params=_params(("parallel",)),
        name="headnorm_gate",
    )(hf, hb, proj, head_norm.reshape(1, width).astype(F32))


def _cm_short_conv_kernel(x0_ref, x1_ref, x2_ref, w_ref, b_ref, o0_ref, o1_ref, o2_ref, s_ref, t_ref):
    seq, tc = x0_ref.shape
    half, n2 = o0_ref.shape[1], o0_ref.shape[2]
    for p, (x_ref, o_ref) in enumerate(((x0_ref, o0_ref), (x1_ref, o1_ref), (x2_ref, o2_ref))):
        x = x_ref[...].astype(F32)
        w = w_ref[p]
        bias = b_ref[p]
        s_ref[...] = (pltpu.roll(x, 1, axis=0) * w[0:1] + x * w[1:2]
                      + pltpu.roll(x, seq - 1, axis=0) * w[2:3] + bias)
        s_ref[0:1, :] = x[0:1] * w[1:2] + x[1:2] * w[2:3] + bias
        s_ref[seq - 1:seq, :] = x[seq - 2:seq - 1] * w[0:1] + x[seq - 1:seq] * w[1:2] + bias

        def flip(a, carry):
            slab = s_ref[pl.ds(pl.multiple_of(a * n2, n2), n2), :]
            t_ref[pl.ds(pl.multiple_of(a * tc, tc), tc), :] = slab.T
            return carry

        lax.fori_loop(0, half, flip, 0, unroll=min(4, half))

        def body(ch, carry, o_ref=o_ref):
            o_ref[ch] = t_ref[pl.ds(ch, half, stride=tc), :].astype(o_ref.dtype)
            return carry

        lax.fori_loop(0, tc, body, 0, unroll=GATHER_UNROLL)


def _cm_short_conv(proj, conv_w, conv_b, *, batch, seq, off, ch, n2):
    tc = LANES
    nct = ch // tc
    half = seq // n2
    w = conv_w.reshape(3, 3, nct, tc).transpose(1, 2, 0, 3).astype(F32)
    b = conv_b.reshape(3, nct, 1, tc).astype(F32)
    in_specs = [pl.BlockSpec((seq, tc), functools.partial(lambda b_, j, p: (b_, off // tc + p * nct + j), p=p))
                for p in range(3)]
    in_specs += [pl.BlockSpec((3, None, 3, tc), lambda b_, j: (0, j, 0, 0)),
                 pl.BlockSpec((3, None, 1, tc), lambda b_, j: (0, j, 0, 0))]
    out = jax.ShapeDtypeStruct((batch, ch, half, n2), BF16)
    return pl.pallas_call(
        _cm_short_conv_kernel,
        out_shape=[out, out, out],
        grid=(batch, nct),
        in_specs=in_specs,
        out_specs=[pl.BlockSpec((None, tc, half, n2), lambda b_, j: (b_, j, 0, 0))] * 3,
        scratch_shapes=[pltpu.VMEM((seq, tc), F32), pltpu.VMEM((half * tc, n2), F32)],
        compiler_params=_params(("parallel", "parallel")),
        name="hyena_short_conv",
    )(proj, proj, proj, w, b)


def _cm_taps_kernel(z_ref, w1_ref, b1_ref, w2_ref, b2_ref, w3_ref, f_ref, dl_ref, o_ref, t_ref, h_ref, *, emb, n2):
    @pl.when(pl.program_id(1) == 0)
    def _():
        f = f_ref[...]
        h = jnp.sin(f * (_dot3(w1_ref[...], z_ref[...]) + b1_ref[...]))
        h_ref[...] = jnp.sin(f * (_dot3(w2_ref[...], h) + b2_ref[...]))

    win = jnp.exp(-dl_ref[...] * z_ref[0:1, :]) * z_ref[emb:emb + 1, :]
    out = _dot3(w3_ref[...], h_ref[...]) * win
    gc = out.shape[0]
    tile = out.shape[1] // n2
    for s in range(tile):
        t_ref[s * gc:(s + 1) * gc, :] = out[:, s * n2:(s + 1) * n2]

    def body(ch, carry):
        o_ref[ch] = t_ref[pl.ds(ch, tile, stride=gc), :].astype(o_ref.dtype)
        return carry

    lax.fori_loop(0, gc, body, 0, unroll=GATHER_UNROLL)


def _cm_taps(seq, w1, b1, w2, b2, w3, freq, ch, n2, gc=256):
    emb, hid = w1.shape
    n1 = 2 * seq // n2
    half = n1 // 2
    tile = min(BF16_TILE_ROWS, half)
    order = w3.shape[1] // (2 * ch)
    lag = jnp.arange(2 * seq, dtype=jnp.int32)
    pos = jnp.where(lag < seq, lag, 2 * seq - lag)
    n = pos.astype(F32)
    t = n / (seq - 1)
    bands = jnp.linspace(1e-4, HYENA_BANDS - 1, HYENA_BANDS, dtype=F32)
    ang = (2.0 * math.pi * n / seq)[:, None] * bands[None, :]
    valid = (lag != seq).astype(F32)
    z = jnp.concatenate([t[:, None], jnp.cos(ang), -jnp.sin(ang), valid[:, None]], axis=-1)
    zt = jnp.pad(z, ((0, 0), (0, LANES - emb - 1))).T
    deltas = jnp.abs(jnp.linspace(HYENA_MIN_DECAY, HYENA_MAX_DECAY, ch, dtype=F32))
    dl = jnp.tile(deltas, order).reshape(order * ch, 1)
    pad_h = LANES - hid
    w1t = jnp.pad(w1.astype(F32), ((0, LANES - emb), (0, pad_h))).T
    w2t = jnp.pad(w2.astype(F32), ((0, pad_h), (0, pad_h))).T
    w3t = jnp.pad(w3.astype(F32), ((0, pad_h), (0, 0))).T.reshape(order, 2, ch, LANES)
    w3t = w3t.transpose(1, 0, 2, 3).reshape(2, order * ch, LANES)
    colv = lambda v: jnp.pad(v.astype(F32), (0, pad_h)).reshape(LANES, 1)
    gc = min(gc, order * ch)
    nat = n1 // tile
    full = lambda shape: pl.BlockSpec(shape, lambda a, j: (0, 0))
    return pl.pallas_call(
        functools.partial(_cm_taps_kernel, emb=emb, n2=n2),
        out_shape=jax.ShapeDtypeStruct((order * ch, n1, n2), BF16),
        grid=(nat, order * ch // gc),
        in_specs=[pl.BlockSpec((LANES, tile * n2), lambda a, j: (0, a)),
                  full((LANES, LANES)), full((LANES, 1)), full((LANES, LANES)), full((LANES, 1)),
                  pl.BlockSpec((None, gc, LANES), lambda a, j: (a // (nat // 2), j, 0)),
                  full((LANES, 1)),
                  pl.BlockSpec((gc, 1), lambda a, j: (j, 0))],
        out_specs=pl.BlockSpec((gc, tile, n2), lambda a, j: (j, a, 0)),
        scratch_shapes=[pltpu.VMEM((tile * gc, n2), F32), pltpu.VMEM((LANES, tile * n2), F32)],
        compiler_params=_params(("parallel", "arbitrary")),
        name="hyena_filters",
    )(zt, w1t, colv(b1), w2t, colv(b2), w3t, colv(freq), dl)


def _cm_tables(n1, n2):
    a = jnp.arange(n1, dtype=jnp.int32)
    ang1 = ((a[:, None] * a[None, :]) % n1).astype(F32) * (2.0 * math.pi / n1)
    f1r, f1i = jnp.cos(ang1), -jnp.sin(ang1)
    b = jnp.arange(n2, dtype=jnp.int32)
    angt = (a[:, None] * b[None, :]).astype(F32) * (2.0 * math.pi / (n1 * n2))
    tw = jnp.stack([jnp.cos(angt), -jnp.sin(angt)])
    ang2 = ((b[:, None] * b[None, :]) % n2).astype(F32) * (2.0 * math.pi / n2)
    wr, wi = jnp.cos(ang2), -jnp.sin(ang2)
    wb = jnp.concatenate([jnp.concatenate([wr, wi], axis=1),
                          jnp.concatenate([-wi, wr], axis=1)], axis=0).astype(BF16)
    return f1r, f1i, tw, wb, wb.T


def _cm_forward(x_ref, f1, twr, twi, wb, n1):
    ys = []
    for g in range(x_ref.shape[0]):
        p = _dot(f1, x_ref[g])
        pr, pi = p[:n1], p[n1:]
        ys.append(jnp.concatenate([pr * twr - pi * twi, pr * twi + pi * twr], axis=1))
    y = jnp.concatenate(ys, axis=0).astype(BF16)
    return _dot(y, wb)


def _cm_spec_kernel(t_ref, f1_ref, tw_ref, wb_ref, o_ref, *, n1, inv_n):
    z = _cm_forward(t_ref, f1_ref[...], tw_ref[0], tw_ref[1], wb_ref[...], n1)
    o_ref[...] = (z * inv_n).reshape(o_ref.shape)


def _cm_spectrum(taps, f1, tw, wb, *, group=32):
    nch, n1, n2 = taps.shape
    group = min(group, nch)
    full2 = lambda shape: pl.BlockSpec(shape, lambda j: (0,) * len(shape))
    return pl.pallas_call(
        functools.partial(_cm_spec_kernel, n1=n1, inv_n=1.0 / (n1 * n2)),
        out_shape=jax.ShapeDtypeStruct((nch, n1, 2 * n2), F32),
        grid=(nch // group,),
        in_specs=[pl.BlockSpec((group, n1, n2), lambda j: (j, 0, 0)),
                  full2((2 * n1, n1)), full2((2, n1, n2)), full2((2 * n2, 2 * n2))],
        out_specs=pl.BlockSpec((group, n1, 2 * n2), lambda j: (j, 0, 0)),
        compiler_params=_params(("parallel",)),
        name="hyena_filter_spectrum",
    )(taps, f1, tw, wb)


def _cm_conv_kernel(x_ref, gate_ref, kf_ref, skip_ref, f1_ref, f1h_ref, tw_ref, wb_ref, wbt_ref, o_ref, *, n1, n2):
    group = x_ref.shape[0]
    twr, twi = tw_ref[0], tw_ref[1]
    z = _cm_forward(x_ref, f1_ref[...], twr, twi, wb_ref[...], n1)
    kf = kf_ref[...].reshape(group * n1, 2 * n2)
    zr, zi, kr, ki = z[:, :n2], z[:, n2:], kf[:, :n2], kf[:, n2:]
    s = jnp.concatenate([zr * kr - zi * ki, zr * ki + zi * kr], axis=1).astype(BF16)
    v = _dot(s, wbt_ref[...])
    f1h = f1h_ref[...]
    for g in range(group):
        vr, vi = v[g * n1:(g + 1) * n1, :n2], v[g * n1:(g + 1) * n1, n2:]
        u = jnp.concatenate([vr * twr + vi * twi, vi * twr - vr * twi], axis=0).astype(BF16)
        y = _dot(f1h, u)
        y = y + skip_ref[g] * x_ref[g].astype(F32)
        o_ref[g] = (gate_ref[g].astype(F32) * y).astype(o_ref.dtype)


def _cm_conv(x, gate, kf, skip, f1d, f1h, tw, wb, wbt, *, kf_off, out_dtype, group=32):
    batch, ch, half, n2 = x.shape
    n1 = 2 * half
    group = min(group, ch)
    sig = pl.BlockSpec((None, group, half, n2), lambda j, b: (b, j, 0, 0))
    full2 = lambda shape: pl.BlockSpec(shape, lambda j, b: (0,) * len(shape))
    skip_b = jnp.broadcast_to(skip.astype(F32)[:, None, None], (ch, 1, n2))
    return pl.pallas_call(
        functools.partial(_cm_conv_kernel, n1=n1, n2=n2),
        out_shape=jax.ShapeDtypeStruct((batch, ch, half, n2), out_dtype),
        grid=(ch // group, batch),
        in_specs=[sig, sig,
                  pl.BlockSpec((group, n1, 2 * n2), lambda j, b: (kf_off // group + j, 0, 0)),
                  pl.BlockSpec((group, 1, n2), lambda j, b: (j, 0, 0)),
                  full2((2 * n1, half)), full2((half, 2 * n1)), full2((2, n1, n2)),
                  full2((2 * n2, 2 * n2)), full2((2 * n2, 2 * n2))],
        out_specs=sig,
        compiler_params=_params(("parallel", "parallel")),
        name="hyena_fft_conv",
    )(x, gate, kf, skip_b, f1d, f1h, tw, wb, wbt)


def _cm_to_time_major_kernel(y_ref, o_ref, *, half):
    tc = o_ref.shape[1]
    n2 = y_ref.shape[1]

    def body(a, carry):
        slab = y_ref[pl.ds(a, tc, stride=half), :]
        o_ref[pl.ds(pl.multiple_of(a * n2, n2), n2), :] = slab.T.astype(o_ref.dtype)
        return carry

    lax.fori_loop(0, half, body, 0, unroll=min(4, half))


def _cm_to_time_major(y, *, batch, seq, ch):
    half, n2 = y.shape[2], y.shape[3]
    tc = LANES
    return pl.pallas_call(
        functools.partial(_cm_to_time_major_kernel, half=half),
        out_shape=jax.ShapeDtypeStruct((batch * seq, ch), BF16),
        grid=(batch, ch // tc),
        in_specs=[pl.BlockSpec((None, tc * half, n2), lambda b, j: (b, j, 0))],
        out_specs=pl.BlockSpec((seq, tc), lambda b, j: (b, j)),
        compiler_params=_params(("parallel", "parallel")),
        name="hyena_to_time_major",
    )(y.reshape(batch, ch * half, n2))


def _hyena_cm(proj, hy_off, conv_w, conv_b, w1, b1, w2, b2, w3, freq, skip, *, batch, seq, ch):
    n2 = FFT_N2
    n1 = 2 * seq // n2
    half = n1 // 2
    v, x1, x2 = _cm_short_conv(proj, conv_w, conv_b, batch=batch, seq=seq, off=hy_off, ch=ch, n2=n2)
    taps = _cm_taps(seq, w1, b1, w2, b2, w3, freq, ch, n2)
    f1r, f1i, tw, wb, wbt = _cm_tables(n1, n2)
    fstack = jnp.concatenate([f1r, f1i], axis=0)
    kf = _cm_spectrum(taps, fstack.astype(BF16), tw, wb)
    f1d = fstack[:, :half].astype(BF16)
    f1h = jnp.concatenate([f1r[:half], f1i[:half]], axis=1).astype(BF16)
    zcur = v
    for o, gate in enumerate((x1, x2)):
        last = o == HYENA_ORDER - 1
        zcur = _cm_conv(zcur, gate, kf, skip[o], f1d, f1h, tw, wb, wbt, kf_off=o * ch,
                        out_dtype=F32 if last else BF16)
    return _cm_to_time_major(zcur, batch=batch, seq=seq, ch=ch)


def _merge_kernel(ya_ref, yb_ref, yc_ref, ga_ref, gb_ref, gc_ref, w_ref, o_ref):
    acc = None
    for n, (y_ref, g_ref) in enumerate(((ya_ref, ga_ref), (yb_ref, gb_ref), (yc_ref, gc_ref))):
        gate = 1.0 / (1.0 + jnp.exp(-g_ref[...].astype(F32)))
        term = gate * _dot(y_ref[...], w_ref[n])
        acc = term if acc is None else acc + term
    o_ref[...] = acc.astype(o_ref.dtype)


def _merge(ys, proj, gate_off, w_branch, layer, *, d_model, tm=1024, tn=1024):
    m, width = ys[0].shape
    tm, tn = min(tm, m), min(tn, d_model)
    y_spec = pl.BlockSpec((tm, width), lambda i, j: (i, 0))
    g_specs = [pl.BlockSpec((tm, tn), functools.partial(
        lambda i, j, n: (i, (gate_off + n * d_model) // tn + j), n=n)) for n in range(N_BRANCH)]
    return pl.pallas_call(
        _merge_kernel,
        out_shape=jax.ShapeDtypeStruct((m, d_model), BF16),
        grid=(m // tm, d_model // tn),
        in_specs=[y_spec, y_spec, y_spec] + g_specs +
                 [pl.BlockSpec((None, N_BRANCH, width, tn), lambda i, j: (layer, 0, 0, j))],
        out_specs=pl.BlockSpec((tm, tn), lambda i, j: (i, j)),
        compiler_params=_params(("parallel", "parallel")),
        name="branch_merge",
    )(*ys, proj, proj, proj, w_branch)


def _xattn_kernel(q_ref, k_ref, v_ref, o_ref, *, scale):
    s = _dot_nt(q_ref[...], k_ref[...]) * scale
    s = s - jnp.max(s, axis=-1, keepdims=True)
    p = jnp.exp(s)
    l = jnp.sum(p, axis=-1, keepdims=True)
    o_ref[...] = (_dot(p.astype(BF16), v_ref[...]) / l).astype(o_ref.dtype)


def _xattn(q, kv, *, batch, seq, mem, heads, hd, tq=1024):
    tq = min(tq, seq)
    nq = seq // tq
    return pl.pallas_call(
        functools.partial(_xattn_kernel, scale=hd ** -0.5),
        out_shape=jax.ShapeDtypeStruct(q.shape, BF16),
        grid=(batch, heads, nq),
        in_specs=[pl.BlockSpec((tq, hd), lambda b, h, i: (b * nq + i, h)),
                  pl.BlockSpec((mem, hd), lambda b, h, i: (b, h)),
                  pl.BlockSpec((mem, hd), lambda b, h, i: (b, heads + h))],
        out_specs=pl.BlockSpec((tq, hd), lambda b, h, i: (b * nq + i, h)),
        compiler_params=_params(("parallel", "parallel", "parallel")),
        name="xattn_core",
    )(q, kv, kv)


def _in_proj_layout(d):
    mix = d // 4
    m_dk = mix // MLSTM_HEADS // 2
    g_dk = mix // GLA_HEADS // 2
    sizes = (MLSTM_HEADS * m_dk, MLSTM_HEADS * m_dk, mix, mix, 4 * MLSTM_HEADS,
             GLA_HEADS * g_dk, GLA_HEADS * g_dk, mix, mix, 2 * GLA_RANK, 3 * mix, N_BRANCH * d)
    offs = [int(o) for o in np.concatenate([[0], np.cumsum(sizes)])]
    runs = ((0, 1, 2, 3), (5, 6, 7, 8), (10, 11))
    narrow = (4, 9)
    where = {i: (r, offs[i] - offs[run[0]]) for r, run in enumerate(runs) for i in run}
    return sizes, offs, runs, narrow, where


def kernel(x, mem, norm_gains, final_norm, w_in, b_in, mlstm_head_norm, gla_decay_up, gla_decay_bias, gla_head_norm, hyena_conv_w, hyena_conv_b, hyena_ffn_w1, hyena_ffn_b1, hyena_ffn_w2, hyena_ffn_b2, hyena_ffn_w3, hyena_freq, hyena_skip, w_branch, w_out, xattn_wq, xattn_wkv, xattn_wo, mlp_w1, mlp_w2):
    batch, seq, d = x.shape
    mem_tokens = mem.shape[1]
    depth = norm_gains.shape[0]
    mix = d // 4
    mh, gh = MLSTM_HEADS, GLA_HEADS
    m_dv, g_dv = mix // mh, mix // gh
    m_dk, g_dk = m_dv // 2, g_dv // 2
    nc = seq // SCAN_CHUNK

    sizes, offs, runs, narrow, where = _in_proj_layout(d)
    seg = lambda a, i: a[..., offs[i]:offs[i + 1]]
    w_in_rows = jnp.swapaxes(w_in, 1, 2).astype(BF16)
    run_rows = [(offs[run[0]], sum(sizes[i] for i in run)) for run in runs]
    b_runs = [b_in[:, r0:r0 + n][:, None, :].astype(F32) for r0, n in run_rows]
    n_small = sum(sizes[i] for i in narrow)
    w_small = jnp.pad(jnp.concatenate([w_in_rows[:, offs[i]:offs[i + 1], :] for i in narrow], axis=1),
                      ((0, 0), (0, LANES - n_small), (0, 0)))
    b_small = jnp.pad(jnp.concatenate([seg(b_in, i) for i in narrow], axis=1),
                      ((0, 0), (0, LANES - n_small)))[:, None, :].astype(F32)
    col = lambda i: where[i][1]
    wb16, wo16 = w_branch.astype(BF16), w_out.astype(BF16)
    wq16, wkv16, wxo16 = xattn_wq.astype(BF16), xattn_wkv.astype(BF16), xattn_wo.astype(BF16)
    w1_16, w2_16 = mlp_w1.astype(BF16), mlp_w2.astype(BF16)

    h = x.reshape(batch * seq, d)
    memf = mem.reshape(batch * mem_tokens, d)
    for l in range(depth):
        g = norm_gains[l]
        xn = _rmsnorm(h, g[0], BF16)
        proj_m, proj_g, proj_h = [
            _mm(xn, w_in_rows, l, bias=b_run, out_dtype=BF16, tn=1024, w_rows=rw, name="in_proj")
            for rw, b_run in zip(run_rows, b_runs)]
        small = _mm(xn, w_small, l, bias=b_small, out_dtype=F32, w_rows=(0, LANES), name="in_proj_gates")

        mg = small[:, :4 * mh].reshape(batch, nc, SCAN_CHUNK, 2, 2, mh)
        mgates = mg.transpose(0, 3, 1, 5, 4, 2).reshape(batch, 2, nc, 2 * mh, SCAN_CHUNK)
        h_m = _mlstm(proj_m, mgates, batch=batch, seq=seq, q_off=col(0), k_off=col(1), v_off=col(2),
                     heads=mh, dk=m_dk, dv=m_dv)
        y_a = _headnorm_gate(h_m, proj_m, col(3), mlstm_head_norm[l], heads=mh, dv=m_dv, act="sigmoid")

        up_pad = jnp.zeros((2, LANES, gh * g_dk), F32)
        for dr in range(2):
            r0 = 4 * mh + dr * GLA_RANK
            up_pad = up_pad.at[dr, r0:r0 + GLA_RANK, :].set(gla_decay_up[l, dr].astype(F32))
        gbias = gla_decay_bias[l].reshape(2, 1, gh * g_dk).astype(F32)
        h_g = _gla(proj_g, small, up_pad, gbias, batch=batch, seq=seq, q_off=col(5), k_off=col(6),
                   v_off=col(7), heads=gh, dk=g_dk, dv=g_dv)
        y_b = _headnorm_gate(h_g, proj_g, col(8), gla_head_norm[l], heads=gh, dv=g_dv, act="silu")

        y_c = _hyena_cm(proj_h, col(10), hyena_conv_w[l], hyena_conv_b[l], hyena_ffn_w1[l], hyena_ffn_b1[l],
                        hyena_ffn_w2[l], hyena_ffn_b2[l], hyena_ffn_w3[l], hyena_freq[l], hyena_skip[l],
                        batch=batch, seq=seq, ch=mix)

        merged = _merge((y_a, y_b, y_c), proj_h, col(11), wb16, l, d_model=d)
        h = _mm(merged, wo16, l, resid=h, out_dtype=F32, tn=1024, name="mixer_out")

        xn = _rmsnorm(h, g[1], BF16)
        memn = _rmsnorm(memf, g[2], BF16)
        q = _mm(xn, wq16, l, out_dtype=BF16, tn=1024, name="xattn_q")
        kv = _mm(memn, wkv16, l, out_dtype=BF16, tn=1024, name="xattn_kv")
        o = _xattn(q, kv, batch=batch, seq=seq, mem=mem_tokens, heads=XATTN_HEADS, hd=d // XATTN_HEADS)
        h = _mm(o, wxo16, l, resid=h, out_dtype=F32, tn=1024, name="xattn_out")

        xn = _rmsnorm(h, g[3], BF16)
        hid = _mm(xn, w1_16, l, act="relu2", out_dtype=BF16, tn=1024, name="mlp_up")
        h = _mm(hid, w2_16, l, resid=h, out_dtype=F32, tm=512, tn=256, name="mlp_down")
    return _rmsnorm(h, final_norm, x.dtype).reshape(batch, seq, d)
```

```python
import functools
import math

import numpy as np
import jax
import jax.numpy as jnp
from jax import lax
from jax.experimental import pallas as pl
from jax.experimental.pallas import tpu as pltpu

F32 = jnp.float32
BF16 = jnp.bfloat16

N_BRANCH = 3
MLSTM_HEADS = 4
GLA_HEADS = 4
GLA_RANK = 16
GLA_TAU = 16.0
HYENA_ORDER = 2
HYENA_BANDS = 16
HYENA_MIN_DECAY = math.log(1e-2) / 1.5
HYENA_MAX_DECAY = math.log(1e-2) / 0.3
XATTN_HEADS = 4
EPS = 1e-6

LANES = 128
BF16_TILE_ROWS = 16
VMEM_LIMIT_BYTES = 56 * 1024 * 1024

SCAN_CHUNK = 128
FFT_N2 = 128
NEG_BIG = -1e30
GATHER_UNROLL = 8
EXACT_PIECES = 2


def _params(sem):
    return pltpu.CompilerParams(dimension_semantics=sem, vmem_limit_bytes=VMEM_LIMIT_BYTES)


def _log_sigmoid(x):
    return -(jnp.maximum(-x, 0.0) + jnp.log(1.0 + jnp.exp(-jnp.abs(x))))


def _split_bf16(a, parts):
    out = []
    r = a
    for _ in range(parts):
        p = r.astype(BF16)
        out.append(p)
        r = r - p.astype(F32)
    return out


def _dot(a, b):
    return jnp.dot(a, b, preferred_element_type=F32)


def _dot_nt(a, b):
    return lax.dot_general(a, b, (((1,), (1,)), ((), ())), preferred_element_type=F32)


def _dot_tn(a, b):
    return lax.dot_general(a, b, (((0,), (0,)), ((), ())), preferred_element_type=F32)


def _dot3(a, b):
    a_hi, a_lo = _split_bf16(a, 2)
    b_hi, b_lo = _split_bf16(b, 2)
    return _dot(a_hi, b_hi) + _dot(a_hi, b_lo) + _dot(a_lo, b_hi)


def _rmsnorm_kernel(x_ref, g_ref, o_ref):
    x = x_ref[...].astype(F32)
    r = lax.rsqrt(jnp.mean(x * x, axis=-1, keepdims=True) + EPS)
    o_ref[...] = ((x * r) * g_ref[...]).astype(o_ref.dtype)


def _rmsnorm(x, g, out_dtype, tm=512):
    m, d = x.shape
    tm = min(tm, m)
    return pl.pallas_call(
        _rmsnorm_kernel,
        out_shape=jax.ShapeDtypeStruct((m, d), out_dtype),
        grid=(m // tm,),
        in_specs=[pl.BlockSpec((tm, d), lambda i: (i, 0)),
                  pl.BlockSpec((1, d), lambda i: (0, 0))],
        out_specs=pl.BlockSpec((tm, d), lambda i: (i, 0)),
        compiler_params=_params(("parallel",)),
        name="rmsnorm",
    )(x, g.reshape(1, d).astype(F32))


def _mm_kernel(*refs, nk, act, has_bias, has_resid, w_rows):
    x_ref, w_ref = refs[0], refs[1]
    product = (lambda a, b: _dot_nt(a, b[0])) if w_rows else _dot
    idx = 2
    b_ref = r_ref = None
    if has_bias:
        b_ref = refs[idx]
        idx += 1
    if has_resid:
        r_ref = refs[idx]
        idx += 1
    o_ref = refs[idx]

    def epilogue(acc):
        if has_bias:
            acc = acc + b_ref[...]
        if act == "relu2":
            acc = jnp.square(jnp.maximum(acc, 0.0))
        if has_resid:
            acc = acc + r_ref[...]
        o_ref[...] = acc.astype(o_ref.dtype)

    if nk == 1:
        epilogue(product(x_ref[...], w_ref[...]))
    else:
        acc_ref = refs[idx + 1]
        k = pl.program_id(2)

        @pl.when(k == 0)
        def _():
            acc_ref[...] = jnp.zeros_like(acc_ref)

        acc_ref[...] += product(x_ref[...], w_ref[...])

        @pl.when(k == nk - 1)
        def _():
            epilogue(acc_ref[...])


def _mm(x, w, layer, *, bias=None, resid=None, act=None, out_dtype=BF16, tm=1024, tn=512, tk=None,
        w_rows=None, name="mm"):
    m, kdim = x.shape
    n = w_rows[1] if w_rows else w.shape[2]
    tm, tn = min(tm, m), min(tn, n)
    tk = kdim if tk is None else min(tk, kdim)
    nk = kdim // tk
    assert m % tm == 0 and n % tn == 0 and kdim % tk == 0
    if w_rows:
        row0 = w_rows[0]
        assert row0 % BF16_TILE_ROWS == 0
        w_spec = pl.BlockSpec((pl.Element(1), pl.Element(tn), pl.Element(tk)),
                              lambda i, j, k: (layer, pl.multiple_of(row0 + j * tn, BF16_TILE_ROWS), k * tk))
    else:
        w_spec = pl.BlockSpec((None, tk, tn), lambda i, j, k: (layer, k, j))
    in_specs = [pl.BlockSpec((tm, tk), lambda i, j, k: (i, k)), w_spec]
    args = [x, w]
    if bias is not None:
        in_specs.append(pl.BlockSpec((None, 1, tn), lambda i, j, k: (layer, 0, j)))
        args.append(bias)
    if resid is not None:
        in_specs.append(pl.BlockSpec((tm, tn), lambda i, j, k: (i, j)))
        args.append(resid)
    scratch = [pltpu.VMEM((tm, tn), F32)] if nk > 1 else []
    return pl.pallas_call(
        functools.partial(_mm_kernel, nk=nk, act=act, has_bias=bias is not None,
                          has_resid=resid is not None, w_rows=w_rows is not None),
        out_shape=jax.ShapeDtypeStruct((m, n), out_dtype),
        grid=(m // tm, n // tn, nk),
        in_specs=in_specs,
        out_specs=pl.BlockSpec((tm, tn), lambda i, j, k: (i, j)),
        scratch_shapes=scratch,
        compiler_params=_params(("parallel", "parallel", "arbitrary")),
        name=name,
    )(*args)


def _rows_to_cols(sel, rows):
    return sum(_dot_nt(sel, piece) for piece in _split_bf16(rows, EXACT_PIECES))


def _mlstm_kernel(qf_ref, qb_ref, kf_ref, kb_ref, vf_ref, vb_ref, g_ref, of_ref, ob_ref, c_ref, m_ref, *,
                  nc, heads, dk, dv, scale):
    lc = SCAN_CHUNK
    c = pl.program_id(1)

    @pl.when(c == 0)
    def _():
        c_ref[...] = jnp.zeros_like(c_ref)
        m_ref[...] = jnp.zeros_like(m_ref)

    dirs = (0, 1)
    q_refs, k_refs, v_refs, o_refs = (qf_ref, qb_ref), (kf_ref, kb_ref), (vf_ref, vb_ref), (of_ref, ob_ref)
    row = lax.broadcasted_iota(jnp.int32, (lc, lc), 0)
    col = lax.broadcasted_iota(jnp.int32, (lc, lc), 1)
    tris = (col <= row, col >= row)
    tri_bs = [jnp.where(t, 1.0, 0.0).astype(BF16) for t in tris]
    eye_b = jnp.where(row == col, 1.0, 0.0).astype(BF16)
    ones_blk = (lax.broadcasted_iota(jnp.int32, (lc, LANES), 1) == 0).astype(BF16)

    gls = [g_ref[0, c], g_ref[1, nc - 1 - c]]
    lf_alls = [_log_sigmoid(gl) for gl in gls]
    lf_pieces = [_split_bf16(lf, EXACT_PIECES) for lf in lf_alls]
    b_rows = [sum(_dot_nt(piece, tri_bs[d]) for piece in lf_pieces[d]) for d in dirs]
    b_cols = [sum(_dot_nt(tri_bs[d], piece) for piece in lf_pieces[d]) for d in dirs]
    g_alls = [jnp.sum(lf, axis=1, keepdims=True) for lf in lf_alls]
    wk_rows, m_news, decays = ([], []), ([], []), ([], [])
    for d in dirs:
        for hh in range(heads):
            ig = gls[d][2 * hh:2 * hh + 1, :]
            g = g_alls[d][2 * hh + 1:2 * hh + 2, :]
            m_prev = m_ref[d, hh]
            a_row = g - b_rows[d][2 * hh + 1:2 * hh + 2, :] + ig
            m_new = jnp.maximum(g + m_prev, jnp.max(a_row, axis=1, keepdims=True))
            wk_rows[d].append(jnp.exp(a_row - m_new))
            m_news[d].append(m_new)
            decays[d].append(jnp.exp(g + m_prev - m_new))
    wk_cols = [_rows_to_cols(eye_b, jnp.concatenate(wk_rows[d] + [jnp.zeros((heads, lc), F32)], axis=0))
               for d in dirs]

    chains = [(d, hh) for hh in range(heads) for d in dirs]
    qs = {ch: q_refs[ch[0]][:, ch[1] * dk:(ch[1] + 1) * dk] for ch in chains}
    ks = {ch: k_refs[ch[0]][:, ch[1] * dk:(ch[1] + 1) * dk] for ch in chains}
    v_augs = {ch: jnp.concatenate([v_refs[ch[0]][:, ch[1] * dv:(ch[1] + 1) * dv], ones_blk], axis=1)
              for ch in chains}
    qk = {ch: _dot_nt(qs[ch], ks[ch]) for ch in chains}
    qc = {ch: _dot(qs[ch], c_ref[ch[0], ch[1]].astype(BF16)) for ch in chains}
    kv = {ch: _dot_tn((ks[ch].astype(F32) * wk_cols[ch[0]][:, ch[1]:ch[1] + 1]).astype(BF16), v_augs[ch])
          for ch in chains}
    for d, hh in chains:
        b_col = b_cols[d][:, 2 * hh + 1:2 * hh + 2]
        log_d = jnp.where(tris[d], b_col - b_rows[d][2 * hh + 1:2 * hh + 2, :] + gls[d][2 * hh:2 * hh + 1, :],
                          NEG_BIG)
        m_inter = b_col + m_ref[d, hh]
        m_j = jnp.maximum(m_inter, jnp.max(log_d, axis=1, keepdims=True))
        smat = (qk[d, hh] * jnp.exp(log_d - m_j)).astype(BF16)
        num_aug = (_dot(smat, v_augs[d, hh]) + jnp.exp(m_inter - m_j) * qc[d, hh]) * scale
        num = num_aug[:, :dv]
        den = num_aug[:, dv:dv + 1]
        o_refs[d][:, hh * dv:(hh + 1) * dv] = (num / jnp.maximum(jnp.abs(den), jnp.exp(-m_j))).astype(of_ref.dtype)
        c_ref[d, hh] = decays[d][hh] * c_ref[d, hh] + kv[d, hh]
        m_ref[d, hh] = m_news[d][hh]


def _mlstm(proj, gates, *, batch, seq, q_off, k_off, v_off, heads, dk, dv):
    lc = SCAN_CHUNK
    nc = seq // lc
    assert dk == lc and seq % lc == 0 and dv % LANES == 0
    wqk, wv = heads * dk, heads * dv
    fwd = lambda b, c: b * nc + c
    bwd = lambda b, c: b * nc + nc - 1 - c

    def both(width, col):
        return [pl.BlockSpec((lc, width), lambda b, c: (fwd(b, c), col)),
                pl.BlockSpec((lc, width), lambda b, c: (bwd(b, c), col))]

    out = jax.ShapeDtypeStruct((batch * seq, wv), F32)
    return pl.pallas_call(
        functools.partial(_mlstm_kernel, nc=nc, heads=heads, dk=dk, dv=dv, scale=dk ** -0.5),
        out_shape=[out, out],
        grid=(batch, nc),
        in_specs=both(wqk, q_off // wqk) + both(wqk, k_off // wqk) + both(wv, v_off // wv) +
                 [pl.BlockSpec((None, 2, nc, 2 * heads, lc), lambda b, c: (b, 0, 0, 0, 0))],
        out_specs=both(wv, 0),
        scratch_shapes=[pltpu.VMEM((2, heads, dk, dv + LANES), F32), pltpu.VMEM((2, heads, 1, 1), F32)],
        compiler_params=_params(("parallel", "arbitrary")),
        name="mlstm_scan",
    )(proj, proj, proj, proj, proj, proj, gates)


def _gla_levels():
    lc = SCAN_CHUNK
    nlev = int(math.log2(lc)) + 1
    t = np.arange(lc)
    diff = np.zeros((2, nlev * lc, lc), np.float32)
    mask = np.zeros((2, nlev, lc, lc), np.float32)
    u = t[None, :]
    for d in range(2):
        cum = (u <= t[:, None]) if d == 0 else (u >= t[:, None])
        diff[d, :lc] = cum
        mask[d, 0] = np.eye(lc)
        for l in range(1, nlev):
            w = 1 << (l - 1)
            pair = t // (2 * w)
            second = (t % (2 * w)) >= w
            if d == 0:
                bd = pair * 2 * w + w - 1
                cum_bd = u <= bd[:, None]
                is_q, is_k = second, ~second
            else:
                bd = pair * 2 * w + w
                cum_bd = u >= bd[:, None]
                is_q, is_k = ~second, second
            diff[d, l * lc:(l + 1) * lc] = cum.astype(np.float32) - cum_bd.astype(np.float32)
            mask[d, l] = ((pair[:, None] == pair[None, :]) & is_q[:, None] & is_k[None, :])
    return diff, mask


def _gla_kernel(qf_ref, qb_ref, kf_ref, kb_ref, vf_ref, vb_ref, smf_ref, smb_ref, up_ref, bias_ref, d_ref,
                mask_ref, of_ref, ob_ref, s_ref, *, heads, dk, dv, nlev, scale):
    lc = SCAN_CHUNK
    c = pl.program_id(1)

    @pl.when(c == 0)
    def _():
        s_ref[...] = jnp.zeros_like(s_ref)

    dirs = (0, 1)
    q_refs, k_refs, v_refs = (qf_ref, qb_ref), (kf_ref, kb_ref), (vf_ref, vb_ref)
    sm_refs, o_refs = (smf_ref, smb_ref), (of_ref, ob_ref)
    hs = [slice(hh * dk, (hh + 1) * dk) for hh in range(heads)]

    zs = [_dot3(sm_refs[d][...], up_ref[d]) + bias_ref[d] for d in dirs]
    las = [_log_sigmoid(z) * (1.0 / GLA_TAU) for z in zs]
    xs = [sum(_dot(d_ref[d], piece) for piece in _split_bf16(las[d], EXACT_PIECES)) for d in dirs]
    g_rows = [jnp.sum(la, axis=0, keepdims=True) for la in las]

    qbs = [q_refs[d][...] for d in dirs]
    kbs = [k_refs[d][...] for d in dirs]
    qfs = [qb.astype(F32) for qb in qbs]
    kfs = [kb.astype(F32) for kb in kbs]
    atts = [[mask_ref[d, 0] * _dot_nt(qbs[d][:, s], kbs[d][:, s]) for s in hs] for d in dirs]
    for l in range(1, nlev):
        es = [jnp.exp(-jnp.abs(xs[d][l * lc:(l + 1) * lc])) for d in dirs]
        qes = [(qfs[d] * es[d]).astype(BF16) for d in dirs]
        kes = [(kfs[d] * es[d]).astype(BF16) for d in dirs]
        atts = [[att + mask_ref[d, l] * _dot_nt(qes[d][:, s], kes[d][:, s]) for att, s in zip(atts[d], hs)]
                for d in dirs]

    bs = [x[:lc] for x in xs]
    qss = [(qfs[d] * jnp.exp(bs[d])).astype(BF16) for d in dirs]
    kds = [(kfs[d] * jnp.exp(g_rows[d] - bs[d])).astype(BF16) for d in dirs]
    egs = [jnp.exp(g) for g in g_rows]
    for hh, s in enumerate(hs):
        for d in dirs:
            v = v_refs[d][:, hh * dv:(hh + 1) * dv]
            o = _dot(atts[d][hh].astype(BF16), v) + _dot_nt(qss[d][:, s], s_ref[d, hh].astype(BF16))
            o_refs[d][:, hh * dv:(hh + 1) * dv] = (o * scale).astype(o_refs[d].dtype)
            s_ref[d, hh] = egs[d][:, s] * s_ref[d, hh] + _dot_tn(v, kds[d][:, s])


def _gla(proj, small, up_pad, bias, *, batch, seq, q_off, k_off, v_off, heads, dk, dv):
    lc = SCAN_CHUNK
    nc = seq // lc
    assert dk == lc and seq % lc == 0
    diff, mask = _gla_levels()
    nlev = mask.shape[1]
    wqk, wv = heads * dk, heads * dv
    fwd = lambda b, c: b * nc + c
    bwd = lambda b, c: b * nc + nc - 1 - c

    def both(width, col):
        return [pl.BlockSpec((lc, width), lambda b, c: (fwd(b, c), col)),
                pl.BlockSpec((lc, width), lambda b, c: (bwd(b, c), col))]

    full = lambda shape: pl.BlockSpec(shape, lambda b, c: (0,) * len(shape))
    out = jax.ShapeDtypeStruct((batch * seq, wv), F32)
    return pl.pallas_call(
        functools.partial(_gla_kernel, heads=heads, dk=dk, dv=dv, nlev=nlev, scale=dk ** -0.5),
        out_shape=[out, out],
        grid=(batch, nc),
        in_specs=both(wqk, q_off // wqk) + both(wqk, k_off // wqk) + both(wv, v_off // wv) + both(LANES, 0) +
                 [full((2, LANES, wqk)), full((2, 1, wqk)), full((2, nlev * lc, lc)), full((2, nlev, lc, lc))],
        out_specs=both(wv, 0),
        scratch_shapes=[pltpu.VMEM((2, heads, dv, dk), F32)],
        compiler_params=_params(("parallel", "arbitrary")),
        name="gla_scan",
    )(proj, proj, proj, proj, proj, proj, small, small, up_pad, bias,
      jnp.asarray(diff, BF16), jnp.asarray(mask, F32))


def _headnorm_gate_kernel(hf_ref, hb_ref, g_ref, hn_ref, o_ref, *, heads, dv, act):
    h = hf_ref[...] + hb_ref[...]
    outs = []
    for hh in range(heads):
        blk = h[:, hh * dv:(hh + 1) * dv]
        r = lax.rsqrt(jnp.mean(blk * blk, axis=-1, keepdims=True) + EPS)
        outs.append(blk * r)
    hn = jnp.concatenate(outs, axis=1) * hn_ref[...]
    gate = g_ref[...].astype(F32)
    sig = 1.0 / (1.0 + jnp.exp(-gate))
    gate = sig if act == "sigmoid" else gate * sig
    o_ref[...] = (hn * gate).astype(o_ref.dtype)


def _headnorm_gate(h2, proj, gate_off, head_norm, *, heads, dv, act, tm=512):
    if isinstance(h2, (tuple, list)):
        hf, hb = h2
        m, width = hf.shape
        dir_specs = [pl.BlockSpec((tm, width), lambda i: (i, 0))] * 2
    else:
        hf = hb = h2
        _, m, width = h2.shape
        dir_specs = [pl.BlockSpec((None, tm, width), lambda i: (0, i, 0)),
                     pl.BlockSpec((None, tm, width), lambda i: (1, i, 0))]
    assert m % tm == 0
    return pl.pallas_call(
        functools.partial(_headnorm_gate_kernel, heads=heads, dv=dv, act=act),
        out_shape=jax.ShapeDtypeStruct((m, width), BF16),
        grid=(m // tm,),
        in_specs=dir_specs + [pl.BlockSpec((tm, width), lambda i: (i, gate_off // width)),
                              pl.BlockSpec((1, width), lambda i: (0, 0))],
        out_specs=pl.BlockSpec((tm, width), lambda i: (i, 0)),
        compiler_params=_params(("parallel",)),
        name="headnorm_gate",
    )(hf, hb, proj, head_norm.reshape(1, width).astype(F32))


def _cm_short_conv_kernel(x0_ref, x1_ref, x2_ref, w_ref, b_ref, o0_ref, o1_ref, o2_ref, s_ref, t_ref):
    seq, tc = x0_ref.shape
    half, n2 = o0_ref.shape[1], o0_ref.shape[2]
    for p, (x_ref, o_ref) in enumerate(((x0_ref, o0_ref), (x1_ref, o1_ref), (x2_ref, o2_ref))):
        x = x_ref[...].astype(F32)
        w = w_ref[p]
        bias = b_ref[p]
        s_ref[...] = (pltpu.roll(x, 1, axis=0) * w[0:1] + x * w[1:2]
                      + pltpu.roll(x, seq - 1, axis=0) * w[2:3] + bias)
        s_ref[0:1, :] = x[0:1] * w[1:2] + x[1:2] * w[2:3] + bias
        s_ref[seq - 1:seq, :] = x[seq - 2:seq - 1] * w[0:1] + x[seq - 1:seq] * w[1:2] + bias

        def flip(a, carry):
            slab = s_ref[pl.ds(pl.multiple_of(a * n2, n2), n2), :]
            t_ref[pl.ds(pl.multiple_of(a * tc, tc), tc), :] = slab.T
            return carry

        lax.fori_loop(0, half, flip, 0, unroll=min(4, half))

        def body(ch, carry, o_ref=o_ref):
            o_ref[ch] = t_ref[pl.ds(ch, half, stride=tc), :].astype(o_ref.dtype)
            return carry

        lax.fori_loop(0, tc, body, 0, unroll=GATHER_UNROLL)


def _cm_short_conv(proj, conv_w, conv_b, *, batch, seq, off, ch, n2):
    tc = LANES
    nct = ch // tc
    half = seq // n2
    w = conv_w.reshape(3, 3, nct, tc).transpose(1, 2, 0, 3).astype(F32)
    b = conv_b.reshape(3, nct, 1, tc).astype(F32)
    in_specs = [pl.BlockSpec((seq, tc), functools.partial(lambda b_, j, p: (b_, off // tc + p * nct + j), p=p))
                for p in range(3)]
    in_specs += [pl.BlockSpec((3, None, 3, tc), lambda b_, j: (0, j, 0, 0)),
                 pl.BlockSpec((3, None, 1, tc), lambda b_, j: (0, j, 0, 0))]
    out = jax.ShapeDtypeStruct((batch, ch, half, n2), BF16)
    return pl.pallas_call(
        _cm_short_conv_kernel,
        out_shape=[out, out, out],
        grid=(batch, nct),
        in_specs=in_specs,
        out_specs=[pl.BlockSpec((None, tc, half, n2), lambda b_, j: (b_, j, 0, 0))] * 3,
        scratch_shapes=[pltpu.VMEM((seq, tc), F32), pltpu.VMEM((half * tc, n2), F32)],
        compiler_params=_params(("parallel", "parallel")),
        name="hyena_short_conv",
    )(proj, proj, proj, w, b)


def _cm_taps_kernel(z_ref, w1_ref, b1_ref, w2_ref, b2_ref, w3_ref, f_ref, dl_ref, o_ref, t_ref, h_ref, *, emb, n2):
    @pl.when(pl.program_id(1) == 0)
    def _():
        f = f_ref[...]
        h = jnp.sin(f * (_dot3(w1_ref[...], z_ref[...]) + b1_ref[...]))
        h_ref[...] = jnp.sin(f * (_dot3(w2_ref[...], h) + b2_ref[...]))

    win = jnp.exp(-dl_ref[...] * z_ref[0:1, :]) * z_ref[emb:emb + 1, :]
    out = _dot3(w3_ref[...], h_ref[...]) * win
    gc = out.shape[0]
    tile = out.shape[1] // n2
    for s in range(tile):
        t_ref[s * gc:(s + 1) * gc, :] = out[:, s * n2:(s + 1) * n2]

    def body(ch, carry):
        o_ref[ch] = t_ref[pl.ds(ch, tile, stride=gc), :].astype(o_ref.dtype)
        return carry

    lax.fori_loop(0, gc, body, 0, unroll=GATHER_UNROLL)


def _cm_taps(seq, w1, b1, w2, b2, w3, freq, ch, n2, gc=256):
    emb, hid = w1.shape
    n1 = 2 * seq // n2
    half = n1 // 2
    tile = min(BF16_TILE_ROWS, half)
    order = w3.shape[1] // (2 * ch)
    lag = jnp.arange(2 * seq, dtype=jnp.int32)
    pos = jnp.where(lag < seq, lag, 2 * seq - lag)
    n = pos.astype(F32)
    t = n / (seq - 1)
    bands = jnp.linspace(1e-4, HYENA_BANDS - 1, HYENA_BANDS, dtype=F32)
    ang = (2.0 * math.pi * n / seq)[:, None] * bands[None, :]
    valid = (lag != seq).astype(F32)
    z = jnp.concatenate([t[:, None], jnp.cos(ang), -jnp.sin(ang), valid[:, None]], axis=-1)
    zt = jnp.pad(z, ((0, 0), (0, LANES - emb - 1))).T
    deltas = jnp.abs(jnp.linspace(HYENA_MIN_DECAY, HYENA_MAX_DECAY, ch, dtype=F32))
    dl = jnp.tile(deltas, order).reshape(order * ch, 1)
    pad_h = LANES - hid
    w1t = jnp.pad(w1.astype(F32), ((0, LANES - emb), (0, pad_h))).T
    w2t = jnp.pad(w2.astype(F32), ((0, pad_h), (0, pad_h))).T
    w3t = jnp.pad(w3.astype(F32), ((0, pad_h), (0, 0))).T.reshape(order, 2, ch, LANES)
    w3t = w3t.transpose(1, 0, 2, 3).reshape(2, order * ch, LANES)
    colv = lambda v: jnp.pad(v.astype(F32), (0, pad_h)).reshape(LANES, 1)
    gc = min(gc, order * ch)
    nat = n1 // tile
    full = lambda shape: pl.BlockSpec(shape, lambda a, j: (0, 0))
    return pl.pallas_call(
        functools.partial(_cm_taps_kernel, emb=emb, n2=n2),
        out_shape=jax.ShapeDtypeStruct((order * ch, n1, n2), BF16),
        grid=(nat, order * ch // gc),
        in_specs=[pl.BlockSpec((LANES, tile * n2), lambda a, j: (0, a)),
                  full((LANES, LANES)), full((LANES, 1)), full((LANES, LANES)), full((LANES, 1)),
                  pl.BlockSpec((None, gc, LANES), lambda a, j: (a // (nat // 2), j, 0)),
                  full((LANES, 1)),
                  pl.BlockSpec((gc, 1), lambda a, j: (j, 0))],
        out_specs=pl.BlockSpec((gc, tile, n2), lambda a, j: (j, a, 0)),
        scratch_shapes=[pltpu.VMEM((tile * gc, n2), F32), pltpu.VMEM((LANES, tile * n2), F32)],
        compiler_params=_params(("parallel", "arbitrary")),
        name="hyena_filters",
    )(zt, w1t, colv(b1), w2t, colv(b2), w3t, colv(freq), dl)


def _cm_tables(n1, n2):
    a = jnp.arange(n1, dtype=jnp.int32)
    ang1 = ((a[:, None] * a[None, :]) % n1).astype(F32) * (2.0 * math.pi / n1)
    f1r, f1i = jnp.cos(ang1), -jnp.sin(ang1)
    b = jnp.arange(n2, dtype=jnp.int32)
    angt = (a[:, None] * b[None, :]).astype(F32) * (2.0 * math.pi / (n1 * n2))
    tw = jnp.stack([jnp.cos(angt), -jnp.sin(angt)])
    ang2 = ((b[:, None] * b[None, :]) % n2).astype(F32) * (2.0 * math.pi / n2)
    wr, wi = jnp.cos(ang2), -jnp.sin(ang2)
    wb = jnp.concatenate([jnp.concatenate([wr, wi], axis=1),
                          jnp.concatenate([-wi, wr], axis=1)], axis=0).astype(BF16)
    return f1r, f1i, tw, wb, wb.T


def _cm_forward(x_ref, f1, twr, twi, wb, n1):
    ys = []
    for g in range(x_ref.shape[0]):
        p = _dot(f1, x_ref[g])
        pr, pi = p[:n1], p[n1:]
        ys.append(jnp.concatenate([pr * twr - pi * twi, pr * twi + pi * twr], axis=1))
    y = jnp.concatenate(ys, axis=0).astype(BF16)
    return _dot(y, wb)


def _cm_spec_kernel(t_ref, f1_ref, tw_ref, wb_ref, o_ref, *, n1, inv_n):
    z = _cm_forward(t_ref, f1_ref[...], tw_ref[0], tw_ref[1], wb_ref[...], n1)
    o_ref[...] = (z * inv_n).reshape(o_ref.shape)


def _cm_spectrum(taps, f1, tw, wb, *, group=32):
    nch, n1, n2 = taps.shape
    group = min(group, nch)
    full2 = lambda shape: pl.BlockSpec(shape, lambda j: (0,) * len(shape))
    return pl.pallas_call(
        functools.partial(_cm_spec_kernel, n1=n1, inv_n=1.0 / (n1 * n2)),
        out_shape=jax.ShapeDtypeStruct((nch, n1, 2 * n2), F32),
        grid=(nch // group,),
        in_specs=[pl.BlockSpec((group, n1, n2), lambda j: (j, 0, 0)),
                  full2((2 * n1, n1)), full2((2, n1, n2)), full2((2 * n2, 2 * n2))],
        out_specs=pl.BlockSpec((group, n1, 2 * n2), lambda j: (j, 0, 0)),
        compiler_params=_params(("parallel",)),
        name="hyena_filter_spectrum",
    )(taps, f1, tw, wb)


def _cm_conv_kernel(x_ref, gate_ref, kf_ref, skip_ref, f1_ref, f1h_ref, tw_ref, wb_ref, wbt_ref, o_ref, *, n1, n2):
    group = x_ref.shape[0]
    twr, twi = tw_ref[0], tw_ref[1]
    z = _cm_forward(x_ref, f1_ref[...], twr, twi, wb_ref[...], n1)
    kf = kf_ref[...].reshape(group * n1, 2 * n2)
    zr, zi, kr, ki = z[:, :n2], z[:, n2:], kf[:, :n2], kf[:, n2:]
    s = jnp.concatenate([zr * kr - zi * ki, zr * ki + zi * kr], axis=1).astype(BF16)
    v = _dot(s, wbt_ref[...])
    f1h = f1h_ref[...]
    for g in range(group):
        vr, vi = v[g * n1:(g + 1) * n1, :n2], v[g * n1:(g + 1) * n1, n2:]
        u = jnp.concatenate([vr * twr + vi * twi, vi * twr - vr * twi], axis=0).astype(BF16)
        y = _dot(f1h, u)
        y = y + skip_ref[g] * x_ref[g].astype(F32)
        o_ref[g] = (gate_ref[g].astype(F32) * y).astype(o_ref.dtype)


def _cm_conv(x, gate, kf, skip, f1d, f1h, tw, wb, wbt, *, kf_off, out_dtype, group=32):
    batch, ch, half, n2 = x.shape
    n1 = 2 * half
    group = min(group, ch)
    sig = pl.BlockSpec((None, group, half, n2), lambda j, b: (b, j, 0, 0))
    full2 = lambda shape: pl.BlockSpec(shape, lambda j, b: (0,) * len(shape))
    skip_b = jnp.broadcast_to(skip.astype(F32)[:, None, None], (ch, 1, n2))
    return pl.pallas_call(
        functools.partial(_cm_conv_kernel, n1=n1, n2=n2),
        out_shape=jax.ShapeDtypeStruct((batch, ch, half, n2), out_dtype),
        grid=(ch // group, batch),
        in_specs=[sig, sig,
                  pl.BlockSpec((group, n1, 2 * n2), lambda j, b: (kf_off // group + j, 0, 0)),
                  pl.BlockSpec((group, 1, n2), lambda j, b: (j, 0, 0)),
                  full2((2 * n1, half)), full2((half, 2 * n1)), full2((2, n1, n2)),
                  full2((2 * n2, 2 * n2)), full2((2 * n2, 2 * n2))],
        out_specs=sig,
        compiler_params=_params(("parallel", "parallel")),
        name="hyena_fft_conv",
    )(x, gate, kf, skip_b, f1d, f1h, tw, wb, wbt)


def _cm_to_time_major_kernel(y_ref, o_ref, *, half):
    tc = o_ref.shape[1]
    n2 = y_ref.shape[1]

    def body(a, carry):
        slab = y_ref[pl.ds(a, tc, stride=half), :]
        o_ref[pl.ds(pl.multiple_of(a * n2, n2), n2), :] = slab.T.astype(o_ref.dtype)
        return carry

    lax.fori_loop(0, half, body, 0, unroll=min(4, half))


def _cm_to_time_major(y, *, batch, seq, ch):
    half, n2 = y.shape[2], y.shape[3]
    tc = LANES
    return pl.pallas_call(
        functools.partial(_cm_to_time_major_kernel, half=half),
        out_shape=jax.ShapeDtypeStruct((batch * seq, ch), BF16),
        grid=(batch, ch // tc),
        in_specs=[pl.BlockSpec((None, tc * half, n2), lambda b, j: (b, j, 0))],
        out_specs=pl.BlockSpec((seq, tc), lambda b, j: (b, j)),
        compiler_params=_params(("parallel", "parallel")),
        name="hyena_to_time_major",
    )(y.reshape(batch, ch * half, n2))


def _hyena_cm(proj, hy_off, conv_w, conv_b, w1, b1, w2, b2, w3, freq, skip, *, batch, seq, ch):
    n2 = FFT_N2
    n1 = 2 * seq // n2
    half = n1 // 2
    v, x1, x2 = _cm_short_conv(proj, conv_w, conv_b, batch=batch, seq=seq, off=hy_off, ch=ch, n2=n2)
    taps = _cm_taps(seq, w1, b1, w2, b2, w3, freq, ch, n2)
    f1r, f1i, tw, wb, wbt = _cm_tables(n1, n2)
    fstack = jnp.concatenate([f1r, f1i], axis=0)
    kf = _cm_spectrum(taps, fstack.astype(BF16), tw, wb)
    f1d = fstack[:, :half].astype(BF16)
    f1h = jnp.concatenate([f1r[:half], f1i[:half]], axis=1).astype(BF16)
    zcur = v
    for o, gate in enumerate((x1, x2)):
        last = o == HYENA_ORDER - 1
        zcur = _cm_conv(zcur, gate, kf, skip[o], f1d, f1h, tw, wb, wbt, kf_off=o * ch,
                        out_dtype=F32 if last else BF16)
    return _cm_to_time_major(zcur, batch=batch, seq=seq, ch=ch)


def _merge_kernel(ya_ref, yb_ref, yc_ref, ga_ref, gb_ref, gc_ref, w_ref, o_ref):
    acc = None
    for n, (y_ref, g_ref) in enumerate(((ya_ref, ga_ref), (yb_ref, gb_ref), (yc_ref, gc_ref))):
        gate = 1.0 / (1.0 + jnp.exp(-g_ref[...].astype(F32)))
        term = gate * _dot(y_ref[...], w_ref[n])
        acc = term if acc is None else acc + term
    o_ref[...] = acc.astype(o_ref.dtype)


def _merge(ys, proj, gate_off, w_branch, layer, *, d_model, tm=1024, tn=1024):
    m, width = ys[0].shape
    tm, tn = min(tm, m), min(tn, d_model)
    y_spec = pl.BlockSpec((tm, width), lambda i, j: (i, 0))
    g_specs = [pl.BlockSpec((tm, tn), functools.partial(
        lambda i, j, n: (i, (gate_off + n * d_model) // tn + j), n=n)) for n in range(N_BRANCH)]
    return pl.pallas_call(
        _merge_kernel,
        out_shape=jax.ShapeDtypeStruct((m, d_model), BF16),
        grid=(m // tm, d_model // tn),
        in_specs=[y_spec, y_spec, y_spec] + g_specs +
                 [pl.BlockSpec((None, N_BRANCH, width, tn), lambda i, j: (layer, 0, 0, j))],
        out_specs=pl.BlockSpec((tm, tn), lambda i, j: (i, j)),
        compiler_params=_params(("parallel", "parallel")),
        name="branch_merge",
    )(*ys, proj, proj, proj, w_branch)


def _xattn_kernel(q_ref, k_ref, v_ref, o_ref, *, scale):
    s = _dot_nt(q_ref[...], k_ref[...]) * scale
    s = s - jnp.max(s, axis=-1, keepdims=True)
    p = jnp.exp(s)
    l = jnp.sum(p, axis=-1, keepdims=True)
    o_ref[...] = (_dot(p.astype(BF16), v_ref[...]) / l).astype(o_ref.dtype)


def _xattn(q, kv, *, batch, seq, mem, heads, hd, tq=1024):
    tq = min(tq, seq)
    nq = seq // tq
    return pl.pallas_call(
        functools.partial(_xattn_kernel, scale=hd ** -0.5),
        out_shape=jax.ShapeDtypeStruct(q.shape, BF16),
        grid=(batch, heads, nq),
        in_specs=[pl.BlockSpec((tq, hd), lambda b, h, i: (b * nq + i, h)),
                  pl.BlockSpec((mem, hd), lambda b, h, i: (b, h)),
                  pl.BlockSpec((mem, hd), lambda b, h, i: (b, heads + h))],
        out_specs=pl.BlockSpec((tq, hd), lambda b, h, i: (b * nq + i, h)),
        compiler_params=_params(("parallel", "parallel", "parallel")),
        name="xattn_core",
    )(q, kv, kv)


def _in_proj_layout(d):
    mix = d // 4
    m_dk = mix // MLSTM_HEADS // 2
    g_dk = mix // GLA_HEADS // 2
    sizes = (MLSTM_HEADS * m_dk, MLSTM_HEADS * m_dk, mix, mix, 4 * MLSTM_HEADS,
             GLA_HEADS * g_dk, GLA_HEADS * g_dk, mix, mix, 2 * GLA_RANK, 3 * mix, N_BRANCH * d)
    offs = [int(o) for o in np.concatenate([[0], np.cumsum(sizes)])]
    runs = ((0, 1, 2, 3), (5, 6, 7, 8), (10, 11))
    narrow = (4, 9)
    where = {i: (r, offs[i] - offs[run[0]]) for r, run in enumerate(runs) for i in run}
    return sizes, offs, runs, narrow, where


def kernel(x, mem, norm_gains, final_norm, w_in, b_in, mlstm_head_norm, gla_decay_up, gla_decay_bias, gla_head_norm, hyena_conv_w, hyena_conv_b, hyena_ffn_w1, hyena_ffn_b1, hyena_ffn_w2, hyena_ffn_b2, hyena_ffn_w3, hyena_freq, hyena_skip, w_branch, w_out, xattn_wq, xattn_wkv, xattn_wo, mlp_w1, mlp_w2):
    batch, seq, d = x.shape
    mem_tokens = mem.shape[1]
    depth = norm_gains.shape[0]
    mix = d // 4
    mh, gh = MLSTM_HEADS, GLA_HEADS
    m_dv, g_dv = mix // mh, mix // gh
    m_dk, g_dk = m_dv // 2, g_dv // 2
    nc = seq // SCAN_CHUNK

    sizes, offs, runs, narrow, where = _in_proj_layout(d)
    seg = lambda a, i: a[..., offs[i]:offs[i + 1]]
    w_in_rows = jnp.swapaxes(w_in, 1, 2).astype(BF16)
    run_rows = [(offs[run[0]], sum(sizes[i] for i in run)) for run in runs]
    b_runs = [b_in[:, r0:r0 + n][:, None, :].astype(F32) for r0, n in run_rows]
    n_small = sum(sizes[i] for i in narrow)
    w_small = jnp.pad(jnp.concatenate([w_in_rows[:, offs[i]:offs[i + 1], :] for i in narrow], axis=1),
                      ((0, 0), (0, LANES - n_small), (0, 0)))
    b_small = jnp.pad(jnp.concatenate([seg(b_in, i) for i in narrow], axis=1),
                      ((0, 0), (0, LANES - n_small)))[:, None, :].astype(F32)
    col = lambda i: where[i][1]
    wb16, wo16 = w_branch.astype(BF16), w_out.astype(BF16)
    wq16, wkv16, wxo16 = xattn_wq.astype(BF16), xattn_wkv.astype(BF16), xattn_wo.astype(BF16)
    w1_16, w2_16 = mlp_w1.astype(BF16), mlp_w2.astype(BF16)

    h = x.reshape(batch * seq, d)
    memf = mem.reshape(batch * mem_tokens, d)
    for l in range(depth):
        g = norm_gains[l]
        xn = _rmsnorm(h, g[0], BF16)
        proj_m, proj_g, proj_h = [
            _mm(xn, w_in_rows, l, bias=b_run, out_dtype=BF16, tn=1024, w_rows=rw, name="in_proj")
            for rw, b_run in zip(run_rows, b_runs)]
        small = _mm(xn, w_small, l, bias=b_small, out_dtype=F32, w_rows=(0, LANES), name="in_proj_gates")

        mg = small[:, :4 * mh].reshape(batch, nc, SCAN_CHUNK, 2, 2, mh)
        mgates = mg.transpose(0, 3, 1, 5, 4, 2).reshape(batch, 2, nc, 2 * mh, SCAN_CHUNK)
        h_m = _mlstm(proj_m, mgates, batch=batch, seq=seq, q_off=col(0), k_off=col(1), v_off=col(2),
                     heads=mh, dk=m_dk, dv=m_dv)
        y_a = _headnorm_gate(h_m, proj_m, col(3), mlstm_head_norm[l], heads=mh, dv=m_dv, act="sigmoid")

        up_pad = jnp.zeros((2, LANES, gh * g_dk), F32)
        for dr in range(2):
            r0 = 4 * mh + dr * GLA_RANK
            up_pad = up_pad.at[dr, r0:r0 + GLA_RANK, :].set(gla_decay_up[l, dr].astype(F32))
        gbias = gla_decay_bias[l].reshape(2, 1, gh * g_dk).astype(F32)
        h_g = _gla(proj_g, small, up_pad, gbias, batch=batch, seq=seq, q_off=col(5), k_off=col(6),
                   v_off=col(7), heads=gh, dk=g_dk, dv=g_dv)
        y_b = _headnorm_gate(h_g, proj_g, col(8), gla_head_norm[l], heads=gh, dv=g_dv, act="silu")

        y_c = _hyena_cm(proj_h, col(10), hyena_conv_w[l], hyena_conv_b[l], hyena_ffn_w1[l], hyena_ffn_b1[l],
                        hyena_ffn_w2[l], hyena_ffn_b2[l], hyena_ffn_w3[l], hyena_freq[l], hyena_skip[l],
                        batch=batch, seq=seq, ch=mix)

        merged = _merge((y_a, y_b, y_c), proj_h, col(11), wb16, l, d_model=d)
        h = _mm(merged, wo16, l, resid=h, out_dtype=F32, tn=1024, name="mixer_out")

        xn = _rmsnorm(h, g[1], BF16)
        memn = _rmsnorm(memf, g[2], BF16)
        q = _mm(xn, wq16, l, out_dtype=BF16, tn=1024, name="xattn_q")
        kv = _mm(memn, wkv16, l, out_dtype=BF16, tn=1024, name="xattn_kv")
        o = _xattn(q, kv, batch=batch, seq=seq, mem=mem_tokens, heads=XATTN_HEADS, hd=d // XATTN_HEADS)
        h = _mm(o, wxo16, l, resid=h, out_dtype=F32, tn=1024, name="xattn_out")

        xn = _rmsnorm(h, g[3], BF16)
        hid = _mm(xn, w1_16, l, act="relu2", out_dtype=BF16, tn=1024, name="mlp_up")
        h = _mm(hid, w2_16, l, resid=h, out_dtype=F32, tm=512, tn=256, name="mlp_down")
    return _rmsnorm(h, final_norm, x.dtype).reshape(batch, seq, d)
```

```python
import functools
import math

import numpy as np
import jax
import jax.numpy as jnp
from jax import lax
from jax.experimental import pallas as pl
from jax.experimental.pallas import tpu as pltpu

F32 = jnp.float32
BF16 = jnp.bfloat16

N_BRANCH = 3
MLSTM_HEADS = 4
GLA_HEADS = 4
GLA_RANK = 16
GLA_TAU = 16.0
HYENA_ORDER = 2
HYENA_BANDS = 16
HYENA_MIN_DECAY = math.log(1e-2) / 1.5
HYENA_MAX_DECAY = math.log(1e-2) / 0.3
XATTN_HEADS = 4
EPS = 1e-6

LANES = 128
BF16_TILE_ROWS = 16
VMEM_LIMIT_BYTES = 56 * 1024 * 1024

SCAN_CHUNK = 128
FFT_N2 = 128
NEG_BIG = -1e30
GATHER_UNROLL = 8
EXACT_PIECES = 2


def _params(sem):
    return pltpu.CompilerParams(dimension_semantics=sem, vmem_limit_bytes=VMEM_LIMIT_BYTES)


def _log_sigmoid(x):
    return -(jnp.maximum(-x, 0.0) + jnp.log(1.0 + jnp.exp(-jnp.abs(x))))


def _split_bf16(a, parts):
    out = []
    r = a
    for _ in range(parts):
        p = r.astype(BF16)
        out.append(p)
        r = r - p.astype(F32)
    return out


def _dot(a, b):
    return jnp.dot(a, b, preferred_element_type=F32)


def _dot_nt(a, b):
    return lax.dot_general(a, b, (((1,), (1,)), ((), ())), preferred_element_type=F32)


def _dot_tn(a, b):
    return lax.dot_general(a, b, (((0,), (0,)), ((), ())), preferred_element_type=F32)


def _dot3(a, b):
    a_hi, a_lo = _split_bf16(a, 2)
    b_hi, b_lo = _split_bf16(b, 2)
    return _dot(a_hi, b_hi) + _dot(a_hi, b_lo) + _dot(a_lo, b_hi)


def _rmsnorm_kernel(x_ref, g_ref, o_ref):
    x = x_ref[...].astype(F32)
    r = lax.rsqrt(jnp.mean(x * x, axis=-1, keepdims=True) + EPS)
    o_ref[...] = ((x * r) * g_ref[...]).astype(o_ref.dtype)


def _rmsnorm(x, g, out_dtype, tm=512):
    m, d = x.shape
    tm = min(tm, m)
    return pl.pallas_call(
        _rmsnorm_kernel,
        out_shape=jax.ShapeDtypeStruct((m, d), out_dtype),
        grid=(m // tm,),
        in_specs=[pl.BlockSpec((tm, d), lambda i: (i, 0)),
                  pl.BlockSpec((1, d), lambda i: (0, 0))],
        out_specs=pl.BlockSpec((tm, d), lambda i: (i, 0)),
        compiler_params=_params(("parallel",)),
        name="rmsnorm",
    )(x, g.reshape(1, d).astype(F32))


def _mm_kernel(*refs, nk, act, has_bias, has_resid, w_rows):
    x_ref, w_ref = refs[0], refs[1]
    product = (lambda a, b: _dot_nt(a, b[0])) if w_rows else _dot
    idx = 2
    b_ref = r_ref = None
    if has_bias:
        b_ref = refs[idx]
        idx += 1
    if has_resid:
        r_ref = refs[idx]
        idx += 1
    o_ref = refs[idx]

    def epilogue(acc):
        if has_bias:
            acc = acc + b_ref[...]
        if act == "relu2":
            acc = jnp.square(jnp.maximum(acc, 0.0))
        if has_resid:
            acc = acc + r_ref[...]
        o_ref[...] = acc.astype(o_ref.dtype)

    if nk == 1:
        epilogue(product(x_ref[...], w_ref[...]))
    else:
        acc_ref = refs[idx + 1]
        k = pl.program_id(2)

        @pl.when(k == 0)
        def _():
            acc_ref[...] = jnp.zeros_like(acc_ref)

        acc_ref[...] += product(x_ref[...], w_ref[...])

        @pl.when(k == nk - 1)
        def _():
            epilogue(acc_ref[...])


def _mm(x, w, layer, *, bias=None, resid=None, act=None, out_dtype=BF16, tm=1024, tn=512, tk=None,
        w_rows=None, name="mm"):
    m, kdim = x.shape
    n = w_rows[1] if w_rows else w.shape[2]
    tm, tn = min(tm, m), min(tn, n)
    tk = kdim if tk is None else min(tk, kdim)
    nk = kdim // tk
    assert m % tm == 0 and n % tn == 0 and kdim % tk == 0
    if w_rows:
        row0 = w_rows[0]
        assert row0 % BF16_TILE_ROWS == 0
        w_spec = pl.BlockSpec((pl.Element(1), pl.Element(tn), pl.Element(tk)),
                              lambda i, j, k: (layer, pl.multiple_of(row0 + j * tn, BF16_TILE_ROWS), k * tk))
    else:
        w_spec = pl.BlockSpec((None, tk, tn), lambda i, j, k: (layer, k, j))
    in_specs = [pl.BlockSpec((tm, tk), lambda i, j, k: (i, k)), w_spec]
    args = [x, w]
    if bias is not None:
        in_specs.append(pl.BlockSpec((None, 1, tn), lambda i, j, k: (layer, 0, j)))
        args.append(bias)
    if resid is not None:
        in_specs.append(pl.BlockSpec((tm, tn), lambda i, j, k: (i, j)))
        args.append(resid)
    scratch = [pltpu.VMEM((tm, tn), F32)] if nk > 1 else []
    return pl.pallas_call(
        functools.partial(_mm_kernel, nk=nk, act=act, has_bias=bias is not None,
                          has_resid=resid is not None, w_rows=w_rows is not None),
        out_shape=jax.ShapeDtypeStruct((m, n), out_dtype),
        grid=(m // tm, n // tn, nk),
        in_specs=in_specs,
        out_specs=pl.BlockSpec((tm, tn), lambda i, j, k: (i, j)),
        scratch_shapes=scratch,
        compiler_params=_params(("parallel", "parallel", "arbitrary")),
        name=name,
    )(*args)


def _rows_to_cols(sel, rows):
    return sum(_dot_nt(sel, piece) for piece in _split_bf16(rows, EXACT_PIECES))


def _mlstm_kernel(qf_ref, qb_ref, kf_ref, kb_ref, vf_ref, vb_ref, g_ref, of_ref, ob_ref, c_ref, m_ref, *,
                  nc, heads, dk, dv, scale):
    lc = SCAN_CHUNK
    c = pl.program_id(1)

    @pl.when(c == 0)
    def _():
        c_ref[...] = jnp.zeros_like(c_ref)
        m_ref[...] = jnp.zeros_like(m_ref)

    dirs = (0, 1)
    q_refs, k_refs, v_refs, o_refs = (qf_ref, qb_ref), (kf_ref, kb_ref), (vf_ref, vb_ref), (of_ref, ob_ref)
    row = lax.broadcasted_iota(jnp.int32, (lc, lc), 0)
    col = lax.broadcasted_iota(jnp.int32, (lc, lc), 1)
    tris = (col <= row, col >= row)
    tri_bs = [jnp.where(t, 1.0, 0.0).astype(BF16) for t in tris]
    eye_b = jnp.where(row == col, 1.0, 0.0).astype(BF16)
    ones_blk = (lax.broadcasted_iota(jnp.int32, (lc, LANES), 1) == 0).astype(BF16)

    gls = [g_ref[0, c], g_ref[1, nc - 1 - c]]
    lf_alls = [_log_sigmoid(gl) for gl in gls]
    lf_pieces = [_split_bf16(lf, EXACT_PIECES) for lf in lf_alls]
    b_rows = [sum(_dot_nt(piece, tri_bs[d]) for piece in lf_pieces[d]) for d in dirs]
    b_cols = [sum(_dot_nt(tri_bs[d], piece) for piece in lf_pieces[d]) for d in dirs]
    g_alls = [jnp.sum(lf, axis=1, keepdims=True) for lf in lf_alls]
    wk_rows, m_news, decays = ([], []), ([], []), ([], [])
    for d in dirs:
        for hh in range(heads):
            ig = gls[d][2 * hh:2 * hh + 1, :]
            g = g_alls[d][2 * hh + 1:2 * hh + 2, :]
            m_prev = m_ref[d, hh]
            a_row = g - b_rows[d][2 * hh + 1:2 * hh + 2, :] + ig
            m_new = jnp.maximum(g + m_prev, jnp.max(a_row, axis=1, keepdims=True))
            wk_rows[d].append(jnp.exp(a_row - m_new))
            m_news[d].append(m_new)
            decays[d].append(jnp.exp(g + m_prev - m_new))
    wk_cols = [_rows_to_cols(eye_b, jnp.concatenate(wk_rows[d] + [jnp.zeros((heads, lc), F32)], axis=0))
               for d in dirs]

    chains = [(d, hh) for hh in range(heads) for d in dirs]
    qs = {ch: q_refs[ch[0]][:, ch[1] * dk:(ch[1] + 1) * dk] for ch in chains}
    ks = {ch: k_refs[ch[0]][:, ch[1] * dk:(ch[1] + 1) * dk] for ch in chains}
    v_augs = {ch: jnp.concatenate([v_refs[ch[0]][:, ch[1] * dv:(ch[1] + 1) * dv], ones_blk], axis=1)
              for ch in chains}
    qk = {ch: _dot_nt(qs[ch], ks[ch]) for ch in chains}
    qc = {ch: _dot(qs[ch], c_ref[ch[0], ch[1]].astype(BF16)) for ch in chains}
    kv = {ch: _dot_tn((ks[ch].astype(F32) * wk_cols[ch[0]][:, ch[1]:ch[1] + 1]).astype(BF16), v_augs[ch])
          for ch in chains}
    for d, hh in chains:
        b_col = b_cols[d][:, 2 * hh + 1:2 * hh + 2]
        log_d = jnp.where(tris[d], b_col - b_rows[d][2 * hh + 1:2 * hh + 2, :] + gls[d][2 * hh:2 * hh + 1, :],
                          NEG_BIG)
        m_inter = b_col + m_ref[d, hh]
        m_j = jnp.maximum(m_inter, jnp.max(log_d, axis=1, keepdims=True))
        smat = (qk[d, hh] * jnp.exp(log_d - m_j)).astype(BF16)
        num_aug = (_dot(smat, v_augs[d, hh]) + jnp.exp(m_inter - m_j) * qc[d, hh]) * scale
        num = num_aug[:, :dv]
        den = num_aug[:, dv:dv + 1]
        o_refs[d][:, hh * dv:(hh + 1) * dv] = (num / jnp.maximum(jnp.abs(den), jnp.exp(-m_j))).astype(of_ref.dtype)
        c_ref[d, hh] = decays[d][hh] * c_ref[d, hh] + kv[d, hh]
        m_ref[d, hh] = m_news[d][hh]


def _mlstm(proj, gates, *, batch, seq, q_off, k_off, v_off, heads, dk, dv):
    lc = SCAN_CHUNK
    nc = seq // lc
    assert dk == lc and seq % lc == 0 and dv % LANES == 0
    wqk, wv = heads * dk, heads * dv
    fwd = lambda b, c: b * nc + c
    bwd = lambda b, c: b * nc + nc - 1 - c

    def both(width, col):
        return [pl.BlockSpec((lc, width), lambda b, c: (fwd(b, c), col)),
                pl.BlockSpec((lc, width), lambda b, c: (bwd(b, c), col))]

    out = jax.ShapeDtypeStruct((batch * seq, wv), F32)
    return pl.pallas_call(
        functools.partial(_mlstm_kernel, nc=nc, heads=heads, dk=dk, dv=dv, scale=dk ** -0.5),
        out_shape=[out, out],
        grid=(batch, nc),
        in_specs=both(wqk, q_off // wqk) + both(wqk, k_off // wqk) + both(wv, v_off // wv) +
                 [pl.BlockSpec((None, 2, nc, 2 * heads, lc), lambda b, c: (b, 0, 0, 0, 0))],
        out_specs=both(wv, 0),
        scratch_shapes=[pltpu.VMEM((2, heads, dk, dv + LANES), F32), pltpu.VMEM((2, heads, 1, 1), F32)],
        compiler_params=_params(("parallel", "arbitrary")),
        name="mlstm_scan",
    )(proj, proj, proj, proj, proj, proj, gates)


def _gla_levels():
    lc = SCAN_CHUNK
    nlev = int(math.log2(lc)) + 1
    t = np.arange(lc)
    diff = np.zeros((2, nlev * lc, lc), np.float32)
    mask = np.zeros((2, nlev, lc, lc), np.float32)
    u = t[None, :]
    for d in range(2):
        cum = (u <= t[:, None]) if d == 0 else (u >= t[:, None])
        diff[d, :lc] = cum
        mask[d, 0] = np.eye(lc)
        for l in range(1, nlev):
            w = 1 << (l - 1)
            pair = t // (2 * w)
            second = (t % (2 * w)) >= w
            if d == 0:
                bd = pair * 2 * w + w - 1
                cum_bd = u <= bd[:, None]
                is_q, is_k = second, ~second
            else:
                bd = pair * 2 * w + w
                cum_bd = u >= bd[:, None]
                is_q, is_k = ~second, second
            diff[d, l * lc:(l + 1) * lc] = cum.astype(np.float32) - cum_bd.astype(np.float32)
            mask[d, l] = ((pair[:, None] == pair[None, :]) & is_q[:, None] & is_k[None, :])
    return diff, mask


def _gla_kernel(qf_ref, qb_ref, kf_ref, kb_ref, vf_ref, vb_ref, smf_ref, smb_ref, up_ref, bias_ref, d_ref,
                mask_ref, of_ref, ob_ref, s_ref, *, heads, dk, dv, nlev, scale):
    lc = SCAN_CHUNK
    c = pl.program_id(1)

    @pl.when(c == 0)
    def _():
        s_ref[...] = jnp.zeros_like(s_ref)

    dirs = (0, 1)
    q_refs, k_refs, v_refs = (qf_ref, qb_ref), (kf_ref, kb_ref), (vf_ref, vb_ref)
    sm_refs, o_refs = (smf_ref, smb_ref), (of_ref, ob_ref)
    hs = [slice(hh * dk, (hh + 1) * dk) for hh in range(heads)]

    zs = [_dot3(sm_refs[d][...], up_ref[d]) + bias_ref[d] for d in dirs]
    las = [_log_sigmoid(z) * (1.0 / GLA_TAU) for z in zs]
    xs = [sum(_dot(d_ref[d], piece) for piece in _split_bf16(las[d], EXACT_PIECES)) for d in dirs]
    g_rows = [jnp.sum(la, axis=0, keepdims=True) for la in las]

    qbs = [q_refs[d][...] for d in dirs]
    kbs = [k_refs[d][...] for d in dirs]
    qfs = [qb.astype(F32) for qb in qbs]
    kfs = [kb.astype(F32) for kb in kbs]
    atts = [[mask_ref[d, 0] * _dot_nt(qbs[d][:, s], kbs[d][:, s]) for s in hs] for d in dirs]
    for l in range(1, nlev):
        es = [jnp.exp(-jnp.abs(xs[d][l * lc:(l + 1) * lc])) for d in dirs]
        qes = [(qfs[d] * es[d]).astype(BF16) for d in dirs]
        kes = [(kfs[d] * es[d]).astype(BF16) for d in dirs]
        atts = [[att + mask_ref[d, l] * _dot_nt(qes[d][:, s], kes[d][:, s]) for att, s in zip(atts[d], hs)]
                for d in dirs]

    bs = [x[:lc] for x in xs]
    qss = [(qfs[d] * jnp.exp(bs[d])).astype(BF16) for d in dirs]
    kds = [(kfs[d] * jnp.exp(g_rows[d] - bs[d])).astype(BF16) for d in dirs]
    egs = [jnp.exp(g) for g in g_rows]
    for hh, s in enumerate(hs):
        for d in dirs:
            v = v_refs[d][:, hh * dv:(hh + 1) * dv]
            o = _dot(atts[d][hh].astype(BF16), v) + _dot_nt(qss[d][:, s], s_ref[d, hh].astype(BF16))
            o_refs[d][:, hh * dv:(hh + 1) * dv] = (o * scale).astype(o_refs[d].dtype)
            s_ref[d, hh] = egs[d][:, s] * s_ref[d, hh] + _dot_tn(v, kds[d][:, s])


def _gla(proj, small, up_pad, bias, *, batch, seq, q_off, k_off, v_off, heads, dk, dv):
    lc = SCAN_CHUNK
    nc = seq // lc
    assert dk == lc and seq % lc == 0
    diff, mask = _gla_levels()
    nlev = mask.shape[1]
    wqk, wv = heads * dk, heads * dv
    fwd = lambda b, c: b * nc + c
    bwd = lambda b, c: b * nc + nc - 1 - c

    def both(width, col):
        return [pl.BlockSpec((lc, width), lambda b, c: (fwd(b, c), col)),
                pl.BlockSpec((lc, width), lambda b, c: (bwd(b, c), col))]

    full = lambda shape: pl.BlockSpec(shape, lambda b, c: (0,) * len(shape))
    out = jax.ShapeDtypeStruct((batch * seq, wv), F32)
    return pl.pallas_call(
        functools.partial(_gla_kernel, heads=heads, dk=dk, dv=dv, nlev=nlev, scale=dk ** -0.5),
        out_shape=[out, out],
        grid=(batch, nc),
        in_specs=both(wqk, q_off // wqk) + both(wqk, k_off // wqk) + both(wv, v_off // wv) + both(LANES, 0) +
                 [full((2, LANES, wqk)), full((2, 1, wqk)), full((2, nlev * lc, lc)), full((2, nlev, lc, lc))],
        out_specs=both(wv, 0),
        scratch_shapes=[pltpu.VMEM((2, heads, dv, dk), F32)],
        compiler_params=_params(("parallel", "arbitrary")),
        name="gla_scan",
    )(proj, proj, proj, proj, proj, proj, small, small, up_pad, bias,
      jnp.asarray(diff, BF16), jnp.asarray(mask, F32))


def _headnorm_gate_kernel(hf_ref, hb_ref, g_ref, hn_ref, o_ref, *, heads, dv, act):
    h = hf_ref[...] + hb_ref[...]
    outs = []
    for hh in range(heads):
        blk = h[:, hh * dv:(hh + 1) * dv]
        r = lax.rsqrt(jnp.mean(blk * blk, axis=-1, keepdims=True) + EPS)
        outs.append(blk * r)
    hn = jnp.concatenate(outs, axis=1) * hn_ref[...]
    gate = g_ref[...].astype(F32)
    sig = 1.0 / (1.0 + jnp.exp(-gate))
    gate = sig if act == "sigmoid" else gate * sig
    o_ref[...] = (hn * gate).astype(o_ref.dtype)


def _headnorm_gate(h2, proj, gate_off, head_norm, *, heads, dv, act, tm=512):
    if isinstance(h2, (tuple, list)):
        hf, hb = h2
        m, width = hf.shape
        dir_specs = [pl.BlockSpec((tm, width), lambda i: (i, 0))] * 2
    else:
        hf = hb = h2
        _, m, width = h2.shape
        dir_specs = [pl.BlockSpec((None, tm, width), lambda i: (0, i, 0)),
                     pl.BlockSpec((None, tm, width), lambda i: (1, i, 0))]
    assert m % tm == 0
    return pl.pallas_call(
        functools.partial(_headnorm_gate_kernel, heads=heads, dv=dv, act=act),
        out_shape=jax.ShapeDtypeStruct((m, width), BF16),
        grid=(m // tm,),
        in_specs=dir_specs + [pl.BlockSpec((tm, width), lambda i: (i, gate_off // width)),
                              pl.BlockSpec((1, width), lambda i: (0, 0))],
        out_specs=pl.BlockSpec((tm, width), lambda i: (i, 0)),
        compiler_params=_params(("parallel",)),
        name="headnorm_gate",
    )(hf, hb, proj, head_norm.reshape(1, width).astype(F32))


def _cm_short_conv_kernel(x0_ref, x1_ref, x2_ref, w_ref, b_ref, o0_ref, o1_ref, o2_ref, s_ref, t_ref):
    seq, tc = x0_ref.shape
    half, n2 = o0_ref.shape[1], o0_ref.shape[2]
    for p, (x_ref, o_ref) in enumerate(((x0_ref, o0_ref), (x1_ref, o1_ref), (x2_ref, o2_ref))):
        x = x_ref[...].astype(F32)
        w = w_ref[p]
        bias = b_ref[p]
        s_ref[...] = (pltpu.roll(x, 1, axis=0) * w[0:1] + x * w[1:2]
                      + pltpu.roll(x, seq - 1, axis=0) * w[2:3] + bias)
        s_ref[0:1, :] = x[0:1] * w[1:2] + x[1:2] * w[2:3] + bias
        s_ref[seq - 1:seq, :] = x[seq - 2:seq - 1] * w[0:1] + x[seq - 1:seq] * w[1:2] + bias

        def flip(a, carry):
            slab = s_ref[pl.ds(pl.multiple_of(a * n2, n2), n2), :]
            t_ref[pl.ds(pl.multiple_of(a * tc, tc), tc), :] = slab.T
            return carry

        lax.fori_loop(0, half, flip, 0, unroll=min(4, half))

        def body(ch, carry, o_ref=o_ref):
            o_ref[ch] = t_ref[pl.ds(ch, half, stride=tc), :].astype(o_ref.dtype)
            return carry

        lax.fori_loop(0, tc, body, 0, unroll=GATHER_UNROLL)


def _cm_short_conv(proj, conv_w, conv_b, *, batch, seq, off, ch, n2):
    tc = LANES
    nct = ch // tc
    half = seq // n2
    w = conv_w.reshape(3, 3, nct, tc).transpose(1, 2, 0, 3).astype(F32)
    b = conv_b.reshape(3, nct, 1, tc).astype(F32)
    in_specs = [pl.BlockSpec((seq, tc), functools.partial(lambda b_, j, p: (b_, off // tc + p * nct + j), p=p))
                for p in range(3)]
    in_specs += [pl.BlockSpec((3, None, 3, tc), lambda b_, j: (0, j, 0, 0)),
                 pl.BlockSpec((3, None, 1, tc), lambda b_, j: (0, j, 0, 0))]
    out = jax.ShapeDtypeStruct((batch, ch, half, n2), BF16)
    return pl.pallas_call(
        _cm_short_conv_kernel,
        out_shape=[out, out, out],
        grid=(batch, nct),
        in_specs=in_specs,
        out_specs=[pl.BlockSpec((None, tc, half, n2), lambda b_, j: (b_, j, 0, 0))] * 3,
        scratch_shapes=[pltpu.VMEM((seq, tc), F32), pltpu.VMEM((half * tc, n2), F32)],
        compiler_params=_params(("parallel", "parallel")),
        name="hyena_short_conv",
    )(proj, proj, proj, w, b)


def _cm_taps_kernel(z_ref, w1_ref, b1_ref, w2_ref, b2_ref, w3_ref, f_ref, dl_ref, o_ref, t_ref, h_ref, *, emb, n2):
    @pl.when(pl.program_id(1) == 0)
    def _():
        f = f_ref[...]
        h = jnp.sin(f * (_dot3(w1_ref[...], z_ref[...]) + b1_ref[...]))
        h_ref[...] = jnp.sin(f * (_dot3(w2_ref[...], h) + b2_ref[...]))

    win = jnp.exp(-dl_ref[...] * z_ref[0:1, :]) * z_ref[emb:emb + 1, :]
    out = _dot3(w3_ref[...], h_ref[...]) * win
    gc = out.shape[0]
    tile = out.shape[1] // n2
    for s in range(tile):
        t_ref[s * gc:(s + 1) * gc, :] = out[:, s * n2:(s + 1) * n2]

    def body(ch, carry):
        o_ref[ch] = t_ref[pl.ds(ch, tile, stride=gc), :].astype(o_ref.dtype)
        return carry

    lax.fori_loop(0, gc, body, 0, unroll=GATHER_UNROLL)


def _cm_taps(seq, w1, b1, w2, b2, w3, freq, ch, n2, gc=256):
    emb, hid = w1.shape
    n1 = 2 * seq // n2
    half = n1 // 2
    tile = min(BF16_TILE_ROWS, half)
    order = w3.shape[1] // (2 * ch)
    lag = jnp.arange(2 * seq, dtype=jnp.int32)
    pos = jnp.where(lag < seq, lag, 2 * seq - lag)
    n = pos.astype(F32)
    t = n / (seq - 1)
    bands = jnp.linspace(1e-4, HYENA_BANDS - 1, HYENA_BANDS, dtype=F32)
    ang = (2.0 * math.pi * n / seq)[:, None] * bands[None, :]
    valid = (lag != seq).astype(F32)
    z = jnp.concatenate([t[:, None], jnp.cos(ang), -jnp.sin(ang), valid[:, None]], axis=-1)
    zt = jnp.pad(z, ((0, 0), (0, LANES - emb - 1))).T
    deltas = jnp.abs(jnp.linspace(HYENA_MIN_DECAY, HYENA_MAX_DECAY, ch, dtype=F32))
    dl = jnp.tile(deltas, order).reshape(order * ch, 1)
    pad_h = LANES - hid
    w1t = jnp.pad(w1.astype(F32), ((0, LANES - emb), (0, pad_h))).T
    w2t = jnp.pad(w2.astype(F32), ((0, pad_h), (0, pad_h))).T
    w3t = jnp.pad(w3.astype(F32), ((0, pad_h), (0, 0))).T.reshape(order, 2, ch, LANES)
    w3t = w3t.transpose(1, 0, 2, 3).reshape(2, order * ch, LANES)
    colv = lambda v: jnp.pad(v.astype(F32), (0, pad_h)).reshape(LANES, 1)
    gc = min(gc, order * ch)
    nat = n1 // tile
    full = lambda shape: pl.BlockSpec(shape, lambda a, j: (0, 0))
    return pl.pallas_call(
        functools.partial(_cm_taps_kernel, emb=emb, n2=n2),
        out_shape=jax.ShapeDtypeStruct((order * ch, n1, n2), BF16),
        grid=(nat, order * ch // gc),
        in_specs=[pl.BlockSpec((LANES, tile * n2), lambda a, j: (0, a)),
                  full((LANES, LANES)), full((LANES, 1)), full((LANES, LANES)), full((LANES, 1)),
                  pl.BlockSpec((None, gc, LANES), lambda a, j: (a // (nat // 2), j, 0)),
                  full((LANES, 1)),
                  pl.BlockSpec((gc, 1), lambda a, j: (j, 0))],
        out_specs=pl.BlockSpec((gc, tile, n2), lambda a, j: (j, a, 0)),
        scratch_shapes=[pltpu.VMEM((tile * gc, n2), F32), pltpu.VMEM((LANES, tile * n2), F32)],
        compiler_params=_params(("parallel", "arbitrary")),
        name="hyena_filters",
    )(zt, w1t, colv(b1), w2t, colv(b2), w3t, colv(freq), dl)


def _cm_tables(n1, n2):
    a = jnp.arange(n1, dtype=jnp.int32)
    ang1 = ((a[:, None] * a[None, :]) % n1).astype(F32) * (2.0 * math.pi / n1)
    f1r, f1i = jnp.cos(ang1), -jnp.sin(ang1)
    b = jnp.arange(n2, dtype=jnp.int32)
    angt = (a[:, None] * b[None, :]).astype(F32) * (2.0 * math.pi / (n1 * n2))
    tw = jnp.stack([jnp.cos(angt), -jnp.sin(angt)])
    ang2 = ((b[:, None] * b[None, :]) % n2).astype(F32) * (2.0 * math.pi / n2)
    wr, wi = jnp.cos(ang2), -jnp.sin(ang2)
    wb = jnp.concatenate([jnp.concatenate([wr, wi], axis=1),
                          jnp.concatenate([-wi, wr], axis=1)], axis=0).astype(BF16)
    return f1r, f1i, tw, wb, wb.T


def _cm_forward(x_ref, f1, twr, twi, wb, n1):
    ys = []
    for g in range(x_ref.shape[0]):
        p = _dot(f1, x_ref[g])
        pr, pi = p[:n1], p[n1:]
        ys.append(jnp.concatenate([pr * twr - pi * twi, pr * twi + pi * twr], axis=1))
    y = jnp.concatenate(ys, axis=0).astype(BF16)
    return _dot(y, wb)


def _cm_spec_kernel(t_ref, f1_ref, tw_ref, wb_ref, o_ref, *, n1, inv_n):
    z = _cm_forward(t_ref, f1_ref[...], tw_ref[0], tw_ref[1], wb_ref[...], n1)
    o_ref[...] = (z * inv_n).reshape(o_ref.shape).astype(o_ref.dtype)


def _cm_spectrum(taps, f1, tw, wb, *, group=32):
    nch, n1, n2 = taps.shape
    group = min(group, nch)
    full2 = lambda shape: pl.BlockSpec(shape, lambda j: (0,) * len(shape))
    return pl.pallas_call(
        functools.partial(_cm_spec_kernel, n1=n1, inv_n=1.0 / (n1 * n2)),
        out_shape=jax.ShapeDtypeStruct((nch, n1, 2 * n2), BF16),
        grid=(nch // group,),
        in_specs=[pl.BlockSpec((group, n1, n2), lambda j: (j, 0, 0)),
                  full2((2 * n1, n1)), full2((2, n1, n2)), full2((2 * n2, 2 * n2))],
        out_specs=pl.BlockSpec((group, n1, 2 * n2), lambda j: (j, 0, 0)),
        compiler_params=_params(("parallel",)),
        name="hyena_filter_spectrum",
    )(taps, f1, tw, wb)


def _cm_conv_kernel(x_ref, gate_ref, kf_ref, skip_ref, f1_ref, f1h_ref, tw_ref, wb_ref, wbt_ref, o_ref, *, n1, n2):
    group = x_ref.shape[0]
    twr, twi = tw_ref[0], tw_ref[1]
    z = _cm_forward(x_ref, f1_ref[...], twr, twi, wb_ref[...], n1)
    kf = kf_ref[...].reshape(group * n1, 2 * n2).astype(F32)
    zr, zi, kr, ki = z[:, :n2], z[:, n2:], kf[:, :n2], kf[:, n2:]
    s = jnp.concatenate([zr * kr - zi * ki, zr * ki + zi * kr], axis=1).astype(BF16)
    v = _dot(s, wbt_ref[...])
    f1h = f1h_ref[...]
    for g in range(group):
        vr, vi = v[g * n1:(g + 1) * n1, :n2], v[g * n1:(g + 1) * n1, n2:]
        u = jnp.concatenate([vr * twr + vi * twi, vi * twr - vr * twi], axis=0).astype(BF16)
        y = _dot(f1h, u)
        y = y + skip_ref[g] * x_ref[g].astype(F32)
        o_ref[g] = (gate_ref[g].astype(F32) * y).astype(o_ref.dtype)


def _cm_conv(x, gate, kf, skip, f1d, f1h, tw, wb, wbt, *, kf_off, out_dtype, group=32):
    batch, ch, half, n2 = x.shape
    n1 = 2 * half
    group = min(group, ch)
    sig = pl.BlockSpec((None, group, half, n2), lambda j, b: (b, j, 0, 0))
    full2 = lambda shape: pl.BlockSpec(shape, lambda j, b: (0,) * len(shape))
    skip_b = jnp.broadcast_to(skip.astype(F32)[:, None, None], (ch, 1, n2))
    return pl.pallas_call(
        functools.partial(_cm_conv_kernel, n1=n1, n2=n2),
        out_shape=jax.ShapeDtypeStruct((batch, ch, half, n2), out_dtype),
        grid=(ch // group, batch),
        in_specs=[sig, sig,
                  pl.BlockSpec((group, n1, 2 * n2), lambda j, b: (kf_off // group + j, 0, 0)),
                  pl.BlockSpec((group, 1, n2), lambda j, b: (j, 0, 0)),
                  full2((2 * n1, half)), full2((half, 2 * n1)), full2((2, n1, n2)),
                  full2((2 * n2, 2 * n2)), full2((2 * n2, 2 * n2))],
        out_specs=sig,
        compiler_params=_params(("parallel", "parallel")),
        name="hyena_fft_conv",
    )(x, gate, kf, skip_b, f1d, f1h, tw, wb, wbt)


def _cm_to_time_major_kernel(y_ref, o_ref, *, half):
    tc = o_ref.shape[1]
    n2 = y_ref.shape[1]

    def body(a, carry):
        slab = y_ref[pl.ds(a, tc, stride=half), :]
        o_ref[pl.ds(pl.multiple_of(a * n2, n2), n2), :] = slab.T.astype(o_ref.dtype)
        return carry

    lax.fori_loop(0, half, body, 0, unroll=min(4, half))


def _cm_to_time_major(y, *, batch, seq, ch):
    half, n2 = y.shape[2], y.shape[3]
    tc = LANES
    return pl.pallas_call(
        functools.partial(_cm_to_time_major_kernel, half=half),
        out_shape=jax.ShapeDtypeStruct((batch * seq, ch), BF16),
        grid=(batch, ch // tc),
        in_specs=[pl.BlockSpec((None, tc * half, n2), lambda b, j: (b, j, 0))],
        out_specs=pl.BlockSpec((seq, tc), lambda b, j: (b, j)),
        compiler_params=_params(("parallel", "parallel")),
        name="hyena_to_time_major",
    )(y.reshape(batch, ch * half, n2))


def _hyena_cm(proj, hy_off, conv_w, conv_b, w1, b1, w2, b2, w3, freq, skip, *, batch, seq, ch):
    n2 = FFT_N2
    n1 = 2 * seq // n2
    half = n1 // 2
    v, x1, x2 = _cm_short_conv(proj, conv_w, conv_b, batch=batch, seq=seq, off=hy_off, ch=ch, n2=n2)
    taps = _cm_taps(seq, w1, b1, w2, b2, w3, freq, ch, n2)
    f1r, f1i, tw, wb, wbt = _cm_tables(n1, n2)
    fstack = jnp.concatenate([f1r, f1i], axis=0)
    kf = _cm_spectrum(taps, fstack.astype(BF16), tw, wb)
    f1d = fstack[:, :half].astype(BF16)
    f1h = jnp.concatenate([f1r[:half], f1i[:half]], axis=1).astype(BF16)
    zcur = v
    for o, gate in enumerate((x1, x2)):
        last = o == HYENA_ORDER - 1
        zcur = _cm_conv(zcur, gate, kf, skip[o], f1d, f1h, tw, wb, wbt, kf_off=o * ch,
                        out_dtype=F32 if last else BF16)
    return _cm_to_time_major(zcur, batch=batch, seq=seq, ch=ch)


def _merge_kernel(ya_ref, yb_ref, yc_ref, ga_ref, gb_ref, gc_ref, w_ref, o_ref):
    acc = None
    for n, (y_ref, g_ref) in enumerate(((ya_ref, ga_ref), (yb_ref, gb_ref), (yc_ref, gc_ref))):
        gate = 1.0 / (1.0 + jnp.exp(-g_ref[...].astype(F32)))
        term = gate * _dot(y_ref[...], w_ref[n])
        acc = term if acc is None else acc + term
    o_ref[...] = acc.astype(o_ref.dtype)


def _merge(ys, proj, gate_off, w_branch, layer, *, d_model, tm=1024, tn=1024):
    m, width = ys[0].shape
    tm, tn = min(tm, m), min(tn, d_model)
    y_spec = pl.BlockSpec((tm, width), lambda i, j: (i, 0))
    g_specs = [pl.BlockSpec((tm, tn), functools.partial(
        lambda i, j, n: (i, (gate_off + n * d_model) // tn + j), n=n)) for n in range(N_BRANCH)]
    return pl.pallas_call(
        _merge_kernel,
        out_shape=jax.ShapeDtypeStruct((m, d_model), BF16),
        grid=(m // tm, d_model // tn),
        in_specs=[y_spec, y_spec, y_spec] + g_specs +
                 [pl.BlockSpec((None, N_BRANCH, width, tn), lambda i, j: (layer, 0, 0, j))],
        out_specs=pl.BlockSpec((tm, tn), lambda i, j: (i, j)),
        compiler_params=_params(("parallel", "parallel")),
        name="branch_merge",
    )(*ys, proj, proj, proj, w_branch)


def _xattn_kernel(q_ref, k_ref, v_ref, o_ref, *, scale):
    s = _dot_nt(q_ref[...], k_ref[...]) * scale
    s = s - jnp.max(s, axis=-1, keepdims=True)
    p = jnp.exp(s)
    l = jnp.sum(p, axis=-1, keepdims=True)
    o_ref[...] = (_dot(p.astype(BF16), v_ref[...]) / l).astype(o_ref.dtype)


def _xattn(q, kv, *, batch, seq, mem, heads, hd, tq=1024):
    tq = min(tq, seq)
    nq = seq // tq
    return pl.pallas_call(
        functools.partial(_xattn_kernel, scale=hd ** -0.5),
        out_shape=jax.ShapeDtypeStruct(q.shape, BF16),
        grid=(batch, heads, nq),
        in_specs=[pl.BlockSpec((tq, hd), lambda b, h, i: (b * nq + i, h)),
                  pl.BlockSpec((mem, hd), lambda b, h, i: (b, h)),
                  pl.BlockSpec((mem, hd), lambda b, h, i: (b, heads + h))],
        out_specs=pl.BlockSpec((tq, hd), lambda b, h, i: (b * nq + i, h)),
        compiler_params=_params(("parallel", "parallel", "parallel")),
        name="xattn_core",
    )(q, kv, kv)


def _in_proj_layout(d):
    mix = d // 4
    m_dk = mix // MLSTM_HEADS // 2
    g_dk = mix // GLA_HEADS // 2
    sizes = (MLSTM_HEADS * m_dk, MLSTM_HEADS * m_dk, mix, mix, 4 * MLSTM_HEADS,
             GLA_HEADS * g_dk, GLA_HEADS * g_dk, mix, mix, 2 * GLA_RANK, 3 * mix, N_BRANCH * d)
    offs = [int(o) for o in np.concatenate([[0], np.cumsum(sizes)])]
    runs = ((0, 1, 2, 3), (5, 6, 7, 8), (10, 11))
    narrow = (4, 9)
    where = {i: (r, offs[i] - offs[run[0]]) for r, run in enumerate(runs) for i in run}
    return sizes, offs, runs, narrow, where


def kernel(x, mem, norm_gains, final_norm, w_in, b_in, mlstm_head_norm, gla_decay_up, gla_decay_bias, gla_head_norm, hyena_conv_w, hyena_conv_b, hyena_ffn_w1, hyena_ffn_b1, hyena_ffn_w2, hyena_ffn_b2, hyena_ffn_w3, hyena_freq, hyena_skip, w_branch, w_out, xattn_wq, xattn_wkv, xattn_wo, mlp_w1, mlp_w2):
    batch, seq, d = x.shape
    mem_tokens = mem.shape[1]
    depth = norm_gains.shape[0]
    mix = d // 4
    mh, gh = MLSTM_HEADS, GLA_HEADS
    m_dv, g_dv = mix // mh, mix // gh
    m_dk, g_dk = m_dv // 2, g_dv // 2
    nc = seq // SCAN_CHUNK

    sizes, offs, runs, narrow, where = _in_proj_layout(d)
    seg = lambda a, i: a[..., offs[i]:offs[i + 1]]
    w_in_rows = jnp.swapaxes(w_in, 1, 2).astype(BF16)
    run_rows = [(offs[run[0]], sum(sizes[i] for i in run)) for run in runs]
    b_runs = [b_in[:, r0:r0 + n][:, None, :].astype(F32) for r0, n in run_rows]
    n_small = sum(sizes[i] for i in narrow)
    w_small = jnp.pad(jnp.concatenate([w_in_rows[:, offs[i]:offs[i + 1], :] for i in narrow], axis=1),
                      ((0, 0), (0, LANES - n_small), (0, 0)))
    b_small = jnp.pad(jnp.concatenate([seg(b_in, i) for i in narrow], axis=1),
                      ((0, 0), (0, LANES - n_small)))[:, None, :].astype(F32)
    col = lambda i: where[i][1]
    wb16, wo16 = w_branch.astype(BF16), w_out.astype(BF16)
    wq16, wkv16, wxo16 = xattn_wq.astype(BF16), xattn_wkv.astype(BF16), xattn_wo.astype(BF16)
    w1_16, w2_16 = mlp_w1.astype(BF16), mlp_w2.astype(BF16)

    h = x.reshape(batch * seq, d)
    memf = mem.reshape(batch * mem_tokens, d)
    for l in range(depth):
        g = norm_gains[l]
        xn = _rmsnorm(h, g[0], BF16)
        proj_m, proj_g, proj_h = [
            _mm(xn, w_in_rows, l, bias=b_run, out_dtype=BF16, tn=1024, w_rows=rw, name="in_proj")
            for rw, b_run in zip(run_rows, b_runs)]
        small = _mm(xn, w_small, l, bias=b_small, out_dtype=F32, w_rows=(0, LANES), name="in_proj_gates")

        mg = small[:, :4 * mh].reshape(batch, nc, SCAN_CHUNK, 2, 2, mh)
        mgates = mg.transpose(0, 3, 1, 5, 4, 2).reshape(batch, 2, nc, 2 * mh, SCAN_CHUNK)
        h_m = _mlstm(proj_m, mgates, batch=batch, seq=seq, q_off=col(0), k_off=col(1), v_off=col(2),
                     heads=mh, dk=m_dk, dv=m_dv)
        y_a = _headnorm_gate(h_m, proj_m, col(3), mlstm_head_norm[l], heads=mh, dv=m_dv, act="sigmoid")

        up_pad = jnp.zeros((2, LANES, gh * g_dk), F32)
        for dr in range(2):
            r0 = 4 * mh + dr * GLA_RANK
            up_pad = up_pad.at[dr, r0:r0 + GLA_RANK, :].set(gla_decay_up[l, dr].astype(F32))
        gbias = gla_decay_bias[l].reshape(2, 1, gh * g_dk).astype(F32)
        h_g = _gla(proj_g, small, up_pad, gbias, batch=batch, seq=seq, q_off=col(5), k_off=col(6),
                   v_off=col(7), heads=gh, dk=g_dk, dv=g_dv)
        y_b = _headnorm_gate(h_g, proj_g, col(8), gla_head_norm[l], heads=gh, dv=g_dv, act="silu")

        y_c = _hyena_cm(proj_h, col(10), hyena_conv_w[l], hyena_conv_b[l], hyena_ffn_w1[l], hyena_ffn_b1[l],
                        hyena_ffn_w2[l], hyena_ffn_b2[l], hyena_ffn_w3[l], hyena_freq[l], hyena_skip[l],
                        batch=batch, seq=seq, ch=mix)

        merged = _merge((y_a, y_b, y_c), proj_h, col(11), wb16, l, d_model=d)
        h = _mm(merged, wo16, l, resid=h, out_dtype=F32, tn=1024, name="mixer_out")

        xn = _rmsnorm(h, g[1], BF16)
        memn = _rmsnorm(memf, g[2], BF16)
        q = _mm(xn, wq16, l, out_dtype=BF16, tn=1024, name="xattn_q")
        kv = _mm(memn, wkv16, l, out_dtype=BF16, tn=1024, name="xattn_kv")
        o = _xattn(q, kv, batch=batch, seq=seq, mem=mem_tokens, heads=XATTN_HEADS, hd=d // XATTN_HEADS)
        h = _mm(o, wxo16, l, resid=h, out_dtype=F32, tn=1024, name="xattn_out")

        xn = _rmsnorm(h, g[3], BF16)
        hid = _mm(xn, w1_16, l, act="relu2", out_dtype=BF16, tn=1024, name="mlp_up")
        h = _mm(hid, w2_16, l, resid=h, out_dtype=F32, tm=512, tn=256, name="mlp_down")
    return _rmsnorm(h, final_norm, x.dtype).reshape(batch, seq, d)
```
